```python
import math, functools
import jax, jax.numpy as jnp
from jax import lax
import numpy as np

D_MODEL = 1024
BATCH = 4
SEQ = 4096
DEPTH = 2
DEC_BATCH = 32
DEC_SEQ = 4
PAST_LEN = 8192
PAGE_SIZE = 128

HEAD_DIM = 64
H_A = 6
H_B = 6
H_C = 4
W_A = H_A * HEAD_DIM
W_B = H_B * HEAD_DIM
W_C = H_C * HEAD_DIM
MIX_WIDTH = W_A + W_B + W_C
LORA_W = 64
LORA_A = 64
LORA_G = 128
A_COLS = 3 * W_A + LORA_W + LORA_A + LORA_G
B_COLS = 3 * W_B
C_COLS = 4 * W_C
IN_COLS = A_COLS + B_COLS + C_COLS
DILATED = ((128, 1), (512, 4), (2048, 16))
WIN_MAX = max(w for w, _ in DILATED)
ROT_DIM = HEAD_DIM // 4
ROPE_THETA = 500000.0
RET_THETA = 10000.0
RET_CHUNK = 128
D_FF = -(-8 * D_MODEL // (3 * 256)) * 256
DEEPNORM_ALPHA = (2 * DEPTH) ** 0.25
DEEPNORM_BETA = (8 * DEPTH) ** -0.25
RWKV_DECAY_SCALE = math.exp(-0.5)
RWKV_GN_EPS = 64e-5
LN_EPS = 1e-5
NEG = -1e30

kernel_name = 'hybrid_rwkv7_dilated_retention_step'

F32 = jnp.float32


def _layer_norm(x, g, b):
    mu = jnp.mean(x, axis=-1, keepdims=True)
    var = jnp.mean(jnp.square(x - mu), axis=-1, keepdims=True)
    return (x - mu) * lax.rsqrt(var + LN_EPS) * g + b


def _head_norm(x, g, b, eps):
    h, e = x.shape[-2], x.shape[-1]
    mu = jnp.mean(x, axis=-1, keepdims=True)
    var = jnp.mean(jnp.square(x - mu), axis=-1, keepdims=True)
    return (x - mu) * lax.rsqrt(var + eps) * g.reshape(h, e) + b.reshape(h, e)


def _rope_inv_freq():
    return ROPE_THETA ** (-jnp.arange(0, ROT_DIM, 2, dtype=F32) / ROT_DIM)


def _ret_inv_freq():
    return 1.0 / (RET_THETA ** jnp.linspace(0.0, 1.0, HEAD_DIM // 2, dtype=F32))


def _apply_rotary(x, pos, inv_freq):
    half = inv_freq.shape[0]
    ang = pos[:, None] * inv_freq[None, :]
    cos = jnp.cos(ang)[None, :, None, :]
    sin = jnp.sin(ang)[None, :, None, :]
    x1 = x[..., :half]
    x2 = x[..., half:2 * half]
    return jnp.concatenate([x1 * cos - x2 * sin, x2 * cos + x1 * sin, x[..., 2 * half:]], axis=-1)


def _wkv7_scan(r, w, k, v, kk, a, s0):
    def step(s, inp):
        r_t, w_t, k_t, v_t, kk_t, a_t = inp
        sa = jnp.einsum('bhvk,bhk->bhv', s, kk_t)
        s = (s * w_t[:, :, None, :] - sa[..., None] * (kk_t * a_t)[:, :, None, :]
             + v_t[..., None] * k_t[:, :, None, :])
        return s, jnp.einsum('bhvk,bhk->bhv', s, r_t)
    seq = tuple(jnp.swapaxes(z, 0, 1) for z in (r, w, k, v, kk, a))
    s_fin, o = lax.scan(step, s0, seq)
    return jnp.swapaxes(o, 0, 1), s_fin


def _rwkv7_time_mix(h_a, shift_prev, s0, p):
    bsz, t, _ = h_a.shape
    prev = jnp.concatenate([shift_prev[:, None].astype(F32), h_a[:, :-1]], axis=1)
    xs = h_a + (prev - h_a) * p['rwkv_mu']
    r = xs[..., :W_A]
    k = xs[..., W_A:2 * W_A]
    v = xs[..., 2 * W_A:3 * W_A]
    o0 = 3 * W_A
    xw = xs[..., o0:o0 + LORA_W]
    xa = xs[..., o0 + LORA_W:o0 + LORA_W + LORA_A]
    xg = xs[..., o0 + LORA_W + LORA_A:]
    log_w = -RWKV_DECAY_SCALE * jax.nn.sigmoid(p['rwkv_w0'] + jnp.tanh(xw) @ p['rwkv_w_lora'])
    a = jax.nn.sigmoid(p['rwkv_a0'] + xa @ p['rwkv_a_lora'])
    g = jax.nn.sigmoid(xg) @ p['rwkv_g_lora']
    heads = lambda z: z.reshape(bsz, t, H_A, HEAD_DIM)
    kk = heads(k * p['rwkv_k_k'])
    kk = kk * lax.rsqrt(jnp.maximum(jnp.sum(kk * kk, axis=-1, keepdims=True), 1e-12))
    k = k * (1.0 + (a - 1.0) * p['rwkv_k_a'])
    r_h, k_h, v_h, a_h = heads(r), heads(k), heads(v), heads(a)
    o, s_fin = _wkv7_scan(r_h, jnp.exp(heads(log_w)), k_h, v_h, kk, a_h, s0.astype(F32))
    o = _head_norm(o, p['rwkv_gn_g'], p['rwkv_gn_b'], RWKV_GN_EPS)
    o = o + jnp.sum(r_h * k_h * p['rwkv_r_k'], axis=-1, keepdims=True) * v_h
    return o.reshape(bsz, t, W_A) * g, s_fin


def _dilated_branch_prompt(q, k, v, win, dil):
    bsz, t, h, e = q.shape
    blk = win // dil
    seg = dil * blk
    t_pad = -(-t // seg) * seg
    length = t_pad // dil
    nb = length // blk

    def to_blocks(z):
        z = jnp.pad(z, ((0, 0), (0, t_pad - t), (0, 0), (0, 0)))
        z = z.reshape(bsz, length, dil, h, e).transpose(0, 2, 1, 3, 4)
        return z.reshape(bsz, dil, nb, blk, h, e)

    def with_prev(z):
        prev = jnp.pad(z, ((0, 0), (0, 0), (1, 0), (0, 0), (0, 0), (0, 0)))[:, :, :-1]
        return jnp.concatenate([prev, z], axis=3)

    qb = to_blocks(q)
    kc = with_prev(to_blocks(k))
    vc = with_prev(to_blocks(v))
    s = jnp.einsum('brnqhe,brnkhe->brnhqk', qb, kc) * (e ** -0.5)
    qi = jnp.arange(blk)[:, None]
    ki = jnp.arange(2 * blk)[None, :]
    j = blk + qi - ki
    ok = (j >= 0) & (j <= blk)
    first = (jnp.arange(nb)[:, None, None] == 0) & (ki < blk)[None]
    ok = ok[None] & ~first
    s = jnp.where(ok[:, None], s, NEG)
    m = jnp.max(s, axis=-1, keepdims=True)
    pr = jnp.exp(s - m)
    l = jnp.sum(pr, axis=-1)
    o = jnp.einsum('brnhqk,brnkhe->brnqhe', pr, vc) / jnp.swapaxes(l, -1, -2)[..., None]
    lse = jnp.swapaxes(m[..., 0] + jnp.log(l), -1, -2)

    def from_blocks(z):
        rest = z.shape[4:]
        z = z.reshape((bsz, dil, length) + rest)
        z = jnp.moveaxis(z, 1, 2).reshape((bsz, t_pad) + rest)
        return z[:, :t]

    return from_blocks(o), from_blocks(lse)


def _dilated_branch_sample(q, k_all, v_all, win, dil, w_buf):
    t, e = q.shape[1], q.shape[-1]
    blk = win // dil
    self_idx = w_buf + jnp.arange(t)
    idx = self_idx[:, None] - dil * jnp.arange(blk + 1)[None, :]
    ok = idx >= 0
    idxc = jnp.maximum(idx, 0)
    kg = k_all[:, idxc]
    vg = v_all[:, idxc]
    s = jnp.einsum('bthe,btjhe->bthj', q, kg) * (e ** -0.5)
    s = jnp.where(ok[:, None, :], s, NEG)
    m = jnp.max(s, axis=-1, keepdims=True)
    pr = jnp.exp(s - m)
    l = jnp.sum(pr, axis=-1)
    o = jnp.einsum('bthj,btjhe->bthe', pr, vg) / l[..., None]
    return o, m[..., 0] + jnp.log(l)


def _merge_by_denominator(outs, lses):
    wts = jax.nn.softmax(jnp.stack(lses, axis=0), axis=0)
    return jnp.einsum('gbth,gbthe->bthe', wts, jnp.stack(outs, axis=0))


def _dilated_prompt(q, k, v):
    outs, lses = [], []
    for win, dil in DILATED:
        o, l = _dilated_branch_prompt(q, k, v, win, dil)
        outs.append(o)
        lses.append(l)
    w_keep = min(WIN_MAX, q.shape[1])
    return _merge_by_denominator(outs, lses), k[:, -w_keep:], v[:, -w_keep:]


def _dilated_sample(q, k, v, buf_k, buf_v):
    w_buf = buf_k.shape[1]
    k_all = jnp.concatenate([buf_k.astype(F32), k], axis=1)
    v_all = jnp.concatenate([buf_v.astype(F32), v], axis=1)
    outs, lses = [], []
    for win, dil in DILATED:
        o, l = _dilated_branch_sample(q, k_all, v_all, win, dil, w_buf)
        outs.append(o)
        lses.append(l)
    return _merge_by_denominator(outs, lses), k_all[:, -w_buf:], v_all[:, -w_buf:]


def _retention(q, k, v, r0):
    bsz, t, h, e = q.shape
    ch = RET_CHUNK if t % RET_CHUNK == 0 else t
    nc = t // ch
    log_gamma = jnp.log(1.0 - 2.0 ** (-5.0 - jnp.arange(h, dtype=F32)))
    qc = q.reshape(bsz, nc, ch, h, e)
    kc = k.reshape(bsz, nc, ch, h, e)
    vc = v.reshape(bsz, nc, ch, h, e)
    i = jnp.arange(ch, dtype=F32)
    rel = i[:, None] - i[None, :]
    dmask = jnp.where(rel >= 0, jnp.exp(log_gamma[:, None, None] * jnp.maximum(rel, 0.0)), 0.0)
    att = jnp.einsum('bcihe,bcjhe->bchij', qc, kc) * dmask
    o_intra = jnp.einsum('bchij,bcjhf->bcihf', att, vc)
    kdec = jnp.exp(log_gamma[None, :] * (ch - 1.0 - i)[:, None])
    kv = jnp.einsum('bcjhe,bcjhf->bchef', kc * kdec[:, :, None], vc)
    chunk_decay = jnp.exp(log_gamma * ch)[None, :, None, None]

    def step(r, kv_c):
        return r * chunk_decay + kv_c, r

    r_fin, r_prev = lax.scan(step, r0.astype(F32), jnp.swapaxes(kv, 0, 1))
    r_prev = jnp.swapaxes(r_prev, 0, 1)
    qdec = jnp.exp(log_gamma[None, :] * (i + 1.0)[:, None])
    o_cross = jnp.einsum('bcihe,bchef->bcihf', qc * qdec[:, :, None], r_prev)
    return (o_intra + o_cross).reshape(bsz, t, h, e), r_fin


def _trunk_layer(x, shift_prev, wkv0, ret0, pos0, attn_fn, p):
    bsz, t, _ = x.shape
    xf = x.astype(F32)
    h = jnp.matmul(x, p['w_in']).astype(F32)
    h_a = h[..., :A_COLS]
    h_b = h[..., A_COLS:A_COLS + B_COLS]
    h_c = h[..., A_COLS + B_COLS:]
    pos = pos0 + jnp.arange(t, dtype=F32)

    o_a, wkv_new = _rwkv7_time_mix(h_a, shift_prev, wkv0, p)
    shift_new = h_a[:, -1]

    q_b, k_b, v_b = [z.reshape(bsz, t, H_B, HEAD_DIM) for z in jnp.split(h_b, 3, axis=-1)]
    inv_b = _rope_inv_freq()
    q_b = _apply_rotary(q_b, pos, inv_b)
    k_b = _apply_rotary(k_b, pos, inv_b)
    o_b, k_keep, v_keep = attn_fn(q_b, k_b, v_b)

    q_c, k_c, v_c, g_c = jnp.split(h_c, 4, axis=-1)
    inv_c = _ret_inv_freq()
    q_c = _apply_rotary(q_c.reshape(bsz, t, H_C, HEAD_DIM), pos, inv_c)
    k_c = _apply_rotary(k_c.reshape(bsz, t, H_C, HEAD_DIM), pos, inv_c) * (HEAD_DIM ** -0.5)
    o_c, ret_new = _retention(q_c, k_c, v_c.reshape(bsz, t, H_C, HEAD_DIM), ret0)
    o_c = _head_norm(o_c, p['ret_gn_g'], p['ret_gn_b'], LN_EPS).reshape(bsz, t, W_C) * jax.nn.silu(g_c)

    mix = jnp.concatenate([o_a, o_b.reshape(bsz, t, W_B), o_c], axis=-1) @ p['w_out']
    x1 = _layer_norm(DEEPNORM_ALPHA * xf + mix, p['ln1_g'], p['ln1_b'])
    ffn = (jax.nn.silu(x1 @ p['w_ffn_gate']) * (x1 @ p['w_ffn_up'])) @ p['w_ffn_down']
    x2 = _layer_norm(DEEPNORM_ALPHA * x1 + ffn, p['ln2_g'], p['ln2_b'])
    return x2.astype(x.dtype), (shift_new, wkv_new, k_keep, v_keep, ret_new)


def setup_inputs(seed: int = 0) -> dict:
    key = jax.random.key(seed)
    ks = jax.random.split(key, 32)
    w_buf = min(WIN_MAX, PAST_LEN)

    def nrm(k, shape, s):
        return s * jax.random.normal(k, shape, F32)

    return {
        'x_prompt': nrm(ks[0], (BATCH, SEQ, D_MODEL), 1.0),
        'x_sample': nrm(ks[1], (DEC_BATCH, DEC_SEQ, D_MODEL), 1.0),
        'state_rwkv_shift': nrm(ks[2], (DEPTH, DEC_BATCH, A_COLS), 1.0),
        'state_rwkv_wkv': nrm(ks[3], (DEPTH, DEC_BATCH, H_A, HEAD_DIM, HEAD_DIM), 0.5),
        'cache_win_k': nrm(ks[4], (DEPTH, DEC_BATCH, w_buf, H_B, HEAD_DIM), 1.0),
        'cache_win_v': nrm(ks[5], (DEPTH, DEC_BATCH, w_buf, H_B, HEAD_DIM), 1.0),
        'state_ret': nrm(ks[6], (DEPTH, DEC_BATCH, H_C, HEAD_DIM, HEAD_DIM), 2.0),
        'w_in': nrm(ks[7], (DEPTH, D_MODEL, IN_COLS), D_MODEL ** -0.5),
        'rwkv_mu': jax.random.uniform(ks[8], (DEPTH, A_COLS), F32),
        'rwkv_w0': nrm(ks[9], (DEPTH, W_A), 1.0),
        'rwkv_w_lora': nrm(ks[10], (DEPTH, LORA_W, W_A), 0.1),
        'rwkv_a0': nrm(ks[11], (DEPTH, W_A), 0.5),
        'rwkv_a_lora': nrm(ks[12], (DEPTH, LORA_A, W_A), 0.1),
        'rwkv_g_lora': nrm(ks[13], (DEPTH, LORA_G, W_A), LORA_G ** -0.5),
        'rwkv_k_k': 0.85 + nrm(ks[14], (DEPTH, W_A), 0.05),
        'rwkv_k_a': 1.0 + nrm(ks[15], (DEPTH, W_A), 0.05),
        'rwkv_r_k': nrm(ks[16], (DEPTH, H_A, HEAD_DIM), 0.1),
        'rwkv_gn_g': 1.0 + nrm(ks[17], (DEPTH, W_A), 0.02),
        'rwkv_gn_b': nrm(ks[18], (DEPTH, W_A), 0.02),
        'ret_gn_g': 1.0 + nrm(ks[19], (DEPTH, W_C), 0.02),
        'ret_gn_b': nrm(ks[20], (DEPTH, W_C), 0.02),
        'w_out': nrm(ks[21], (DEPTH, MIX_WIDTH, D_MODEL), MIX_WIDTH ** -0.5 * DEEPNORM_BETA),
        'ln1_g': 1.0 + nrm(ks[22], (DEPTH, D_MODEL), 0.02),
        'ln1_b': nrm(ks[23], (DEPTH, D_MODEL), 0.02),
        'w_ffn_gate': nrm(ks[24], (DEPTH, D_MODEL, D_FF), D_MODEL ** -0.5),
        'w_ffn_up': nrm(ks[25], (DEPTH, D_MODEL, D_FF), D_MODEL ** -0.5),
        'w_ffn_down': nrm(ks[26], (DEPTH, D_FF, D_MODEL), D_FF ** -0.5 * DEEPNORM_BETA),
        'ln2_g': 1.0 + nrm(ks[27], (DEPTH, D_MODEL), 0.02),
        'ln2_b': nrm(ks[28], (DEPTH, D_MODEL), 0.02),
    }


def reference(x_prompt, x_sample, state_rwkv_shift, state_rwkv_wkv, cache_win_k, cache_win_v, state_ret,
              w_in, rwkv_mu, rwkv_w0, rwkv_w_lora, rwkv_a0, rwkv_a_lora, rwkv_g_lora, rwkv_k_k, rwkv_k_a,
              rwkv_r_k, rwkv_gn_g, rwkv_gn_b, ret_gn_g, ret_gn_b, w_out, ln1_g, ln1_b,
              w_ffn_gate, w_ffn_up, w_ffn_down, ln2_g, ln2_b):
    xp, xs = x_prompt, x_sample
    bp = x_prompt.shape[0]
    p_states, s_states = [], []
    for l in range(DEPTH):
        p = {
            'w_in': w_in[l], 'rwkv_mu': rwkv_mu[l], 'rwkv_w0': rwkv_w0[l], 'rwkv_w_lora': rwkv_w_lora[l],
            'rwkv_a0': rwkv_a0[l], 'rwkv_a_lora': rwkv_a_lora[l], 'rwkv_g_lora': rwkv_g_lora[l],
            'rwkv_k_k': rwkv_k_k[l], 'rwkv_k_a': rwkv_k_a[l], 'rwkv_r_k': rwkv_r_k[l],
            'rwkv_gn_g': rwkv_gn_g[l], 'rwkv_gn_b': rwkv_gn_b[l], 'ret_gn_g': ret_gn_g[l], 'ret_gn_b': ret_gn_b[l],
            'w_out': w_out[l], 'ln1_g': ln1_g[l], 'ln1_b': ln1_b[l], 'w_ffn_gate': w_ffn_gate[l],
            'w_ffn_up': w_ffn_up[l], 'w_ffn_down': w_ffn_down[l], 'ln2_g': ln2_g[l], 'ln2_b': ln2_b[l],
        }
        xp, sp = _trunk_layer(
            xp,
            jnp.zeros((bp, A_COLS), F32),
            jnp.zeros((bp, H_A, HEAD_DIM, HEAD_DIM), F32),
            jnp.zeros((bp, H_C, HEAD_DIM, HEAD_DIM), F32),
            0.0, _dilated_prompt, p)
        xs, ss = _trunk_layer(
            xs, state_rwkv_shift[l], state_rwkv_wkv[l], state_ret[l], float(PAST_LEN),
            functools.partial(_dilated_sample, buf_k=cache_win_k[l], buf_v=cache_win_v[l]), p)
        p_states.append(sp)
        s_states.append(ss)
    p_shift = jnp.stack([s[0] for s in p_states])
    p_wkv = jnp.stack([s[1] for s in p_states])
    p_win_k = jnp.stack([s[2] for s in p_states])
    p_win_v = jnp.stack([s[3] for s in p_states])
    p_ret = jnp.stack([s[4] for s in p_states])
    s_shift = jnp.stack([s[0] for s in s_states])
    s_wkv = jnp.stack([s[1] for s in s_states])
    s_win_k = jnp.stack([s[2] for s in s_states])
    s_win_v = jnp.stack([s[3] for s in s_states])
    s_ret = jnp.stack([s[4] for s in s_states])
    return (xp, xs, p_shift, p_wkv, p_win_k, p_win_v, p_ret, s_shift, s_wkv, s_win_k, s_win_v, s_ret)
```

```python
import functools
import math

import jax
import jax.numpy as jnp
from jax import lax
from jax.experimental import pallas as pl
from jax.experimental.pallas import tpu as pltpu

F32 = jnp.float32
BF16 = jnp.bfloat16
HI = lax.Precision.HIGHEST

PAST_LEN = 8192
HEAD_DIM = 64
H_A, H_B, H_C = 6, 6, 4
W_A, W_B, W_C = H_A * HEAD_DIM, H_B * HEAD_DIM, H_C * HEAD_DIM
LORA_W, LORA_A, LORA_G = 64, 64, 128
A_COLS = 3 * W_A + LORA_W + LORA_A + LORA_G
B_COLS = 3 * W_B
C_COLS = 4 * W_C
DILATED = ((128, 1), (512, 4), (2048, 16))
WIN_MAX = max(w for w, _ in DILATED)
ATT_BLK = 128
ROT_DIM = HEAD_DIM // 4
ROPE_THETA = 500000.0
RET_THETA = 10000.0
RET_CHUNK = 128
RWKV_CHUNK = 64
RWKV_DECAY_SCALE = math.exp(-0.5)
RWKV_GN_EPS = 64e-5
LN_EPS = 1e-5
NEG = -1e30
LANES = 128
VMEM_LIMIT = 56 * 1024 * 1024


def _params(sem, vmem=VMEM_LIMIT):
    return pltpu.CompilerParams(dimension_semantics=sem, vmem_limit_bytes=vmem)


def _dot(a, b, precision=None):
    return jnp.dot(a, b, preferred_element_type=F32, precision=precision)


def _dot_nt(a, b, precision=None):
    return lax.dot_general(a, b, (((1,), (1,)), ((), ())), preferred_element_type=F32, precision=precision)


def _dot_tn(a, b, precision=None):
    return lax.dot_general(a, b, (((0,), (0,)), ((), ())), preferred_element_type=F32, precision=precision)


def _sigmoid(x):
    return 1.0 / (1.0 + jnp.exp(-x))


def _layer_norm(x, g, b, eps):
    mu = jnp.mean(x, axis=-1, keepdims=True)
    xc = x - mu
    var = jnp.mean(xc * xc, axis=-1, keepdims=True)
    return xc * lax.rsqrt(var + eps) * g + b


def _rot_tables(pos, inv_freq, rot_width):
    half = rot_width // 2
    ang = pos[:, None] * inv_freq[None, :]
    cos, sin = jnp.cos(ang), jnp.sin(ang)
    lane = jnp.arange(LANES) % HEAD_DIM
    idx = lane % half
    cos_l = jnp.where(lane[None, :] < rot_width, cos[:, idx], 1.0)
    sin_l = sin[:, idx]
    sin_up = jnp.where(lane[None, :] < half, -sin_l, 0.0)
    sin_dn = jnp.where((lane[None, :] >= half) & (lane[None, :] < rot_width), sin_l, 0.0)
    return cos_l.astype(F32), sin_up.astype(F32), sin_dn.astype(F32)


def _rotate_slab(x, cos, sin_up, sin_dn, half):
    up = pltpu.roll(x, LANES - half, 1)
    dn = pltpu.roll(x, half, 1)
    return x * cos + up * sin_up + dn * sin_dn


def _in_proj_kernel(x_ref, w_ref, cb_ref, ub_ref, db_ref, cc_ref, uc_ref, dc_ref,
                    ha_ref, hb_ref, hc_ref):
    h = _dot(x_ref[...].astype(BF16), w_ref[...])
    ha_ref[...] = h[:, :A_COLS]
    cb, ub, db = cb_ref[...], ub_ref[...], db_ref[...]
    for j in range(B_COLS // LANES):
        slab = h[:, A_COLS + j * LANES:A_COLS + (j + 1) * LANES]
        if j < 2 * W_B // LANES:
            slab = _rotate_slab(slab, cb, ub, db, ROT_DIM // 2)
        hb_ref[:, j * LANES:(j + 1) * LANES] = slab
    cc, uc, dc = cc_ref[...], uc_ref[...], dc_ref[...]
    c0 = A_COLS + B_COLS
    for j in range(C_COLS // LANES):
        slab = h[:, c0 + j * LANES:c0 + (j + 1) * LANES]
        if j < 2 * W_C // LANES:
            slab = _rotate_slab(slab, cc, uc, dc, HEAD_DIM // 2)
        if W_C // LANES <= j < 2 * W_C // LANES:
            slab = slab * (HEAD_DIM ** -0.5)
        hc_ref[:, j * LANES:(j + 1) * LANES] = slab


def _in_proj(x2d, w_bf16, tabs_b, tabs_c, bm):
    n, d = x2d.shape
    in_cols = w_bf16.shape[1]
    t_rows = tabs_b[0].shape[0]
    nt = t_rows // bm
    tab_spec = pl.BlockSpec((bm, LANES), lambda i: (i % nt, 0))
    row = lambda w: pl.BlockSpec((bm, w), lambda i: (i, 0))
    return pl.pallas_call(
        _in_proj_kernel,
        grid=(n // bm,),
        in_specs=[row(d), pl.BlockSpec((d, in_cols), lambda i: (0, 0))] + [tab_spec] * 6,
        out_specs=[row(A_COLS), row(B_COLS), row(C_COLS)],
        out_shape=[jax.ShapeDtypeStruct((n, A_COLS), F32),
                   jax.ShapeDtypeStruct((n, B_COLS), F32),
                   jax.ShapeDtypeStruct((n, C_COLS), F32)],
        compiler_params=_params(("parallel",)),
        name="in_proj",
    )(x2d, w_bf16, *tabs_b, *tabs_c)


def _rwkv_kernel(ha_ref, shift_ref, s0_ref, mu_ref, w0_ref, wl_ref, a0_ref, al_ref, gl_ref,
                 kk_ref, ka_ref, rk_ref, gng_ref, gnb_ref, o_ref, sfin_ref,
                 hp_ref, st_ref, *, t_valid):
    c = pl.program_id(1)
    ch = RWKV_CHUNK

    @pl.when(c == 0)
    def _():
        hp_ref[0:8, :] = jnp.broadcast_to(shift_ref[0], (8, A_COLS))
        st_ref[...] = s0_ref[0]

    h = ha_ref[0]
    hp_ref[8:8 + ch, :] = h
    prev = hp_ref[7:7 + ch, :]
    hp_ref[0:8, :] = h[ch - 8:ch, :]
    xs = h + (prev - h) * mu_ref[...]

    r = xs[:, :W_A]
    k = xs[:, W_A:2 * W_A]
    v = xs[:, 2 * W_A:3 * W_A]
    o0 = 3 * W_A
    xw = xs[:, o0:o0 + LORA_W]
    xa = xs[:, o0 + LORA_W:o0 + LORA_W + LORA_A]
    xg = xs[:, o0 + LORA_W + LORA_A:]
    log_w = -RWKV_DECAY_SCALE * _sigmoid(w0_ref[...] + _dot(jnp.tanh(xw).astype(BF16), wl_ref[...]))
    a = _sigmoid(a0_ref[...] + _dot(xa.astype(BF16), al_ref[...]))
    g = _dot(_sigmoid(xg).astype(BF16), gl_ref[...])
    kk = k * kk_ref[...]
    kmod = k * (1.0 + (a - 1.0) * ka_ref[...])

    row = lax.broadcasted_iota(jnp.int32, (ch, ch), 0)
    col = lax.broadcasted_iota(jnp.int32, (ch, ch), 1)
    if t_valid is not None:
        tok = c * ch + lax.broadcasted_iota(jnp.int32, (ch, 1), 0)
        valid = tok < t_valid
        log_w = jnp.where(valid, log_w, 0.0)
        kk = jnp.where(valid, kk, 0.0)
        kmod = jnp.where(valid, kmod, 0.0)

    lower = (col <= row).astype(F32)
    cum = _dot(lower, log_w, HI)
    p_in = jnp.exp(cum)
    p_ex = jnp.exp(cum - log_w)
    p_inv = jnp.exp(-cum)
    to_end = jnp.exp(cum[ch - 1:ch, :] - cum)
    p_end = p_in[ch - 1:ch, :]
    strict = col < row
    incl = col <= row
    diag = col == row

    for hd in range(H_A):
        sl = slice(hd * HEAD_DIM, (hd + 1) * HEAD_DIM)
        kk_h = kk[:, sl]
        kk_h = kk_h * lax.rsqrt(jnp.maximum(jnp.sum(kk_h * kk_h, axis=-1, keepdims=True), 1e-12))
        b_h = kk_h * a[:, sl]
        kkt = kk_h * p_ex[:, sl]
        bt = b_h * p_inv[:, sl]
        kt = kmod[:, sl] * p_inv[:, sl]
        rt = r[:, sl] * p_in[:, sl]
        v_h = v[:, sl]
        aa = _dot_nt(jnp.concatenate([kkt, rt], axis=0), jnp.concatenate([bt, kt], axis=0), HI)
        a_kb = jnp.where(strict, aa[:ch, :ch], 0.0)
        a_kk = jnp.where(strict, aa[:ch, ch:], 0.0)
        a_rb = jnp.where(incl, aa[ch:, :ch], 0.0)
        a_rk = jnp.where(incl, aa[ch:, ch:], 0.0)
        y = jnp.concatenate([kkt, _dot(a_kk, v_h, HI)], axis=1)
        y = y - _dot(a_kb, y, HI)
        pw = a_kb
        for _ in range(int(math.log2(ch)) - 1):
            pw = _dot(pw, pw, HI)
            y = y + _dot(pw, y, HI)
        w1 = y[:, :HEAD_DIM]
        u0 = -y[:, HEAD_DIM:]
        z = _dot(a_rb, jnp.concatenate([w1, u0], axis=1), HI)
        q_eff = rt - z[:, :HEAD_DIM]
        o_loc = z[:, HEAD_DIM:] + _dot(a_rk, v_h, HI)
        bd = b_h * to_end[:, sl]
        kd = kmod[:, sl] * to_end[:, sl]
        gz = _dot_tn(bd, jnp.concatenate([w1, u0], axis=1), HI)
        g_mat = jnp.where(diag, jnp.broadcast_to(p_end[:, sl], (HEAD_DIM, HEAD_DIM)), 0.0) - gz[:, :HEAD_DIM]
        st = st_ref[hd]
        o_h = _dot(q_eff, st, HI) + o_loc
        st_ref[hd] = _dot(g_mat, st, HI) + gz[:, HEAD_DIM:] + _dot_tn(kd, v_h, HI)

        mu_o = jnp.mean(o_h, axis=-1, keepdims=True)
        oc = o_h - mu_o
        var = jnp.mean(oc * oc, axis=-1, keepdims=True)
        o_n = oc * lax.rsqrt(var + RWKV_GN_EPS) * gng_ref[:, sl] + gnb_ref[:, sl]
        bonus = jnp.sum(r[:, sl] * kmod[:, sl] * rk_ref[:, sl], axis=-1, keepdims=True)
        o_ref[0, :, sl] = (o_n + bonus * v_h) * g[:, sl]

    @pl.when(c == pl.num_programs(1) - 1)
    def _():
        sfin_ref[0] = st_ref[...]


def _rwkv(ha, shift_prev, s0_t, p, t_valid):
    b, tp, _ = ha.shape
    nc = tp // RWKV_CHUNK
    vec = lambda w: pl.BlockSpec((1, w), lambda i, c: (0, 0))
    mat = lambda r, w: pl.BlockSpec((r, w), lambda i, c: (0, 0))
    kern = functools.partial(_rwkv_kernel, t_valid=None if t_valid == tp else t_valid)
    return pl.pallas_call(
        kern,
        grid=(b, nc),
        in_specs=[pl.BlockSpec((1, RWKV_CHUNK, A_COLS), lambda i, c: (i, c, 0)),
                  pl.BlockSpec((1, 1, A_COLS), lambda i, c: (i, 0, 0)),
                  pl.BlockSpec((1, H_A, HEAD_DIM, HEAD_DIM), lambda i, c: (i, 0, 0, 0)),
                  vec(A_COLS), vec(W_A), mat(LORA_W, W_A), vec(W_A), mat(LORA_A, W_A), mat(LORA_G, W_A),
                  vec(W_A), vec(W_A), vec(W_A), vec(W_A), vec(W_A)],
        out_specs=[pl.BlockSpec((1, RWKV_CHUNK, W_A), lambda i, c: (i, c, 0)),
                   pl.BlockSpec((1, H_A, HEAD_DIM, HEAD_DIM), lambda i, c: (i, 0, 0, 0))],
        out_shape=[jax.ShapeDtypeStruct((b, tp, W_A), F32),
                   jax.ShapeDtypeStruct((b, H_A, HEAD_DIM, HEAD_DIM), F32)],
        scratch_shapes=[pltpu.VMEM((RWKV_CHUNK + 8, A_COLS), F32),
                        pltpu.VMEM((H_A, HEAD_DIM, HEAD_DIM), F32)],
        compiler_params=_params(("parallel", "arbitrary")),
        name="rwkv7",
    )(ha, shift_prev[:, None, :], s0_t,
      p['rwkv_mu'][None], p['rwkv_w0'][None], p['rwkv_w_lora'].astype(BF16), p['rwkv_a0'][None],
      p['rwkv_a_lora'].astype(BF16), p['rwkv_g_lora'].astype(BF16), p['rwkv_k_k'][None],
      p['rwkv_k_a'][None], p['rwkv_r_k'].reshape(1, W_A), p['rwkv_gn_g'][None], p['rwkv_gn_b'][None])


def _attn_branch_kernel(q_ref, k_ref, v_ref, kp_ref, vp_ref, o_ref, lse_ref):
    not_first = pl.program_id(2) > 0
    blk = ATT_BLK
    qi = lax.broadcasted_iota(jnp.int32, (blk, blk), 0)
    ki = lax.broadcasted_iota(jnp.int32, (blk, blk), 1)
    ok_prev = (ki >= qi) & not_first
    ok_own = ki <= qi
    for hd in range(H_B):
        sl = slice(hd * HEAD_DIM, (hd + 1) * HEAD_DIM)
        q = q_ref[0, :, sl].astype(BF16)
        s_prev = _dot_nt(q, kp_ref[0, :, sl].astype(BF16)) * (HEAD_DIM ** -0.5)
        s_own = _dot_nt(q, k_ref[0, :, sl].astype(BF16)) * (HEAD_DIM ** -0.5)
        s_prev = jnp.where(ok_prev, s_prev, NEG)
        s_own = jnp.where(ok_own, s_own, NEG)
        m = jnp.maximum(jnp.max(s_prev, axis=-1, keepdims=True), jnp.max(s_own, axis=-1, keepdims=True))
        p_prev = jnp.exp(s_prev - m)
        p_own = jnp.exp(s_own - m)
        l = jnp.sum(p_prev, axis=-1, keepdims=True) + jnp.sum(p_own, axis=-1, keepdims=True)
        acc = _dot(p_prev.astype(BF16), vp_ref[0, :, sl].astype(BF16)) + \
            _dot(p_own.astype(BF16), v_ref[0, :, sl].astype(BF16))
        o_ref[0, :, sl] = acc / l
        lse_ref[0, :, sl] = jnp.broadcast_to(m + jnp.log(l), (blk, HEAD_DIM))


def _attn_branch(hb, dil):
    b, t, _ = hb.shape
    length = t // dil
    nb = length // ATT_BLK
    hb_v = hb.reshape(b, length, dil * B_COLS)
    blk = (1, ATT_BLK, W_B)
    own = lambda part: pl.BlockSpec(blk, lambda i, r, n: (i, n, 3 * r + part))
    prev = lambda part: pl.BlockSpec(blk, lambda i, r, n: (i, jnp.maximum(n - 1, 0), 3 * r + part))
    out_spec = pl.BlockSpec(blk, lambda i, r, n: (i, n, r))
    o, lse = pl.pallas_call(
        _attn_branch_kernel,
        grid=(b, dil, nb),
        in_specs=[own(0), own(1), own(2), prev(1), prev(2)],
        out_specs=[out_spec, out_spec],
        out_shape=[jax.ShapeDtypeStruct((b, length, dil * W_B), F32)] * 2,
        compiler_params=_params(("parallel", "parallel", "arbitrary")),
        name=f"attn_branch_d{dil}",
    )(hb_v, hb_v, hb_v, hb_v, hb_v)
    return o.reshape(b, t, W_B), lse.reshape(b, t, W_B)


def _attn_merge_kernel(o1, l1, o2, l2, o3, l3, out_ref):
    a1, a2, a3 = l1[...], l2[...], l3[...]
    m = jnp.maximum(jnp.maximum(a1, a2), a3)
    e1, e2, e3 = jnp.exp(a1 - m), jnp.exp(a2 - m), jnp.exp(a3 - m)
    tot = e1 + e2 + e3
    out_ref[...] = (e1 / tot) * o1[...] + (e2 / tot) * o2[...] + (e3 / tot) * o3[...]


def _attn_merge(parts, bm):
    n = parts[0].shape[0]
    spec = pl.BlockSpec((bm, W_B), lambda i: (i, 0))
    return pl.pallas_call(
        _attn_merge_kernel,
        grid=(n // bm,),
        in_specs=[spec] * 6,
        out_specs=spec,
        out_shape=jax.ShapeDtypeStruct((n, W_B), F32),
        compiler_params=_params(("parallel",)),
        name="attn_merge",
    )(*parts)


def _attn_sample_kernel(hb_ref, ck_ref, cv_ref, o_ref, nk_ref, nv_ref, *, t_new, rows):
    w_buf = ck_ref.shape[2]
    q_all = hb_ref[0][:, :W_B]
    k_new = hb_ref[0][:, W_B:2 * W_B]
    v_new = hb_ref[0][:, 2 * W_B:]
    nk_ref[0, 0, 0:w_buf - t_new, :] = ck_ref[0, 0, t_new:w_buf, :]
    nv_ref[0, 0, 0:w_buf - t_new, :] = cv_ref[0, 0, t_new:w_buf, :]
    nk_ref[0, 0, w_buf - t_new:w_buf, :] = k_new[0:t_new, :]
    nv_ref[0, 0, w_buf - t_new:w_buf, :] = v_new[0:t_new, :]

    qt = lax.broadcasted_iota(jnp.int32, (rows, w_buf), 0)
    kc = lax.broadcasted_iota(jnp.int32, (rows, w_buf), 1)
    d_cache = w_buf + qt - kc
    qn = lax.broadcasted_iota(jnp.int32, (rows, rows), 0)
    kn = lax.broadcasted_iota(jnp.int32, (rows, rows), 1)
    d_new = qn - kn
    new_valid = (d_new >= 0) & (kn < t_new)
    for hd in range(H_B):
        sl = slice(hd * HEAD_DIM, (hd + 1) * HEAD_DIM)
        q = q_all[:, sl].astype(BF16)
        kc_h = ck_ref[0, 0, :, sl].astype(BF16)
        vc_h = cv_ref[0, 0, :, sl].astype(BF16)
        kn_h = k_new[:, sl].astype(BF16)
        vn_h = v_new[:, sl].astype(BF16)
        s_c = _dot_nt(q, kc_h) * (HEAD_DIM ** -0.5)
        s_n = _dot_nt(q, kn_h) * (HEAD_DIM ** -0.5)
        outs, lses = [], []
        for win, dil in DILATED:
            ok_c = ((d_cache & (dil - 1)) == 0) & (d_cache <= win)
            ok_n = new_valid & ((d_new & (dil - 1)) == 0) & (d_new <= win)
            sc = jnp.where(ok_c, s_c, NEG)
            sn = jnp.where(ok_n, s_n, NEG)
            m = jnp.maximum(jnp.max(sc, axis=-1, keepdims=True), jnp.max(sn, axis=-1, keepdims=True))
            pc = jnp.exp(sc - m)
            pn = jnp.exp(sn - m)
            l = jnp.sum(pc, axis=-1, keepdims=True) + jnp.sum(pn, axis=-1, keepdims=True)
            outs.append((_dot(pc.astype(BF16), vc_h) + _dot(pn.astype(BF16), vn_h)) / l)
            lses.append(m + jnp.log(l))
        m3 = jnp.maximum(jnp.maximum(lses[0], lses[1]), lses[2])
        es = [jnp.exp(x - m3) for x in lses]
        tot = es[0] + es[1] + es[2]
        o_ref[0, :, sl] = (es[0] / tot) * outs[0] + (es[1] / tot) * outs[1] + (es[2] / tot) * outs[2]


def _attn_sample(hb, cache_k, cache_v, layer, t_new):
    b, rows, _ = hb.shape
    depth, _, w_buf, _ = cache_k.shape
    cspec = pl.BlockSpec((1, 1, w_buf, W_B), lambda i: (layer, i, 0, 0))
    ospec = pl.BlockSpec((1, 1, w_buf, W_B), lambda i: (0, i, 0, 0))
    kern = functools.partial(_attn_sample_kernel, t_new=t_new, rows=rows)
    o, nk, nv = pl.pallas_call(
        kern,
        grid=(b,),
        in_specs=[pl.BlockSpec((1, rows, B_COLS), lambda i: (i, 0, 0)), cspec, cspec],
        out_specs=[pl.BlockSpec((1, rows, W_B), lambda i: (i, 0, 0)), ospec, ospec],
        out_shape=[jax.ShapeDtypeStruct((b, rows, W_B), F32),
                   jax.ShapeDtypeStruct((1, b, w_buf, W_B), F32),
                   jax.ShapeDtypeStruct((1, b, w_buf, W_B), F32)],
        compiler_params=_params(("parallel",)),
        name="attn_sample",
    )(hb, cache_k, cache_v)
    return o, nk[0], nv[0]


def _retention_kernel(q_ref, k_ref, v_ref, g_ref, r0_ref, gng_ref, gnb_ref, o_ref, rfin_ref, st_ref,
                      *, ch):
    c = pl.program_id(1)
    rows = q_ref.shape[1]

    @pl.when(c == 0)
    def _():
        st_ref[...] = r0_ref[0]

    qi = lax.broadcasted_iota(jnp.int32, (rows, rows), 0)
    kj = lax.broadcasted_iota(jnp.int32, (rows, rows), 1)
    rel = (qi - kj).astype(F32)
    pos = lax.broadcasted_iota(jnp.int32, (rows, 1), 0).astype(F32)
    gate = g_ref[0]
    gate = gate * _sigmoid(gate)
    for hd in range(H_C):
        sl = slice(hd * HEAD_DIM, (hd + 1) * HEAD_DIM)
        log_gamma = math.log(1.0 - 2.0 ** (-5.0 - hd))
        dmask = jnp.where(rel >= 0, jnp.exp(log_gamma * jnp.maximum(rel, 0.0)), 0.0)
        kdec = jnp.exp(log_gamma * (ch - 1.0 - pos))
        qdec = jnp.exp(log_gamma * (pos + 1.0))
        q = q_ref[0, :, sl]
        k = k_ref[0, :, sl]
        v = v_ref[0, :, sl].astype(BF16)
        att = _dot_nt(q.astype(BF16), k.astype(BF16)) * dmask
        st = st_ref[hd]
        o = _dot(att.astype(BF16), v) + _dot((q * qdec).astype(BF16), st.astype(BF16))
        kv = _dot_tn((k * kdec).astype(BF16), v)
        st_ref[hd] = st * math.exp(log_gamma * ch) + kv
        mu = jnp.mean(o, axis=-1, keepdims=True)
        oc = o - mu
        var = jnp.mean(oc * oc, axis=-1, keepdims=True)
        o_ref[0, :, sl] = (oc * lax.rsqrt(var + LN_EPS) * gng_ref[:, sl] + gnb_ref[:, sl]) * gate[:, sl]

    @pl.when(c == pl.num_programs(1) - 1)
    def _():
        rfin_ref[0] = st_ref[...]


def _retention(hc, r0, gn_g, gn_b, rows, ch):
    b, tp, _ = hc.shape
    nc = tp // rows
    part = lambda j: pl.BlockSpec((1, rows, W_C), lambda i, c: (i, c, j))
    st_spec = pl.BlockSpec((1, H_C, HEAD_DIM, HEAD_DIM), lambda i, c: (i, 0, 0, 0))
    vec = pl.BlockSpec((1, W_C), lambda i, c: (0, 0))
    return pl.pallas_call(
        functools.partial(_retention_kernel, ch=ch),
        grid=(b, nc),
        in_specs=[part(0), part(1), part(2), part(3), st_spec, vec, vec],
        out_specs=[pl.BlockSpec((1, rows, W_C), lambda i, c: (i, c, 0)), st_spec],
        out_shape=[jax.ShapeDtypeStruct((b, tp, W_C), F32),
                   jax.ShapeDtypeStruct((b, H_C, HEAD_DIM, HEAD_DIM), F32)],
        scratch_shapes=[pltpu.VMEM((H_C, HEAD_DIM, HEAD_DIM), F32)],
        compiler_params=_params(("parallel", "arbitrary")),
        name="retention",
    )(hc, hc, hc, hc, r0, gn_g[None], gn_b[None])


def _out_proj_kernel(oa_ref, ob_ref, oc_ref, x_ref, w_ref, g_ref, b_ref, y_ref, *, alpha):
    mix = _dot(oa_ref[...].astype(BF16), w_ref[0:W_A, :])
    mix += _dot(ob_ref[...].astype(BF16), w_ref[W_A:W_A + W_B, :])
    mix += _dot(oc_ref[...].astype(BF16), w_ref[W_A + W_B:, :])
    y_ref[...] = _layer_norm(alpha * x_ref[...] + mix, g_ref[...], b_ref[...], LN_EPS)


def _out_proj(oa, ob, oc, x2d, w_bf16, g, b, alpha, bm):
    n, d = x2d.shape
    row = lambda w: pl.BlockSpec((bm, w), lambda i: (i, 0))
    vec = pl.BlockSpec((1, d), lambda i: (0, 0))
    return pl.pallas_call(
        functools.partial(_out_proj_kernel, alpha=alpha),
        grid=(n // bm,),
        in_specs=[row(W_A), row(W_B), row(W_C), row(d), pl.BlockSpec(w_bf16.shape, lambda i: (0, 0)), vec, vec],
        out_specs=row(d),
        out_shape=jax.ShapeDtypeStruct((n, d), F32),
        compiler_params=_params(("parallel",)),
        name="out_proj",
    )(oa, ob, oc, x2d, w_bf16, g[None], b[None])


FFN_CHUNK = 256


def _ffn_kernel(x_ref, wg_ref, wu_ref, wd_ref, g_ref, b_ref, y_ref, *, alpha):
    x = x_ref[...]
    xb = x.astype(BF16)
    d_ff = wg_ref.shape[1]
    acc = alpha * x
    for j in range(d_ff // FFN_CHUNK):
        cs = slice(j * FFN_CHUNK, (j + 1) * FFN_CHUNK)
        gate = _dot(xb, wg_ref[:, cs])
        up = _dot(xb, wu_ref[:, cs])
        act = gate * _sigmoid(gate) * up
        acc += _dot(act.astype(BF16), wd_ref[cs, :])
    y_ref[...] = _layer_norm(acc, g_ref[...], b_ref[...], LN_EPS)


def _ffn(x2d, wg, wu, wd, g, b, alpha, bm):
    n, d = x2d.shape
    row = pl.BlockSpec((bm, d), lambda i: (i, 0))
    vec = pl.BlockSpec((1, d), lambda i: (0, 0))
    full = lambda w: pl.BlockSpec(w.shape, lambda i: (0, 0), pipeline_mode=pl.Buffered(1))
    return pl.pallas_call(
        functools.partial(_ffn_kernel, alpha=alpha),
        grid=(n // bm,),
        in_specs=[row, full(wg), full(wu), full(wd), vec, vec],
        out_specs=row,
        out_shape=jax.ShapeDtypeStruct((n, d), F32),
        compiler_params=_params(("parallel",)),
        name="ffn",
    )(x2d, wg, wu, wd, g[None], b[None])


def _pad_rows(z, rows):
    return jnp.pad(z, ((0, 0), (0, rows - z.shape[1]), (0, 0)))


def _layer(x, shift_prev, wkv0, ret0, pos0, p, wts, alpha, caches=None):
    bsz, t, d = x.shape
    n = bsz * t
    x2d = x.reshape(n, d)
    bm = min(512, n)
    pos = pos0 + jnp.arange(t, dtype=F32)
    reps = max(bm // t, 1)
    tile = lambda tabs: tuple(jnp.tile(z, (reps, 1)) for z in tabs)
    inv_b = ROPE_THETA ** (-jnp.arange(0, ROT_DIM, 2, dtype=F32) / ROT_DIM)
    inv_c = 1.0 / (RET_THETA ** jnp.linspace(0.0, 1.0, HEAD_DIM // 2, dtype=F32))
    ha, hb, hc = _in_proj(x2d, wts['w_in'], tile(_rot_tables(pos, inv_b, ROT_DIM)),
                          tile(_rot_tables(pos, inv_c, HEAD_DIM)), bm)
    ha = ha.reshape(bsz, t, A_COLS)
    hb = hb.reshape(bsz, t, B_COLS)
    hc = hc.reshape(bsz, t, C_COLS)
    shift_new = ha[:, -1]

    tp = -(-t // RWKV_CHUNK) * RWKV_CHUNK
    o_a, wkv_t = _rwkv(_pad_rows(ha, tp), shift_prev, jnp.swapaxes(wkv0, -1, -2), p, t)
    o_a = o_a[:, :t].reshape(n, W_A)
    wkv_new = jnp.swapaxes(wkv_t, -1, -2)

    if caches is None:
        parts = []
        for _, dil in DILATED:
            o_g, lse_g = _attn_branch(hb, dil)
            parts += [o_g.reshape(n, W_B), lse_g.reshape(n, W_B)]
        o_b = _attn_merge(parts, bm)
        w_keep = min(WIN_MAX, t)
        k_keep = hb[:, t - w_keep:, W_B:2 * W_B].reshape(bsz, w_keep, H_B, HEAD_DIM)
        v_keep = hb[:, t - w_keep:, 2 * W_B:].reshape(bsz, w_keep, H_B, HEAD_DIM)
    else:
        cache_k, cache_v, layer = caches
        rows = -(-t // 8) * 8
        o_b, k_keep, v_keep = _attn_sample(_pad_rows(hb, rows), cache_k, cache_v, layer, t)
        o_b = o_b[:, :t].reshape(n, W_B)
        k_keep = k_keep.reshape(bsz, -1, H_B, HEAD_DIM)
        v_keep = v_keep.reshape(bsz, -1, H_B, HEAD_DIM)

    if t % RET_CHUNK == 0:
        o_c, ret_new = _retention(hc, ret0, p['ret_gn_g'], p['ret_gn_b'], RET_CHUNK, RET_CHUNK)
    else:
        rows = -(-t // 8) * 8
        o_c, ret_new = _retention(_pad_rows(hc, rows), ret0, p['ret_gn_g'], p['ret_gn_b'], rows, t)
        o_c = o_c[:, :t]
    o_c = o_c.reshape(n, W_C)

    x1 = _out_proj(o_a, o_b, o_c, x2d, wts['w_out'], p['ln1_g'], p['ln1_b'], alpha, bm)
    x2 = _ffn(x1, wts['w_ffn_gate'], wts['w_ffn_up'], wts['w_ffn_down'], p['ln2_g'], p['ln2_b'], alpha, bm)
    return x2.reshape(bsz, t, d), (shift_new, wkv_new, k_keep, v_keep, ret_new)


def kernel(x_prompt, x_sample, state_rwkv_shift, state_rwkv_wkv, cache_win_k, cache_win_v, state_ret, w_in, rwkv_mu, rwkv_w0, rwkv_w_lora, rwkv_a0, rwkv_a_lora, rwkv_g_lora, rwkv_k_k, rwkv_k_a, rwkv_r_k, rwkv_gn_g, rwkv_gn_b, ret_gn_g, ret_gn_b, w_out, ln1_g, ln1_b, w_ffn_gate, w_ffn_up, w_ffn_down, ln2_g, ln2_b):
    depth = w_in.shape[0]
    alpha = (2 * depth) ** 0.25
    names = ('rwkv_mu', 'rwkv_w0', 'rwkv_w_lora', 'rwkv_a0', 'rwkv_a_lora', 'rwkv_g_lora', 'rwkv_k_k',
             'rwkv_k_a', 'rwkv_r_k', 'rwkv_gn_g', 'rwkv_gn_b', 'ret_gn_g', 'ret_gn_b', 'ln1_g', 'ln1_b',
             'ln2_g', 'ln2_b')
    vals = (rwkv_mu, rwkv_w0, rwkv_w_lora, rwkv_a0, rwkv_a_lora, rwkv_g_lora, rwkv_k_k, rwkv_k_a, rwkv_r_k,
            rwkv_gn_g, rwkv_gn_b, ret_gn_g, ret_gn_b, ln1_g, ln1_b, ln2_g, ln2_b)
    bp = x_prompt.shape[0]
    w_buf = cache_win_k.shape[2]
    ck = cache_win_k.reshape(cache_win_k.shape[:3] + (W_B,))
    cv = cache_win_v.reshape(cache_win_v.shape[:3] + (W_B,))
    xp, xs = x_prompt, x_sample
    p_states, s_states = [], []
    for l in range(depth):
        p = {k: v[l] for k, v in zip(names, vals)}
        wts = {'w_in': w_in[l].astype(BF16), 'w_out': w_out[l].astype(BF16),
               'w_ffn_gate': w_ffn_gate[l].astype(BF16), 'w_ffn_up': w_ffn_up[l].astype(BF16),
               'w_ffn_down': w_ffn_down[l].astype(BF16)}
        xp, sp = _layer(xp, jnp.zeros((bp, A_COLS), F32), jnp.zeros((bp, H_A, HEAD_DIM, HEAD_DIM), F32),
                        jnp.zeros((bp, H_C, HEAD_DIM, HEAD_DIM), F32), 0.0, p, wts, alpha)
        xs, ss = _layer(xs, state_rwkv_shift[l], state_rwkv_wkv[l], state_ret[l], float(PAST_LEN), p, wts,
                        alpha, caches=(ck, cv, l))
        p_states.append(sp)
        s_states.append(ss)
    stack = lambda states, j: jnp.stack([s[j] for s in states])
    return (xp, xs) + tuple(stack(p_states, j) for j in range(5)) + tuple(stack(s_states, j) for j in range(5))
```

```python
import functools
import math

import jax
import jax.numpy as jnp
from jax import lax
from jax.experimental import pallas as pl
from jax.experimental.pallas import tpu as pltpu

F32 = jnp.float32
BF16 = jnp.bfloat16
HI = lax.Precision.HIGHEST

PAST_LEN = 8192
HEAD_DIM = 64
H_A, H_B, H_C = 6, 6, 4
W_A, W_B, W_C = H_A * HEAD_DIM, H_B * HEAD_DIM, H_C * HEAD_DIM
LORA_W, LORA_A, LORA_G = 64, 64, 128
A_COLS = 3 * W_A + LORA_W + LORA_A + LORA_G
B_COLS = 3 * W_B
C_COLS = 4 * W_C
DILATED = ((128, 1), (512, 4), (2048, 16))
WIN_MAX = max(w for w, _ in DILATED)
ATT_BLK = 128
ROT_DIM = HEAD_DIM // 4
ROPE_THETA = 500000.0
RET_THETA = 10000.0
RET_CHUNK = 128
RWKV_CHUNK = 64
RWKV_SUBCHUNKS = 4
RWKV_DECAY_SCALE = math.exp(-0.5)
RWKV_GN_EPS = 64e-5
LN_EPS = 1e-5
NEG = -1e30
LANES = 128
VMEM_LIMIT = 56 * 1024 * 1024


def _params(sem, vmem=VMEM_LIMIT):
    return pltpu.CompilerParams(dimension_semantics=sem, vmem_limit_bytes=vmem)


def _dot(a, b, precision=None):
    return jnp.dot(a, b, preferred_element_type=F32, precision=precision)


def _dot_nt(a, b, precision=None):
    return lax.dot_general(a, b, (((1,), (1,)), ((), ())), preferred_element_type=F32, precision=precision)


def _dot_tn(a, b, precision=None):
    return lax.dot_general(a, b, (((0,), (0,)), ((), ())), preferred_element_type=F32, precision=precision)


def _sigmoid(x):
    return 1.0 / (1.0 + jnp.exp(-x))


def _layer_norm(x, g, b, eps):
    mu = jnp.mean(x, axis=-1, keepdims=True)
    xc = x - mu
    var = jnp.mean(xc * xc, axis=-1, keepdims=True)
    return xc * lax.rsqrt(var + eps) * g + b


def _rot_tables(pos, inv_freq, rot_width):
    half = rot_width // 2
    ang = pos[:, None] * inv_freq[None, :]
    cos, sin = jnp.cos(ang), jnp.sin(ang)
    lane = jnp.arange(LANES) % HEAD_DIM
    idx = lane % half
    cos_l = jnp.where(lane[None, :] < rot_width, cos[:, idx], 1.0)
    sin_l = sin[:, idx]
    sin_up = jnp.where(lane[None, :] < half, -sin_l, 0.0)
    sin_dn = jnp.where((lane[None, :] >= half) & (lane[None, :] < rot_width), sin_l, 0.0)
    return cos_l.astype(F32), sin_up.astype(F32), sin_dn.astype(F32)


def _rotate_slab(x, cos, sin_up, sin_dn, half):
    up = pltpu.roll(x, LANES - half, 1)
    dn = pltpu.roll(x, half, 1)
    return x * cos + up * sin_up + dn * sin_dn


def _in_proj_kernel(x_ref, w_ref, cb_ref, ub_ref, db_ref, cc_ref, uc_ref, dc_ref,
                    ha_ref, hb_ref, hc_ref):
    h = _dot(x_ref[...].astype(BF16), w_ref[...])
    ha_ref[...] = h[:, :A_COLS]
    cb, ub, db = cb_ref[...], ub_ref[...], db_ref[...]
    for j in range(B_COLS // LANES):
        slab = h[:, A_COLS + j * LANES:A_COLS + (j + 1) * LANES]
        if j < 2 * W_B // LANES:
            slab = _rotate_slab(slab, cb, ub, db, ROT_DIM // 2)
        hb_ref[:, j * LANES:(j + 1) * LANES] = slab
    cc, uc, dc = cc_ref[...], uc_ref[...], dc_ref[...]
    c0 = A_COLS + B_COLS
    for j in range(C_COLS // LANES):
        slab = h[:, c0 + j * LANES:c0 + (j + 1) * LANES]
        if j < 2 * W_C // LANES:
            slab = _rotate_slab(slab, cc, uc, dc, HEAD_DIM // 2)
        if W_C // LANES <= j < 2 * W_C // LANES:
            slab = slab * (HEAD_DIM ** -0.5)
        hc_ref[:, j * LANES:(j + 1) * LANES] = slab


def _in_proj(x2d, w_bf16, tabs_b, tabs_c, bm):
    n, d = x2d.shape
    in_cols = w_bf16.shape[1]
    t_rows = tabs_b[0].shape[0]
    nt = t_rows // bm
    tab_spec = pl.BlockSpec((bm, LANES), lambda i: (i % nt, 0))
    row = lambda w: pl.BlockSpec((bm, w), lambda i: (i, 0))
    return pl.pallas_call(
        _in_proj_kernel,
        grid=(n // bm,),
        in_specs=[row(d), pl.BlockSpec((d, in_cols), lambda i: (0, 0))] + [tab_spec] * 6,
        out_specs=[row(A_COLS), row(B_COLS), row(C_COLS)],
        out_shape=[jax.ShapeDtypeStruct((n, A_COLS), F32),
                   jax.ShapeDtypeStruct((n, B_COLS), F32),
                   jax.ShapeDtypeStruct((n, C_COLS), F32)],
        compiler_params=_params(("parallel",)),
        name="in_proj",
    )(x2d, w_bf16, *tabs_b, *tabs_c)


assert RWKV_CHUNK == HEAD_DIM


def _split(x):
    hi = x.astype(BF16)
    return hi, (x - hi.astype(F32)).astype(BF16)


def _mm3(dot, a, b):
    return dot(a[0], b[0]) + dot(a[0], b[1]) + dot(a[1], b[0])


def _head_sum(z, ones):
    hi, lo = _split(z)
    return _dot(hi, ones) + _dot(lo, ones)


def _rwkv_kernel(ha_ref, shift_ref, s0_ref, mu_ref, w0_ref, wl_ref, a0_ref, al_ref, gl_ref,
                 kk_ref, ka_ref, rk_ref, gng_ref, gnb_ref, ones_ref, o_ref, sfin_ref,
                 hp_ref, st_ref, oraw_ref, *, t_valid, n_sub):
    c = pl.program_id(1)
    ch = RWKV_CHUNK
    rows = n_sub * ch

    @pl.when(c == 0)
    def _():
        hp_ref[0:8, :] = jnp.broadcast_to(shift_ref[0], (8, A_COLS))
        st_ref[...] = s0_ref[0]

    h = ha_ref[0]
    hp_ref[8:8 + rows, :] = h
    prev = hp_ref[7:7 + rows, :]
    hp_ref[0:8, :] = h[rows - 8:rows, :]
    xs = h + (prev - h) * mu_ref[...]

    r = xs[:, :W_A]
    k = xs[:, W_A:2 * W_A]
    v = xs[:, 2 * W_A:3 * W_A]
    o0 = 3 * W_A
    xw = xs[:, o0:o0 + LORA_W]
    xa = xs[:, o0 + LORA_W:o0 + LORA_W + LORA_A]
    xg = xs[:, o0 + LORA_W + LORA_A:]
    log_w = -RWKV_DECAY_SCALE * _sigmoid(w0_ref[...] + _dot(jnp.tanh(xw).astype(BF16), wl_ref[...]))
    a = _sigmoid(a0_ref[...] + _dot(xa.astype(BF16), al_ref[...]))
    g = _dot(_sigmoid(xg).astype(BF16), gl_ref[...])
    kk = k * kk_ref[...]
    kmod = k * (1.0 + (a - 1.0) * ka_ref[...])

    if t_valid is not None:
        tok = c * rows + lax.broadcasted_iota(jnp.int32, (rows, 1), 0)
        valid = tok < t_valid
        log_w = jnp.where(valid, log_w, 0.0)
        kk = jnp.where(valid, kk, 0.0)
        kmod = jnp.where(valid, kmod, 0.0)
    ones = ones_ref[...]
    kk = kk * lax.rsqrt(jnp.maximum(_head_sum(kk * kk, ones), 1e-12))
    b = kk * a

    ri = lax.broadcasted_iota(jnp.int32, (rows, rows), 0)
    ci = lax.broadcasted_iota(jnp.int32, (rows, rows), 1)
    shift = int(math.log2(ch))
    same = (ri >> shift) == (ci >> shift)
    lower = (same & (ci <= ri)).astype(BF16)
    whole = same.astype(BF16)
    lw_hi = log_w.astype(BF16)
    rem = log_w - lw_hi.astype(F32)
    lw_mid = rem.astype(BF16)
    lw_lo = (rem - lw_mid.astype(F32)).astype(BF16)
    cum = _dot(lower, lw_hi) + _dot(lower, lw_mid) + _dot(lower, lw_lo)
    cum_end = _dot(whole, lw_hi) + _dot(whole, lw_mid) + _dot(whole, lw_lo)
    p_in = jnp.exp(cum)
    p_inv = jnp.exp(-cum)
    to_end = jnp.exp(cum_end - cum)
    p_end = jnp.exp(cum_end)
    kkt = kk * jnp.exp(cum - log_w)
    bt = b * p_inv
    kt = kmod * p_inv
    rt = r * p_in
    bd = b * to_end
    kd = kmod * to_end

    row = lax.broadcasted_iota(jnp.int32, (ch, ch), 0)
    col = lax.broadcasted_iota(jnp.int32, (ch, ch), 1)
    strict = col < row
    incl = col <= row
    diag = col == row
    off = [((row >> (lvl + 1)) == (col >> (lvl + 1))) & (((row >> lvl) & 1) == 1) & (((col >> lvl) & 1) == 0)
           for lvl in range(shift)]

    chains = [(slice(sub * ch, (sub + 1) * ch), slice(hd * HEAD_DIM, (hd + 1) * HEAD_DIM))
              for sub in range(n_sub) for hd in range(H_A)]
    each = lambda fn, *lists: [fn(*args) for args in zip(*lists)]
    cut = lambda z: [z[rs, sl] for rs, sl in chains]
    kkt_c, rt_c = cut(kkt), cut(rt)
    v_c = each(_split, cut(v))
    aa = each(lambda x1, x2, x3, x4: _mm3(_dot_nt, _split(jnp.concatenate([x1, x2], axis=0)),
                                          _split(jnp.concatenate([x3, x4], axis=0))),
              kkt_c, rt_c, cut(bt), cut(kt))
    a_kb = each(lambda m: m[:ch, :ch], aa)
    a_kb_s = each(_split, a_kb)
    a_kk = each(lambda m: _split(jnp.where(strict, m[:ch, ch:], 0.0)), aa)
    a_rb = each(lambda m: _split(jnp.where(incl, m[ch:, :ch], 0.0)), aa)
    a_rk = each(lambda m: _split(jnp.where(incl, m[ch:, ch:], 0.0)), aa)
    t_inv = each(lambda m: jnp.where(diag, 1.0, 0.0) - jnp.where(off[0], m, 0.0), a_kb)
    for lvl in range(1, shift):
        off_b = off[lvl].astype(BF16)
        t_s = each(_split, t_inv)
        inner = each(lambda m, ts: _split(_mm3(_dot, (m[0] * off_b, m[1] * off_b), ts)), a_kb_s, t_s)
        t_inv = each(lambda t, ts, inn: t - _mm3(_dot, ts, inn), t_inv, t_s, inner)
    x_rhs = each(lambda x, m, vs: jnp.concatenate([x, _mm3(_dot, m, vs)], axis=1), kkt_c, a_kk, v_c)
    y_s = each(lambda t, x: _split(_mm3(_dot, _split(t), _split(x))), t_inv, x_rhs)
    z = each(lambda m, ys: _mm3(_dot, m, ys), a_rb, y_s)
    gz = each(lambda m, ys: _mm3(_dot_tn, _split(m), ys), cut(bd), y_s)
    q_eff = each(lambda x, zz: _split(x - zz[:, :HEAD_DIM]), rt_c, z)
    o_loc = each(lambda m, vs, zz: _mm3(_dot, m, vs) - zz[:, HEAD_DIM:], a_rk, v_c, z)
    g_mat = each(lambda pe, gg: _split(jnp.where(diag, jnp.broadcast_to(pe[0:1, :], (ch, ch)), 0.0)
                                       - gg[:, :HEAD_DIM]), cut(p_end), gz)
    h_mat = each(lambda m, vs, gg: _mm3(_dot_tn, _split(m), vs) - gg[:, HEAD_DIM:], cut(kd), v_c, gz)

    state = [st_ref[hd] for hd in range(H_A)]
    for sub in range(n_sub):
        st_s = each(_split, state)
        ids = range(sub * H_A, (sub + 1) * H_A)
        o_sub = [_mm3(_dot, q_eff[i], s) + o_loc[i] for i, s in zip(ids, st_s)]
        state = [_mm3(_dot, g_mat[i], s) + h_mat[i] for i, s in zip(ids, st_s)]
        for hd in range(H_A):
            oraw_ref[sub * ch:(sub + 1) * ch, hd * HEAD_DIM:(hd + 1) * HEAD_DIM] = o_sub[hd]
    for hd in range(H_A):
        st_ref[hd] = state[hd]

    o_raw = oraw_ref[...]
    oc = o_raw - _head_sum(o_raw, ones) * (1.0 / HEAD_DIM)
    var = _head_sum(oc * oc, ones) * (1.0 / HEAD_DIM)
    o_n = oc * lax.rsqrt(var + RWKV_GN_EPS) * gng_ref[...] + gnb_ref[...]
    bonus = _head_sum(r * kmod * rk_ref[...], ones)
    o_ref[0] = (o_n + bonus * v) * g

    @pl.when(c == pl.num_programs(1) - 1)
    def _():
        sfin_ref[0] = st_ref[...]


def _rwkv(ha, shift_prev, s0_t, p, t_valid, n_sub):
    b, tp, _ = ha.shape
    rows = n_sub * RWKV_CHUNK
    nc = tp // rows
    vec = lambda w: pl.BlockSpec((1, w), lambda i, c: (0, 0))
    mat = lambda r, w: pl.BlockSpec((r, w), lambda i, c: (0, 0))
    lane_head = jnp.arange(W_A) // HEAD_DIM
    ones = (lane_head[:, None] == lane_head[None, :]).astype(BF16)
    kern = functools.partial(_rwkv_kernel, t_valid=None if t_valid == tp else t_valid, n_sub=n_sub)
    return pl.pallas_call(
        kern,
        grid=(b, nc),
        in_specs=[pl.BlockSpec((1, rows, A_COLS), lambda i, c: (i, c, 0)),
                  pl.BlockSpec((1, 1, A_COLS), lambda i, c: (i, 0, 0)),
                  pl.BlockSpec((1, H_A, HEAD_DIM, HEAD_DIM), lambda i, c: (i, 0, 0, 0)),
                  vec(A_COLS), vec(W_A), mat(LORA_W, W_A), vec(W_A), mat(LORA_A, W_A), mat(LORA_G, W_A),
                  vec(W_A), vec(W_A), vec(W_A), vec(W_A), vec(W_A), mat(W_A, W_A)],
        out_specs=[pl.BlockSpec((1, rows, W_A), lambda i, c: (i, c, 0)),
                   pl.BlockSpec((1, H_A, HEAD_DIM, HEAD_DIM), lambda i, c: (i, 0, 0, 0))],
        out_shape=[jax.ShapeDtypeStruct((b, tp, W_A), F32),
                   jax.ShapeDtypeStruct((b, H_A, HEAD_DIM, HEAD_DIM), F32)],
        scratch_shapes=[pltpu.VMEM((rows + 8, A_COLS), F32),
                        pltpu.VMEM((H_A, HEAD_DIM, HEAD_DIM), F32),
                        pltpu.VMEM((rows, W_A), F32)],
        compiler_params=_params(("parallel", "arbitrary")),
        name="rwkv7",
    )(ha, shift_prev[:, None, :], s0_t,
      p['rwkv_mu'][None], p['rwkv_w0'][None], p['rwkv_w_lora'].astype(BF16), p['rwkv_a0'][None],
      p['rwkv_a_lora'].astype(BF16), p['rwkv_g_lora'].astype(BF16), p['rwkv_k_k'][None],
      p['rwkv_k_a'][None], p['rwkv_r_k'].reshape(1, W_A), p['rwkv_gn_g'][None], p['rwkv_gn_b'][None], ones)


def _attn_branch_kernel(q_ref, k_ref, v_ref, kp_ref, vp_ref, o_ref, lse_ref):
    not_first = pl.program_id(2) > 0
    blk = ATT_BLK
    qi = lax.broadcasted_iota(jnp.int32, (blk, blk), 0)
    ki = lax.broadcasted_iota(jnp.int32, (blk, blk), 1)
    ok_prev = (ki >= qi) & not_first
    ok_own = ki <= qi
    for hd in range(H_B):
        sl = slice(hd * HEAD_DIM, (hd + 1) * HEAD_DIM)
        q = q_ref[0, :, sl].astype(BF16)
        s_prev = _dot_nt(q, kp_ref[0, :, sl].astype(BF16)) * (HEAD_DIM ** -0.5)
        s_own = _dot_nt(q, k_ref[0, :, sl].astype(BF16)) * (HEAD_DIM ** -0.5)
        s_prev = jnp.where(ok_prev, s_prev, NEG)
        s_own = jnp.where(ok_own, s_own, NEG)
        m = jnp.maximum(jnp.max(s_prev, axis=-1, keepdims=True), jnp.max(s_own, axis=-1, keepdims=True))
        p_prev = jnp.exp(s_prev - m)
        p_own = jnp.exp(s_own - m)
        l = jnp.sum(p_prev, axis=-1, keepdims=True) + jnp.sum(p_own, axis=-1, keepdims=True)
        acc = _dot(p_prev.astype(BF16), vp_ref[0, :, sl].astype(BF16)) + \
            _dot(p_own.astype(BF16), v_ref[0, :, sl].astype(BF16))
        o_ref[0, :, sl] = acc / l
        lse_ref[0, :, sl] = jnp.broadcast_to(m + jnp.log(l), (blk, HEAD_DIM))


def _attn_branch(hb, dil):
    b, t, _ = hb.shape
    length = t // dil
    nb = length // ATT_BLK
    hb_v = hb.reshape(b, length, dil * B_COLS)
    blk = (1, ATT_BLK, W_B)
    own = lambda part: pl.BlockSpec(blk, lambda i, r, n: (i, n, 3 * r + part))
    prev = lambda part: pl.BlockSpec(blk, lambda i, r, n: (i, jnp.maximum(n - 1, 0), 3 * r + part))
    out_spec = pl.BlockSpec(blk, lambda i, r, n: (i, n, r))
    o, lse = pl.pallas_call(
        _attn_branch_kernel,
        grid=(b, dil, nb),
        in_specs=[own(0), own(1), own(2), prev(1), prev(2)],
        out_specs=[out_spec, out_spec],
        out_shape=[jax.ShapeDtypeStruct((b, length, dil * W_B), F32)] * 2,
        compiler_params=_params(("parallel", "parallel", "arbitrary")),
        name=f"attn_branch_d{dil}",
    )(hb_v, hb_v, hb_v, hb_v, hb_v)
    return o.reshape(b, t, W_B), lse.reshape(b, t, W_B)


def _attn_merge_kernel(o1, l1, o2, l2, o3, l3, out_ref):
    a1, a2, a3 = l1[...], l2[...], l3[...]
    m = jnp.maximum(jnp.maximum(a1, a2), a3)
    e1, e2, e3 = jnp.exp(a1 - m), jnp.exp(a2 - m), jnp.exp(a3 - m)
    tot = e1 + e2 + e3
    out_ref[...] = (e1 / tot) * o1[...] + (e2 / tot) * o2[...] + (e3 / tot) * o3[...]


def _attn_merge(parts, bm):
    n = parts[0].shape[0]
    spec = pl.BlockSpec((bm, W_B), lambda i: (i, 0))
    return pl.pallas_call(
        _attn_merge_kernel,
        grid=(n // bm,),
        in_specs=[spec] * 6,
        out_specs=spec,
        out_shape=jax.ShapeDtypeStruct((n, W_B), F32),
        compiler_params=_params(("parallel",)),
        name="attn_merge",
    )(*parts)


def _attn_sample_kernel(hb_ref, ck_ref, cv_ref, o_ref, nk_ref, nv_ref, *, t_new, rows):
    w_buf = ck_ref.shape[2]
    q_all = hb_ref[0][:, :W_B]
    k_new = hb_ref[0][:, W_B:2 * W_B]
    v_new = hb_ref[0][:, 2 * W_B:]
    nk_ref[0, 0, 0:w_buf - t_new, :] = ck_ref[0, 0, t_new:w_buf, :]
    nv_ref[0, 0, 0:w_buf - t_new, :] = cv_ref[0, 0, t_new:w_buf, :]
    nk_ref[0, 0, w_buf - t_new:w_buf, :] = k_new[0:t_new, :]
    nv_ref[0, 0, w_buf - t_new:w_buf, :] = v_new[0:t_new, :]

    qt = lax.broadcasted_iota(jnp.int32, (rows, w_buf), 0)
    kc = lax.broadcasted_iota(jnp.int32, (rows, w_buf), 1)
    d_cache = w_buf + qt - kc
    qn = lax.broadcasted_iota(jnp.int32, (rows, rows), 0)
    kn = lax.broadcasted_iota(jnp.int32, (rows, rows), 1)
    d_new = qn - kn
    new_valid = (d_new >= 0) & (kn < t_new)
    for hd in range(H_B):
        sl = slice(hd * HEAD_DIM, (hd + 1) * HEAD_DIM)
        q = q_all[:, sl].astype(BF16)
        kc_h = ck_ref[0, 0, :, sl].astype(BF16)
        vc_h = cv_ref[0, 0, :, sl].astype(BF16)
        kn_h = k_new[:, sl].astype(BF16)
        vn_h = v_new[:, sl].astype(BF16)
        s_c = _dot_nt(q, kc_h) * (HEAD_DIM ** -0.5)
        s_n = _dot_nt(q, kn_h) * (HEAD_DIM ** -0.5)
        outs, lses = [], []
        for win, dil in DILATED:
            ok_c = ((d_cache & (dil - 1)) == 0) & (d_cache <= win)
            ok_n = new_valid & ((d_new & (dil - 1)) == 0) & (d_new <= win)
            sc = jnp.where(ok_c, s_c, NEG)
            sn = jnp.where(ok_n, s_n, NEG)
            m = jnp.maximum(jnp.max(sc, axis=-1, keepdims=True), jnp.max(sn, axis=-1, keepdims=True))
            pc = jnp.exp(sc - m)
            pn = jnp.exp(sn - m)
            l = jnp.sum(pc, axis=-1, keepdims=True) + jnp.sum(pn, axis=-1, keepdims=True)
            outs.append((_dot(pc.astype(BF16), vc_h) + _dot(pn.astype(BF16), vn_h)) / l)
            lses.append(m + jnp.log(l))
        m3 = jnp.maximum(jnp.maximum(lses[0], lses[1]), lses[2])
        es = [jnp.exp(x - m3) for x in lses]
        tot = es[0] + es[1] + es[2]
        o_ref[0, :, sl] = (es[0] / tot) * outs[0] + (es[1] / tot) * outs[1] + (es[2] / tot) * outs[2]


def _attn_sample(hb, cache_k, cache_v, layer, t_new):
    b, rows, _ = hb.shape
    depth, _, w_buf, _ = cache_k.shape
    cspec = pl.BlockSpec((1, 1, w_buf, W_B), lambda i: (layer, i, 0, 0))
    ospec = pl.BlockSpec((1, 1, w_buf, W_B), lambda i: (0, i, 0, 0))
    kern = functools.partial(_attn_sample_kernel, t_new=t_new, rows=rows)
    o, nk, nv = pl.pallas_call(
        kern,
        grid=(b,),
        in_specs=[pl.BlockSpec((1, rows, B_COLS), lambda i: (i, 0, 0)), cspec, cspec],
        out_specs=[pl.BlockSpec((1, rows, W_B), lambda i: (i, 0, 0)), ospec, ospec],
        out_shape=[jax.ShapeDtypeStruct((b, rows, W_B), F32),
                   jax.ShapeDtypeStruct((1, b, w_buf, W_B), F32),
                   jax.ShapeDtypeStruct((1, b, w_buf, W_B), F32)],
        compiler_params=_params(("parallel",)),
        name="attn_sample",
    )(hb, cache_k, cache_v)
    return o, nk[0], nv[0]


def _retention_kernel(q_ref, k_ref, v_ref, g_ref, r0_ref, gng_ref, gnb_ref, o_ref, rfin_ref, st_ref,
                      *, ch):
    c = pl.program_id(1)
    rows = q_ref.shape[1]

    @pl.when(c == 0)
    def _():
        st_ref[...] = r0_ref[0]

    qi = lax.broadcasted_iota(jnp.int32, (rows, rows), 0)
    kj = lax.broadcasted_iota(jnp.int32, (rows, rows), 1)
    rel = (qi - kj).astype(F32)
    pos = lax.broadcasted_iota(jnp.int32, (rows, 1), 0).astype(F32)
    gate = g_ref[0]
    gate = gate * _sigmoid(gate)
    for hd in range(H_C):
        sl = slice(hd * HEAD_DIM, (hd + 1) * HEAD_DIM)
        log_gamma = math.log(1.0 - 2.0 ** (-5.0 - hd))
        dmask = jnp.where(rel >= 0, jnp.exp(log_gamma * jnp.maximum(rel, 0.0)), 0.0)
        kdec = jnp.exp(log_gamma * (ch - 1.0 - pos))
        qdec = jnp.exp(log_gamma * (pos + 1.0))
        q = q_ref[0, :, sl]
        k = k_ref[0, :, sl]
        v = v_ref[0, :, sl].astype(BF16)
        att = _dot_nt(q.astype(BF16), k.astype(BF16)) * dmask
        st = st_ref[hd]
        o = _dot(att.astype(BF16), v) + _dot((q * qdec).astype(BF16), st.astype(BF16))
        kv = _dot_tn((k * kdec).astype(BF16), v)
        st_ref[hd] = st * math.exp(log_gamma * ch) + kv
        mu = jnp.mean(o, axis=-1, keepdims=True)
        oc = o - mu
        var = jnp.mean(oc * oc, axis=-1, keepdims=True)
        o_ref[0, :, sl] = (oc * lax.rsqrt(var + LN_EPS) * gng_ref[:, sl] + gnb_ref[:, sl]) * gate[:, sl]

    @pl.when(c == pl.num_programs(1) - 1)
    def _():
        rfin_ref[0] = st_ref[...]


def _retention(hc, r0, gn_g, gn_b, rows, ch):
    b, tp, _ = hc.shape
    nc = tp // rows
    part = lambda j: pl.BlockSpec((1, rows, W_C), lambda i, c: (i, c, j))
    st_spec = pl.BlockSpec((1, H_C, HEAD_DIM, HEAD_DIM), lambda i, c: (i, 0, 0, 0))
    vec = pl.BlockSpec((1, W_C), lambda i, c: (0, 0))
    return pl.pallas_call(
        functools.partial(_retention_kernel, ch=ch),
        grid=(b, nc),
        in_specs=[part(0), part(1), part(2), part(3), st_spec, vec, vec],
        out_specs=[pl.BlockSpec((1, rows, W_C), lambda i, c: (i, c, 0)), st_spec],
        out_shape=[jax.ShapeDtypeStruct((b, tp, W_C), F32),
                   jax.ShapeDtypeStruct((b, H_C, HEAD_DIM, HEAD_DIM), F32)],
        scratch_shapes=[pltpu.VMEM((H_C, HEAD_DIM, HEAD_DIM), F32)],
        compiler_params=_params(("parallel", "arbitrary")),
        name="retention",
    )(hc, hc, hc, hc, r0, gn_g[None], gn_b[None])


def _out_proj_kernel(oa_ref, ob_ref, oc_ref, x_ref, w_ref, g_ref, b_ref, y_ref, *, alpha):
    mix = _dot(oa_ref[...].astype(BF16), w_ref[0:W_A, :])
    mix += _dot(ob_ref[...].astype(BF16), w_ref[W_A:W_A + W_B, :])
    mix += _dot(oc_ref[...].astype(BF16), w_ref[W_A + W_B:, :])
    y_ref[...] = _layer_norm(alpha * x_ref[...] + mix, g_ref[...], b_ref[...], LN_EPS)


def _out_proj(oa, ob, oc, x2d, w_bf16, g, b, alpha, bm):
    n, d = x2d.shape
    row = lambda w: pl.BlockSpec((bm, w), lambda i: (i, 0))
    vec = pl.BlockSpec((1, d), lambda i: (0, 0))
    return pl.pallas_call(
        functools.partial(_out_proj_kernel, alpha=alpha),
        grid=(n // bm,),
        in_specs=[row(W_A), row(W_B), row(W_C), row(d), pl.BlockSpec(w_bf16.shape, lambda i: (0, 0)), vec, vec],
        out_specs=row(d),
        out_shape=jax.ShapeDtypeStruct((n, d), F32),
        compiler_params=_params(("parallel",)),
        name="out_proj",
    )(oa, ob, oc, x2d, w_bf16, g[None], b[None])


FFN_CHUNK = 256


def _ffn_kernel(x_ref, wg_ref, wu_ref, wd_ref, g_ref, b_ref, y_ref, *, alpha):
    x = x_ref[...]
    xb = x.astype(BF16)
    d_ff = wg_ref.shape[1]
    acc = alpha * x
    for j in range(d_ff // FFN_CHUNK):
        cs = slice(j * FFN_CHUNK, (j + 1) * FFN_CHUNK)
        gate = _dot(xb, wg_ref[:, cs])
        up = _dot(xb, wu_ref[:, cs])
        act = gate * _sigmoid(gate) * up
        acc += _dot(act.astype(BF16), wd_ref[cs, :])
    y_ref[...] = _layer_norm(acc, g_ref[...], b_ref[...], LN_EPS)


def _ffn(x2d, wg, wu, wd, g, b, alpha, bm):
    n, d = x2d.shape
    row = pl.BlockSpec((bm, d), lambda i: (i, 0))
    vec = pl.BlockSpec((1, d), lambda i: (0, 0))
    full = lambda w: pl.BlockSpec(w.shape, lambda i: (0, 0), pipeline_mode=pl.Buffered(1))
    return pl.pallas_call(
        functools.partial(_ffn_kernel, alpha=alpha),
        grid=(n // bm,),
        in_specs=[row, full(wg), full(wu), full(wd), vec, vec],
        out_specs=row,
        out_shape=jax.ShapeDtypeStruct((n, d), F32),
        compiler_params=_params(("parallel",)),
        name="ffn",
    )(x2d, wg, wu, wd, g[None], b[None])


def _pad_rows(z, rows):
    return jnp.pad(z, ((0, 0), (0, rows - z.shape[1]), (0, 0)))


def _layer(x, shift_prev, wkv0, ret0, pos0, p, wts, alpha, caches=None):
    bsz, t, d = x.shape
    n = bsz * t
    x2d = x.reshape(n, d)
    bm = min(512, n)
    pos = pos0 + jnp.arange(t, dtype=F32)
    reps = max(bm // t, 1)
    tile = lambda tabs: tuple(jnp.tile(z, (reps, 1)) for z in tabs)
    inv_b = ROPE_THETA ** (-jnp.arange(0, ROT_DIM, 2, dtype=F32) / ROT_DIM)
    inv_c = 1.0 / (RET_THETA ** jnp.linspace(0.0, 1.0, HEAD_DIM // 2, dtype=F32))
    ha, hb, hc = _in_proj(x2d, wts['w_in'], tile(_rot_tables(pos, inv_b, ROT_DIM)),
                          tile(_rot_tables(pos, inv_c, HEAD_DIM)), bm)
    ha = ha.reshape(bsz, t, A_COLS)
    hb = hb.reshape(bsz, t, B_COLS)
    hc = hc.reshape(bsz, t, C_COLS)
    shift_new = ha[:, -1]

    tp = -(-t // RWKV_CHUNK) * RWKV_CHUNK
    o_a, wkv_t = _rwkv(_pad_rows(ha, tp), shift_prev, jnp.swapaxes(wkv0, -1, -2), p, t,
                       RWKV_SUBCHUNKS if tp % (RWKV_SUBCHUNKS * RWKV_CHUNK) == 0 else 1)
    o_a = o_a[:, :t].reshape(n, W_A)
    wkv_new = jnp.swapaxes(wkv_t, -1, -2)

    if caches is None:
        parts = []
        for _, dil in DILATED:
            o_g, lse_g = _attn_branch(hb, dil)
            parts += [o_g.reshape(n, W_B), lse_g.reshape(n, W_B)]
        o_b = _attn_merge(parts, bm)
        w_keep = min(WIN_MAX, t)
        k_keep = hb[:, t - w_keep:, W_B:2 * W_B].reshape(bsz, w_keep, H_B, HEAD_DIM)
        v_keep = hb[:, t - w_keep:, 2 * W_B:].reshape(bsz, w_keep, H_B, HEAD_DIM)
    else:
        cache_k, cache_v, layer = caches
        rows = -(-t // 8) * 8
        o_b, k_keep, v_keep = _attn_sample(_pad_rows(hb, rows), cache_k, cache_v, layer, t)
        o_b = o_b[:, :t].reshape(n, W_B)
        k_keep = k_keep.reshape(bsz, -1, H_B, HEAD_DIM)
        v_keep = v_keep.reshape(bsz, -1, H_B, HEAD_DIM)

    if t % RET_CHUNK == 0:
        o_c, ret_new = _retention(hc, ret0, p['ret_gn_g'], p['ret_gn_b'], RET_CHUNK, RET_CHUNK)
    else:
        rows = -(-t // 8) * 8
        o_c, ret_new = _retention(_pad_rows(hc, rows), ret0, p['ret_gn_g'], p['ret_gn_b'], rows, t)
        o_c = o_c[:, :t]
    o_c = o_c.reshape(n, W_C)

    x1 = _out_proj(o_a, o_b, o_c, x2d, wts['w_out'], p['ln1_g'], p['ln1_b'], alpha, bm)
    x2 = _ffn(x1, wts['w_ffn_gate'], wts['w_ffn_up'], wts['w_ffn_down'], p['ln2_g'], p['ln2_b'], alpha, bm)
    return x2.reshape(bsz, t, d), (shift_new, wkv_new, k_keep, v_keep, ret_new)


def kernel(x_prompt, x_sample, state_rwkv_shift, state_rwkv_wkv, cache_win_k, cache_win_v, state_ret, w_in, rwkv_mu, rwkv_w0, rwkv_w_lora, rwkv_a0, rwkv_a_lora, rwkv_g_lora, rwkv_k_k, rwkv_k_a, rwkv_r_k, rwkv_gn_g, rwkv_gn_b, ret_gn_g, ret_gn_b, w_out, ln1_g, ln1_b, w_ffn_gate, w_ffn_up, w_ffn_down, ln2_g, ln2_b):
    depth = w_in.shape[0]
    alpha = (2 * depth) ** 0.25
    names = ('rwkv_mu', 'rwkv_w0', 'rwkv_w_lora', 'rwkv_a0', 'rwkv_a_lora', 'rwkv_g_lora', 'rwkv_k_k',
             'rwkv_k_a', 'rwkv_r_k', 'rwkv_gn_g', 'rwkv_gn_b', 'ret_gn_g', 'ret_gn_b', 'ln1_g', 'ln1_b',
             'ln2_g', 'ln2_b')
    vals = (rwkv_mu, rwkv_w0, rwkv_w_lora, rwkv_a0, rwkv_a_lora, rwkv_g_lora, rwkv_k_k, rwkv_k_a, rwkv_r_k,
            rwkv_gn_g, rwkv_gn_b, ret_gn_g, ret_gn_b, ln1_g, ln1_b, ln2_g, ln2_b)
    bp = x_prompt.shape[0]
    w_buf = cache_win_k.shape[2]
    ck = cache_win_k.reshape(cache_win_k.shape[:3] + (W_B,))
    cv = cache_win_v.reshape(cache_win_v.shape[:3] + (W_B,))
    xp, xs = x_prompt, x_sample
    p_states, s_states = [], []
    for l in range(depth):
        p = {k: v[l] for k, v in zip(names, vals)}
        wts = {'w_in': w_in[l].astype(BF16), 'w_out': w_out[l].astype(BF16),
               'w_ffn_gate': w_ffn_gate[l].astype(BF16), 'w_ffn_up': w_ffn_up[l].astype(BF16),
               'w_ffn_down': w_ffn_down[l].astype(BF16)}
        xp, sp = _layer(xp, jnp.zeros((bp, A_COLS), F32), jnp.zeros((bp, H_A, HEAD_DIM, HEAD_DIM), F32),
                        jnp.zeros((bp, H_C, HEAD_DIM, HEAD_DIM), F32), 0.0, p, wts, alpha)
        xs, ss = _layer(xs, state_rwkv_shift[l], state_rwkv_wkv[l], state_ret[l], float(PAST_LEN), p, wts,
                        alpha, caches=(ck, cv, l))
        p_states.append(sp)
        s_states.append(ss)
    stack = lambda states, j: jnp.stack([s[j] for s in states])
    return (xp, xs) + tuple(stack(p_states, j) for j in range(5)) + tuple(stack(s_states, j) for j in range(5))
```

```python
import functools
import math

import jax
import jax.numpy as jnp
from jax import lax
from jax.experimental import pallas as pl
from jax.experimental.pallas import tpu as pltpu

F32 = jnp.float32
BF16 = jnp.bfloat16
HI = lax.Precision.HIGHEST

PAST_LEN = 8192
HEAD_DIM = 64
H_A, H_B, H_C = 6, 6, 4
W_A, W_B, W_C = H_A * HEAD_DIM, H_B * HEAD_DIM, H_C * HEAD_DIM
LORA_W, LORA_A, LORA_G = 64, 64, 128
A_COLS = 3 * W_A + LORA_W + LORA_A + LORA_G
B_COLS = 3 * W_B
C_COLS = 4 * W_C
DILATED = ((128, 1), (512, 4), (2048, 16))
WIN_MAX = max(w for w, _ in DILATED)
ATT_BLK = 128
ROT_DIM = HEAD_DIM // 4
ROPE_THETA = 500000.0
RET_THETA = 10000.0
RET_CHUNK = 128
RWKV_CHUNK = 64
RWKV_SUBCHUNKS = 4
RWKV_DECAY_SCALE = math.exp(-0.5)
RWKV_GN_EPS = 64e-5
LN_EPS = 1e-5
NEG = -1e30
LANES = 128
VMEM_LIMIT = 56 * 1024 * 1024


def _params(sem, vmem=VMEM_LIMIT):
    return pltpu.CompilerParams(dimension_semantics=sem, vmem_limit_bytes=vmem)


def _dot(a, b, precision=None):
    return jnp.dot(a, b, preferred_element_type=F32, precision=precision)


def _dot_nt(a, b, precision=None):
    return lax.dot_general(a, b, (((1,), (1,)), ((), ())), preferred_element_type=F32, precision=precision)


def _dot_tn(a, b, precision=None):
    return lax.dot_general(a, b, (((0,), (0,)), ((), ())), preferred_element_type=F32, precision=precision)


def _sigmoid(x):
    return 1.0 / (1.0 + jnp.exp(-x))


def _layer_norm(x, g, b, eps):
    mu = jnp.mean(x, axis=-1, keepdims=True)
    xc = x - mu
    var = jnp.mean(xc * xc, axis=-1, keepdims=True)
    return xc * lax.rsqrt(var + eps) * g + b


def _rot_tables(pos, inv_freq, rot_width):
    half = rot_width // 2
    ang = pos[:, None] * inv_freq[None, :]
    cos, sin = jnp.cos(ang), jnp.sin(ang)
    lane = jnp.arange(LANES) % HEAD_DIM
    idx = lane % half
    cos_l = jnp.where(lane[None, :] < rot_width, cos[:, idx], 1.0)
    sin_l = sin[:, idx]
    sin_up = jnp.where(lane[None, :] < half, -sin_l, 0.0)
    sin_dn = jnp.where((lane[None, :] >= half) & (lane[None, :] < rot_width), sin_l, 0.0)
    return cos_l.astype(F32), sin_up.astype(F32), sin_dn.astype(F32)


def _rotate_slab(x, cos, sin_up, sin_dn, half):
    up = pltpu.roll(x, LANES - half, 1)
    dn = pltpu.roll(x, half, 1)
    return x * cos + up * sin_up + dn * sin_dn


def _in_proj_kernel(x_ref, w_ref, cb_ref, ub_ref, db_ref, cc_ref, uc_ref, dc_ref,
                    ha_ref, hb_ref, hc_ref):
    h = _dot(x_ref[...].astype(BF16), w_ref[...])
    ha_ref[...] = h[:, :A_COLS]
    cb, ub, db = cb_ref[...], ub_ref[...], db_ref[...]
    for j in range(B_COLS // LANES):
        slab = h[:, A_COLS + j * LANES:A_COLS + (j + 1) * LANES]
        if j < 2 * W_B // LANES:
            slab = _rotate_slab(slab, cb, ub, db, ROT_DIM // 2)
        hb_ref[j] = slab
    cc, uc, dc = cc_ref[...], uc_ref[...], dc_ref[...]
    c0 = A_COLS + B_COLS
    for j in range(C_COLS // LANES):
        slab = h[:, c0 + j * LANES:c0 + (j + 1) * LANES]
        if j < 2 * W_C // LANES:
            slab = _rotate_slab(slab, cc, uc, dc, HEAD_DIM // 2)
        if W_C // LANES <= j < 2 * W_C // LANES:
            slab = slab * (HEAD_DIM ** -0.5)
        hc_ref[:, j * LANES:(j + 1) * LANES] = slab


def _in_proj(x2d, w_bf16, tabs_b, tabs_c, bm):
    n, d = x2d.shape
    in_cols = w_bf16.shape[1]
    t_rows = tabs_b[0].shape[0]
    nt = t_rows // bm
    tab_spec = pl.BlockSpec((bm, LANES), lambda i: (i % nt, 0))
    row = lambda w: pl.BlockSpec((bm, w), lambda i: (i, 0))
    return pl.pallas_call(
        _in_proj_kernel,
        grid=(n // bm,),
        in_specs=[row(d), pl.BlockSpec((d, in_cols), lambda i: (0, 0))] + [tab_spec] * 6,
        out_specs=[row(A_COLS), pl.BlockSpec((B_COLS // LANES, bm, LANES), lambda i: (0, i, 0)), row(C_COLS)],
        out_shape=[jax.ShapeDtypeStruct((n, A_COLS), F32),
                   jax.ShapeDtypeStruct((B_COLS // LANES, n, LANES), F32),
                   jax.ShapeDtypeStruct((n, C_COLS), F32)],
        compiler_params=_params(("parallel",)),
        name="in_proj",
    )(x2d, w_bf16, *tabs_b, *tabs_c)


assert RWKV_CHUNK == HEAD_DIM


def _split(x):
    hi = x.astype(BF16)
    return hi, (x - hi.astype(F32)).astype(BF16)


def _mm3(dot, a, b):
    return dot(a[0], b[0]) + dot(a[0], b[1]) + dot(a[1], b[0])


def _head_sum(z, ones):
    hi, lo = _split(z)
    return _dot(hi, ones) + _dot(lo, ones)


def _rwkv_kernel(ha_ref, shift_ref, s0_ref, mu_ref, w0_ref, wl_ref, a0_ref, al_ref, gl_ref,
                 kk_ref, ka_ref, rk_ref, gng_ref, gnb_ref, ones_ref, o_ref, sfin_ref,
                 hp_ref, st_ref, oraw_ref, *, t_valid, n_sub):
    c = pl.program_id(1)
    ch = RWKV_CHUNK
    rows = n_sub * ch

    @pl.when(c == 0)
    def _():
        hp_ref[0:8, :] = jnp.broadcast_to(shift_ref[0], (8, A_COLS))
        st_ref[...] = s0_ref[0]

    h = ha_ref[0]
    hp_ref[8:8 + rows, :] = h
    prev = hp_ref[7:7 + rows, :]
    hp_ref[0:8, :] = h[rows - 8:rows, :]
    xs = h + (prev - h) * mu_ref[...]

    r = xs[:, :W_A]
    k = xs[:, W_A:2 * W_A]
    v = xs[:, 2 * W_A:3 * W_A]
    o0 = 3 * W_A
    xw = xs[:, o0:o0 + LORA_W]
    xa = xs[:, o0 + LORA_W:o0 + LORA_W + LORA_A]
    xg = xs[:, o0 + LORA_W + LORA_A:]
    log_w = -RWKV_DECAY_SCALE * _sigmoid(w0_ref[...] + _dot(jnp.tanh(xw).astype(BF16), wl_ref[...]))
    a = _sigmoid(a0_ref[...] + _dot(xa.astype(BF16), al_ref[...]))
    g = _dot(_sigmoid(xg).astype(BF16), gl_ref[...])
    kk = k * kk_ref[...]
    kmod = k * (1.0 + (a - 1.0) * ka_ref[...])

    if t_valid is not None:
        tok = c * rows + lax.broadcasted_iota(jnp.int32, (rows, 1), 0)
        valid = tok < t_valid
        log_w = jnp.where(valid, log_w, 0.0)
        kk = jnp.where(valid, kk, 0.0)
        kmod = jnp.where(valid, kmod, 0.0)
    ones = ones_ref[...]
    kk = kk * lax.rsqrt(jnp.maximum(_head_sum(kk * kk, ones), 1e-12))
    b = kk * a

    ri = lax.broadcasted_iota(jnp.int32, (rows, rows), 0)
    ci = lax.broadcasted_iota(jnp.int32, (rows, rows), 1)
    shift = int(math.log2(ch))
    same = (ri >> shift) == (ci >> shift)
    lower = (same & (ci <= ri)).astype(BF16)
    whole = same.astype(BF16)
    lw_hi = log_w.astype(BF16)
    rem = log_w - lw_hi.astype(F32)
    lw_mid = rem.astype(BF16)
    lw_lo = (rem - lw_mid.astype(F32)).astype(BF16)
    cum = _dot(lower, lw_hi) + _dot(lower, lw_mid) + _dot(lower, lw_lo)
    cum_end = _dot(whole, lw_hi) + _dot(whole, lw_mid) + _dot(whole, lw_lo)
    p_in = jnp.exp(cum)
    p_inv = jnp.exp(-cum)
    to_end = jnp.exp(cum_end - cum)
    p_end = jnp.exp(cum_end)
    kkt = kk * jnp.exp(cum - log_w)
    bt = b * p_inv
    kt = kmod * p_inv
    rt = r * p_in
    bd = b * to_end
    kd = kmod * to_end

    row = lax.broadcasted_iota(jnp.int32, (ch, ch), 0)
    col = lax.broadcasted_iota(jnp.int32, (ch, ch), 1)
    strict = col < row
    incl = col <= row
    diag = col == row
    off = [((row >> (lvl + 1)) == (col >> (lvl + 1))) & (((row >> lvl) & 1) == 1) & (((col >> lvl) & 1) == 0)
           for lvl in range(shift)]

    chains = [(slice(sub * ch, (sub + 1) * ch), slice(hd * HEAD_DIM, (hd + 1) * HEAD_DIM))
              for sub in range(n_sub) for hd in range(H_A)]
    each = lambda fn, *lists: [fn(*args) for args in zip(*lists)]
    cut = lambda z: [z[rs, sl] for rs, sl in chains]
    kkt_c, rt_c = cut(kkt), cut(rt)
    v_c = each(_split, cut(v))
    aa = each(lambda x1, x2, x3, x4: _mm3(_dot_nt, _split(jnp.concatenate([x1, x2], axis=0)),
                                          _split(jnp.concatenate([x3, x4], axis=0))),
              kkt_c, rt_c, cut(bt), cut(kt))
    a_kb = each(lambda m: m[:ch, :ch], aa)
    a_kb_s = each(_split, a_kb)
    a_kk = each(lambda m: _split(jnp.where(strict, m[:ch, ch:], 0.0)), aa)
    a_rb = each(lambda m: _split(jnp.where(incl, m[ch:, :ch], 0.0)), aa)
    a_rk = each(lambda m: _split(jnp.where(incl, m[ch:, ch:], 0.0)), aa)
    t_inv = each(lambda m: jnp.where(diag, 1.0, 0.0) - jnp.where(off[0], m, 0.0), a_kb)
    for lvl in range(1, shift):
        off_b = off[lvl].astype(BF16)
        t_s = each(_split, t_inv)
        inner = each(lambda m, ts: _split(_mm3(_dot, (m[0] * off_b, m[1] * off_b), ts)), a_kb_s, t_s)
        t_inv = each(lambda t, ts, inn: t - _mm3(_dot, ts, inn), t_inv, t_s, inner)
    x_rhs = each(lambda x, m, vs: jnp.concatenate([x, _mm3(_dot, m, vs)], axis=1), kkt_c, a_kk, v_c)
    y_s = each(lambda t, x: _split(_mm3(_dot, _split(t), _split(x))), t_inv, x_rhs)
    z = each(lambda m, ys: _mm3(_dot, m, ys), a_rb, y_s)
    gz = each(lambda m, ys: _mm3(_dot_tn, _split(m), ys), cut(bd), y_s)
    q_eff = each(lambda x, zz: _split(x - zz[:, :HEAD_DIM]), rt_c, z)
    o_loc = each(lambda m, vs, zz: _mm3(_dot, m, vs) - zz[:, HEAD_DIM:], a_rk, v_c, z)
    g_mat = each(lambda pe, gg: _split(jnp.where(diag, jnp.broadcast_to(pe[0:1, :], (ch, ch)), 0.0)
                                       - gg[:, :HEAD_DIM]), cut(p_end), gz)
    h_mat = each(lambda m, vs, gg: _mm3(_dot_tn, _split(m), vs) - gg[:, HEAD_DIM:], cut(kd), v_c, gz)

    state = [st_ref[hd] for hd in range(H_A)]
    for sub in range(n_sub):
        st_s = each(_split, state)
        ids = range(sub * H_A, (sub + 1) * H_A)
        o_sub = [_mm3(_dot, q_eff[i], s) + o_loc[i] for i, s in zip(ids, st_s)]
        state = [_mm3(_dot, g_mat[i], s) + h_mat[i] for i, s in zip(ids, st_s)]
        for hd in range(H_A):
            oraw_ref[sub * ch:(sub + 1) * ch, hd * HEAD_DIM:(hd + 1) * HEAD_DIM] = o_sub[hd]
    for hd in range(H_A):
        st_ref[hd] = state[hd]

    o_raw = oraw_ref[...]
    oc = o_raw - _head_sum(o_raw, ones) * (1.0 / HEAD_DIM)
    var = _head_sum(oc * oc, ones) * (1.0 / HEAD_DIM)
    o_n = oc * lax.rsqrt(var + RWKV_GN_EPS) * gng_ref[...] + gnb_ref[...]
    bonus = _head_sum(r * kmod * rk_ref[...], ones)
    o_ref[0] = (o_n + bonus * v) * g

    @pl.when(c == pl.num_programs(1) - 1)
    def _():
        sfin_ref[0] = st_ref[...]


def _rwkv(ha, shift_prev, s0_t, p, t_valid, n_sub):
    b, tp, _ = ha.shape
    rows = n_sub * RWKV_CHUNK
    nc = tp // rows
    vec = lambda w: pl.BlockSpec((1, w), lambda i, c: (0, 0))
    mat = lambda r, w: pl.BlockSpec((r, w), lambda i, c: (0, 0))
    lane_head = jnp.arange(W_A) // HEAD_DIM
    ones = (lane_head[:, None] == lane_head[None, :]).astype(BF16)
    kern = functools.partial(_rwkv_kernel, t_valid=None if t_valid == tp else t_valid, n_sub=n_sub)
    return pl.pallas_call(
        kern,
        grid=(b, nc),
        in_specs=[pl.BlockSpec((1, rows, A_COLS), lambda i, c: (i, c, 0)),
                  pl.BlockSpec((1, 1, A_COLS), lambda i, c: (i, 0, 0)),
                  pl.BlockSpec((1, H_A, HEAD_DIM, HEAD_DIM), lambda i, c: (i, 0, 0, 0)),
                  vec(A_COLS), vec(W_A), mat(LORA_W, W_A), vec(W_A), mat(LORA_A, W_A), mat(LORA_G, W_A),
                  vec(W_A), vec(W_A), vec(W_A), vec(W_A), vec(W_A), mat(W_A, W_A)],
        out_specs=[pl.BlockSpec((1, rows, W_A), lambda i, c: (i, c, 0)),
                   pl.BlockSpec((1, H_A, HEAD_DIM, HEAD_DIM), lambda i, c: (i, 0, 0, 0))],
        out_shape=[jax.ShapeDtypeStruct((b, tp, W_A), F32),
                   jax.ShapeDtypeStruct((b, H_A, HEAD_DIM, HEAD_DIM), F32)],
        scratch_shapes=[pltpu.VMEM((rows + 8, A_COLS), F32),
                        pltpu.VMEM((H_A, HEAD_DIM, HEAD_DIM), F32),
                        pltpu.VMEM((rows, W_A), F32)],
        compiler_params=_params(("parallel", "arbitrary")),
        name="rwkv7",
    )(ha, shift_prev[:, None, :], s0_t,
      p['rwkv_mu'][None], p['rwkv_w0'][None], p['rwkv_w_lora'].astype(BF16), p['rwkv_a0'][None],
      p['rwkv_a_lora'].astype(BF16), p['rwkv_g_lora'].astype(BF16), p['rwkv_k_k'][None],
      p['rwkv_k_a'][None], p['rwkv_r_k'].reshape(1, W_A), p['rwkv_gn_g'][None], p['rwkv_gn_b'][None], ones)


ATT_SEG = WIN_MAX


def _attn_prompt_kernel(q_ref, k_ref, v_ref, o_ref, m_ref, l_ref, n_ref):
    seg = pl.program_id(1)
    blk = ATT_BLK
    n_slab = W_B // LANES
    heads = [(j, half) for j in range(n_slab) for half in (0, 1)]
    low = lax.broadcasted_iota(jnp.int32, (blk, LANES), 1) < HEAD_DIM
    qi = lax.broadcasted_iota(jnp.int32, (blk, blk), 0)
    ki = lax.broadcasted_iota(jnp.int32, (blk, blk), 1)
    ok_own = ki <= qi
    ok_prev = ki >= qi
    scale = HEAD_DIM ** -0.5

    def unit(u, carry, *, win, dil, first, last):
        log_d = int(math.log2(dil))
        start = (u >> log_d) * (blk * dil) + (u & (dil - 1))
        k_start = seg * ATT_SEG + start
        p_start = jnp.maximum(k_start - win, 0)
        no_prev = jnp.where(k_start >= win, 0.0, NEG)
        ds = (lambda s: pl.ds(s, blk, stride=dil)) if dil > 1 else (lambda s: pl.ds(s, blk))
        q = [q_ref[j, ds(start), :] for j in range(n_slab)]
        k_own = [k_ref[j, ds(k_start), :].astype(BF16) for j in range(n_slab)]
        k_prev = [k_ref[j, ds(p_start), :].astype(BF16) for j in range(n_slab)]
        v_own = [v_ref[j, ds(k_start), :].astype(BF16) for j in range(n_slab)]
        v_prev = [v_ref[j, ds(p_start), :].astype(BF16) for j in range(n_slab)]
        qh = [jnp.where(low if half == 0 else ~low, q[j], 0.0).astype(BF16) for j, half in heads]
        s_own = [_dot_nt(x, k_own[j]) * scale for x, (j, _) in zip(qh, heads)]
        s_prev = [_dot_nt(x, k_prev[j]) * scale + no_prev for x, (j, _) in zip(qh, heads)]
        s_own = [jnp.where(ok_own, s, NEG) for s in s_own]
        s_prev = [jnp.where(ok_prev, s, NEG) for s in s_prev]
        m = [jnp.maximum(jnp.max(a, axis=-1, keepdims=True), jnp.max(b, axis=-1, keepdims=True))
             for a, b in zip(s_own, s_prev)]
        p_own = [jnp.exp(s - mm) for s, mm in zip(s_own, m)]
        p_prev = [jnp.exp(s - mm) for s, mm in zip(s_prev, m)]
        l = [jnp.sum(a, axis=-1, keepdims=True) + jnp.sum(b, axis=-1, keepdims=True)
             for a, b in zip(p_own, p_prev)]
        acc = [_dot(b.astype(BF16), v_prev[j]) + _dot(a.astype(BF16), v_own[j])
               for a, b, (j, _) in zip(p_own, p_prev, heads)]
        for j in range(n_slab):
            rows = ds(start)
            m_t = jnp.where(low, m[2 * j], m[2 * j + 1])
            l_t = jnp.where(low, l[2 * j], l[2 * j + 1])
            n_t = jnp.where(low, acc[2 * j], acc[2 * j + 1])
            if not first:
                m_o = m_ref[j, rows, :]
                m_n = jnp.maximum(m_o, m_t)
                e_o, e_t = jnp.exp(m_o - m_n), jnp.exp(m_t - m_n)
                l_t = l_ref[j, rows, :] * e_o + l_t * e_t
                n_t = n_ref[j, rows, :] * e_o + n_t * e_t
                m_t = m_n
            if last:
                o_ref[j, rows, :] = n_t / l_t
            else:
                m_ref[j, rows, :] = m_t
                l_ref[j, rows, :] = l_t
                n_ref[j, rows, :] = n_t
        return carry

    for idx, (win, dil) in enumerate(DILATED):
        lax.fori_loop(0, ATT_SEG // blk,
                      functools.partial(unit, win=win, dil=dil, first=idx == 0, last=idx == len(DILATED) - 1), 0)


def _attn_prompt(hb_slabs, bsz, t):
    n_slab = W_B // LANES
    assert t % ATT_SEG == 0 and all(win // dil == ATT_BLK and ATT_SEG % win == 0 for win, dil in DILATED)
    nseg = t // ATT_SEG
    state = pltpu.VMEM((n_slab, ATT_SEG, LANES), F32)
    whole = lambda part: pl.BlockSpec((n_slab, t, LANES), lambda i, s: (part, i, 0), pipeline_mode=pl.Buffered(1))
    seg_spec = pl.BlockSpec((n_slab, ATT_SEG, LANES), lambda i, s: (0, i * nseg + s, 0))
    return pl.pallas_call(
        _attn_prompt_kernel,
        grid=(bsz, nseg),
        in_specs=[seg_spec, whole(1), whole(2)],
        out_specs=seg_spec,
        out_shape=jax.ShapeDtypeStruct((n_slab, bsz * t, LANES), F32),
        scratch_shapes=[state, state, state],
        compiler_params=_params(("parallel", "arbitrary")),
        name="attn_prompt",
    )(hb_slabs, hb_slabs, hb_slabs)


def _kv_tail_kernel(k_ref, v_ref, kt_ref, vt_ref):
    for j in range(W_B // LANES):
        kt_ref[0, j * LANES:(j + 1) * LANES, :] = k_ref[j].T
        vt_ref[0, j * LANES:(j + 1) * LANES, :] = v_ref[j].T


def _kv_tail(hb_slabs, bsz, t, w_keep):
    n_slab = W_B // LANES
    assert t % w_keep == 0
    per = t // w_keep
    tail = lambda part: pl.BlockSpec((n_slab, w_keep, LANES), lambda i: (part, i * per + per - 1, 0))
    out = pl.BlockSpec((1, W_B, w_keep), lambda i: (i, 0, 0))
    return pl.pallas_call(
        _kv_tail_kernel,
        grid=(bsz,),
        in_specs=[tail(1), tail(2)],
        out_specs=[out, out],
        out_shape=[jax.ShapeDtypeStruct((bsz, W_B, w_keep), F32)] * 2,
        compiler_params=_params(("parallel",)),
        name="kv_tail",
    )(hb_slabs, hb_slabs)


def _attn_sample_kernel(hb_ref, ck_ref, cv_ref, *rest, t_new, rows):
    o_ref, nk_ref, nv_ref = rest[-3:]
    w_buf = ck_ref.shape[3]
    hb = hb_ref[0]
    q, k_new, v_new = hb[:, :W_B], hb[:, W_B:2 * W_B], hb[:, 2 * W_B:]
    k_t, v_t = ck_ref[0, 0], cv_ref[0, 0]
    scale = HEAD_DIM ** -0.5
    nrow = H_B * rows

    def iota3(shape, dim):
        return lax.broadcasted_iota(jnp.int32, (H_B, rows) + shape, dim).reshape((nrow,) + shape)

    head_of_lane = iota3((W_B,), 2) >> int(math.log2(HEAD_DIM))
    own_head = iota3((W_B,), 0) == head_of_lane
    q_bd = jnp.where(own_head, jnp.concatenate([q] * H_B, axis=0), 0.0).astype(BF16)
    s_c = _dot(q_bd, k_t.astype(BF16)) * scale
    s_n = _dot_nt(q_bd, k_new.astype(BF16)) * scale
    d_c = w_buf + iota3((w_buf,), 1) - iota3((w_buf,), 2)
    d_n = iota3((rows,), 1) - iota3((rows,), 2)
    new_ok = (d_n >= 0) & (iota3((rows,), 2) < t_new)
    ms, ls, pcs, pns = [], [], [], []
    for win, dil in DILATED:
        sc = jnp.where(((d_c & (dil - 1)) == 0) & (d_c <= win), s_c, NEG)
        sn = jnp.where(new_ok & ((d_n & (dil - 1)) == 0) & (d_n <= win), s_n, NEG)
        m = jnp.maximum(jnp.max(sc, axis=-1, keepdims=True), jnp.max(sn, axis=-1, keepdims=True))
        pc, pn = jnp.exp(sc - m), jnp.exp(sn - m)
        ms.append(m)
        ls.append(jnp.sum(pc, axis=-1, keepdims=True) + jnp.sum(pn, axis=-1, keepdims=True))
        pcs.append(pc.astype(BF16))
        pns.append(pn.astype(BF16))
    acc = _dot_nt(jnp.concatenate(pcs, axis=0), v_t.astype(BF16)) + \
        _dot(jnp.concatenate(pns, axis=0), v_new.astype(BF16))
    m_all = jnp.maximum(jnp.maximum(ms[0], ms[1]), ms[2])
    es = [jnp.exp(m - m_all) for m in ms]
    num = sum(e * acc[g * nrow:(g + 1) * nrow] for g, e in enumerate(es))
    den = sum(e * l for e, l in zip(es, ls))
    o_rows = jnp.where(own_head, num / den, 0.0)
    o_ref[0] = sum(o_rows[h * rows:(h + 1) * rows] for h in range(H_B))

    lane = lax.broadcasted_iota(jnp.int32, (rows, LANES), 1)
    tok = lax.broadcasted_iota(jnp.int32, (rows, LANES), 0)
    place = ((lane == tok + (LANES - t_new)) & (tok < t_new)).astype(BF16)

    def transposed_tail(x):
        hi = x.astype(BF16)
        rem = x - hi.astype(F32)
        mid = rem.astype(BF16)
        lo = (rem - mid.astype(F32)).astype(BF16)
        return _dot_tn(hi, place) + _dot_tn(mid, place) + _dot_tn(lo, place)

    tail_lane = lax.broadcasted_iota(jnp.int32, (W_B, LANES), 1) >= LANES - t_new
    for src, new, dst in ((k_t, k_new, nk_ref), (v_t, v_new, nv_ref)):
        rolled = pltpu.roll(src, w_buf - t_new, 1)
        dst[0, 0, :, 0:w_buf - LANES] = rolled[:, 0:w_buf - LANES]
        dst[0, 0, :, w_buf - LANES:] = jnp.where(tail_lane, transposed_tail(new), rolled[:, w_buf - LANES:])


def _attn_sample(hb, cache_kt, cache_vt, layer, t_new, prev):
    b, rows, _ = hb.shape
    depth, _, _, w_buf = cache_kt.shape
    cspec = pl.BlockSpec((1, 1, W_B, w_buf), lambda i: (layer, i, 0, 0))
    in_specs = [pl.BlockSpec((1, rows, B_COLS), lambda i: (i, 0, 0)), cspec, cspec]
    args = [hb, cache_kt, cache_vt]
    aliases = {}
    if prev is not None:
        in_specs += [pl.BlockSpec(memory_space=pl.ANY)] * 2
        args += list(prev)
        aliases = {3: 1, 4: 2}
    new_shape = jax.ShapeDtypeStruct((depth, b, W_B, w_buf), F32)
    return pl.pallas_call(
        functools.partial(_attn_sample_kernel, t_new=t_new, rows=rows),
        grid=(b,),
        in_specs=in_specs,
        out_specs=[pl.BlockSpec((1, rows, W_B), lambda i: (i, 0, 0)), cspec, cspec],
        out_shape=[jax.ShapeDtypeStruct((b, rows, W_B), F32), new_shape, new_shape],
        input_output_aliases=aliases,
        compiler_params=_params(("parallel",)),
        name="attn_sample",
    )(*args)


def _retention_kernel(q_ref, k_ref, v_ref, g_ref, r0_ref, gng_ref, gnb_ref, o_ref, rfin_ref, st_ref,
                      *, ch):
    c = pl.program_id(1)
    rows = q_ref.shape[1]

    @pl.when(c == 0)
    def _():
        st_ref[...] = r0_ref[0]

    qi = lax.broadcasted_iota(jnp.int32, (rows, rows), 0)
    kj = lax.broadcasted_iota(jnp.int32, (rows, rows), 1)
    rel = (qi - kj).astype(F32)
    pos = lax.broadcasted_iota(jnp.int32, (rows, 1), 0).astype(F32)
    gate = g_ref[0]
    gate = gate * _sigmoid(gate)
    for hd in range(H_C):
        sl = slice(hd * HEAD_DIM, (hd + 1) * HEAD_DIM)
        log_gamma = math.log(1.0 - 2.0 ** (-5.0 - hd))
        dmask = jnp.where(rel >= 0, jnp.exp(log_gamma * jnp.maximum(rel, 0.0)), 0.0)
        kdec = jnp.exp(log_gamma * (ch - 1.0 - pos))
        qdec = jnp.exp(log_gamma * (pos + 1.0))
        q = q_ref[0, :, sl]
        k = k_ref[0, :, sl]
        v = v_ref[0, :, sl].astype(BF16)
        att = _dot_nt(q.astype(BF16), k.astype(BF16)) * dmask
        st = st_ref[hd]
        o = _dot(att.astype(BF16), v) + _dot((q * qdec).astype(BF16), st.astype(BF16))
        kv = _dot_tn((k * kdec).astype(BF16), v)
        st_ref[hd] = st * math.exp(log_gamma * ch) + kv
        mu = jnp.mean(o, axis=-1, keepdims=True)
        oc = o - mu
        var = jnp.mean(oc * oc, axis=-1, keepdims=True)
        o_ref[0, :, sl] = (oc * lax.rsqrt(var + LN_EPS) * gng_ref[:, sl] + gnb_ref[:, sl]) * gate[:, sl]

    @pl.when(c == pl.num_programs(1) - 1)
    def _():
        rfin_ref[0] = st_ref[...]


def _retention(hc, r0, gn_g, gn_b, rows, ch):
    b, tp, _ = hc.shape
    nc = tp // rows
    part = lambda j: pl.BlockSpec((1, rows, W_C), lambda i, c: (i, c, j))
    st_spec = pl.BlockSpec((1, H_C, HEAD_DIM, HEAD_DIM), lambda i, c: (i, 0, 0, 0))
    vec = pl.BlockSpec((1, W_C), lambda i, c: (0, 0))
    return pl.pallas_call(
        functools.partial(_retention_kernel, ch=ch),
        grid=(b, nc),
        in_specs=[part(0), part(1), part(2), part(3), st_spec, vec, vec],
        out_specs=[pl.BlockSpec((1, rows, W_C), lambda i, c: (i, c, 0)), st_spec],
        out_shape=[jax.ShapeDtypeStruct((b, tp, W_C), F32),
                   jax.ShapeDtypeStruct((b, H_C, HEAD_DIM, HEAD_DIM), F32)],
        scratch_shapes=[pltpu.VMEM((H_C, HEAD_DIM, HEAD_DIM), F32)],
        compiler_params=_params(("parallel", "arbitrary")),
        name="retention",
    )(hc, hc, hc, hc, r0, gn_g[None], gn_b[None])


def _out_proj_kernel(oa_ref, ob_ref, oc_ref, x_ref, w_ref, g_ref, b_ref, y_ref, *, alpha):
    mix = _dot(oa_ref[...].astype(BF16), w_ref[0:W_A, :])
    for j in range(W_B // LANES):
        mix += _dot(ob_ref[j].astype(BF16), w_ref[W_A + j * LANES:W_A + (j + 1) * LANES, :])
    mix += _dot(oc_ref[...].astype(BF16), w_ref[W_A + W_B:, :])
    y_ref[...] = _layer_norm(alpha * x_ref[...] + mix, g_ref[...], b_ref[...], LN_EPS)


def _out_proj(oa, ob, oc, x2d, w_bf16, g, b, alpha, bm):
    n, d = x2d.shape
    row = lambda w: pl.BlockSpec((bm, w), lambda i: (i, 0))
    vec = pl.BlockSpec((1, d), lambda i: (0, 0))
    return pl.pallas_call(
        functools.partial(_out_proj_kernel, alpha=alpha),
        grid=(n // bm,),
        in_specs=[row(W_A), pl.BlockSpec((W_B // LANES, bm, LANES), lambda i: (0, i, 0)), row(W_C), row(d),
                  pl.BlockSpec(w_bf16.shape, lambda i: (0, 0)), vec, vec],
        out_specs=row(d),
        out_shape=jax.ShapeDtypeStruct((n, d), F32),
        compiler_params=_params(("parallel",)),
        name="out_proj",
    )(oa, ob, oc, x2d, w_bf16, g[None], b[None])


FFN_CHUNK = 256


def _ffn_kernel(x_ref, wg_ref, wu_ref, wd_ref, g_ref, b_ref, y_ref, *, alpha):
    x = x_ref[...]
    xb = x.astype(BF16)
    d_ff = wg_ref.shape[1]
    acc = alpha * x
    for j in range(d_ff // FFN_CHUNK):
        cs = slice(j * FFN_CHUNK, (j + 1) * FFN_CHUNK)
        gate = _dot(xb, wg_ref[:, cs])
        up = _dot(xb, wu_ref[:, cs])
        act = gate * _sigmoid(gate) * up
        acc += _dot(act.astype(BF16), wd_ref[cs, :])
    y_ref[...] = _layer_norm(acc, g_ref[...], b_ref[...], LN_EPS)


def _ffn(x2d, wg, wu, wd, g, b, alpha, bm):
    n, d = x2d.shape
    row = pl.BlockSpec((bm, d), lambda i: (i, 0))
    vec = pl.BlockSpec((1, d), lambda i: (0, 0))
    full = lambda w: pl.BlockSpec(w.shape, lambda i: (0, 0), pipeline_mode=pl.Buffered(1))
    return pl.pallas_call(
        functools.partial(_ffn_kernel, alpha=alpha),
        grid=(n // bm,),
        in_specs=[row, full(wg), full(wu), full(wd), vec, vec],
        out_specs=row,
        out_shape=jax.ShapeDtypeStruct((n, d), F32),
        compiler_params=_params(("parallel",)),
        name="ffn",
    )(x2d, wg, wu, wd, g[None], b[None])


def _pad_rows(z, rows):
    return jnp.pad(z, ((0, 0), (0, rows - z.shape[1]), (0, 0)))


def _layer(x, shift_prev, wkv0, ret0, pos0, p, wts, alpha, caches=None):
    bsz, t, d = x.shape
    n = bsz * t
    x2d = x.reshape(n, d)
    bm = min(512, n)
    n_slab = W_B // LANES
    pos = pos0 + jnp.arange(t, dtype=F32)
    reps = max(bm // t, 1)
    tile = lambda tabs: tuple(jnp.tile(z, (reps, 1)) for z in tabs)
    inv_b = ROPE_THETA ** (-jnp.arange(0, ROT_DIM, 2, dtype=F32) / ROT_DIM)
    inv_c = 1.0 / (RET_THETA ** jnp.linspace(0.0, 1.0, HEAD_DIM // 2, dtype=F32))
    ha, hb_slabs, hc = _in_proj(x2d, wts['w_in'], tile(_rot_tables(pos, inv_b, ROT_DIM)),
                                tile(_rot_tables(pos, inv_c, HEAD_DIM)), bm)
    ha = ha.reshape(bsz, t, A_COLS)
    hc = hc.reshape(bsz, t, C_COLS)
    shift_new = ha[:, -1]

    tp = -(-t // RWKV_CHUNK) * RWKV_CHUNK
    o_a, wkv_t = _rwkv(_pad_rows(ha, tp), shift_prev, jnp.swapaxes(wkv0, -1, -2), p, t,
                       RWKV_SUBCHUNKS if tp % (RWKV_SUBCHUNKS * RWKV_CHUNK) == 0 else 1)
    o_a = o_a[:, :t].reshape(n, W_A)
    wkv_new = jnp.swapaxes(wkv_t, -1, -2)

    if caches is None:
        o_b = _attn_prompt(hb_slabs, bsz, t)
        k_keep, v_keep = _kv_tail(hb_slabs, bsz, t, min(WIN_MAX, t))
    else:
        cache_kt, cache_vt, layer, prev = caches
        rows = -(-t // 8) * 8
        hb = jnp.swapaxes(hb_slabs, 0, 1).reshape(bsz, t, B_COLS)
        o_b, k_keep, v_keep = _attn_sample(_pad_rows(hb, rows), cache_kt, cache_vt, layer, t, prev)
        o_b = jnp.swapaxes(o_b[:, :t].reshape(n, n_slab, LANES), 0, 1)

    if t % RET_CHUNK == 0:
        o_c, ret_new = _retention(hc, ret0, p['ret_gn_g'], p['ret_gn_b'], RET_CHUNK, RET_CHUNK)
    else:
        rows = -(-t // 8) * 8
        o_c, ret_new = _retention(_pad_rows(hc, rows), ret0, p['ret_gn_g'], p['ret_gn_b'], rows, t)
        o_c = o_c[:, :t]
    o_c = o_c.reshape(n, W_C)

    x1 = _out_proj(o_a, o_b, o_c, x2d, wts['w_out'], p['ln1_g'], p['ln1_b'], alpha, bm)
    x2 = _ffn(x1, wts['w_ffn_gate'], wts['w_ffn_up'], wts['w_ffn_down'], p['ln2_g'], p['ln2_b'], alpha, bm)
    return x2.reshape(bsz, t, d), (shift_new, wkv_new, k_keep, v_keep, ret_new)


def _token_major(z):
    z = z.reshape(z.shape[:-2] + (H_B, HEAD_DIM, z.shape[-1]))
    return jnp.moveaxis(z, -1, -3)


def kernel(x_prompt, x_sample, state_rwkv_shift, state_rwkv_wkv, cache_win_k, cache_win_v, state_ret, w_in, rwkv_mu, rwkv_w0, rwkv_w_lora, rwkv_a0, rwkv_a_lora, rwkv_g_lora, rwkv_k_k, rwkv_k_a, rwkv_r_k, rwkv_gn_g, rwkv_gn_b, ret_gn_g, ret_gn_b, w_out, ln1_g, ln1_b, w_ffn_gate, w_ffn_up, w_ffn_down, ln2_g, ln2_b):
    depth = w_in.shape[0]
    alpha = (2 * depth) ** 0.25
    names = ('rwkv_mu', 'rwkv_w0', 'rwkv_w_lora', 'rwkv_a0', 'rwkv_a_lora', 'rwkv_g_lora', 'rwkv_k_k',
             'rwkv_k_a', 'rwkv_r_k', 'rwkv_gn_g', 'rwkv_gn_b', 'ret_gn_g', 'ret_gn_b', 'ln1_g', 'ln1_b',
             'ln2_g', 'ln2_b')
    vals = (rwkv_mu, rwkv_w0, rwkv_w_lora, rwkv_a0, rwkv_a_lora, rwkv_g_lora, rwkv_k_k, rwkv_k_a, rwkv_r_k,
            rwkv_gn_g, rwkv_gn_b, ret_gn_g, ret_gn_b, ln1_g, ln1_b, ln2_g, ln2_b)
    bp = x_prompt.shape[0]
    to_minor = lambda c: jnp.moveaxis(c, 2, -1).reshape(c.shape[:2] + (W_B, c.shape[2]))
    ckt, cvt = to_minor(cache_win_k), to_minor(cache_win_v)
    xp, xs = x_prompt, x_sample
    p_states, s_states = [], []
    new_caches = None
    for l in range(depth):
        p = {k: v[l] for k, v in zip(names, vals)}
        wts = {'w_in': w_in[l].astype(BF16), 'w_out': w_out[l].astype(BF16),
               'w_ffn_gate': w_ffn_gate[l].astype(BF16), 'w_ffn_up': w_ffn_up[l].astype(BF16),
               'w_ffn_down': w_ffn_down[l].astype(BF16)}
        xp, sp = _layer(xp, jnp.zeros((bp, A_COLS), F32), jnp.zeros((bp, H_A, HEAD_DIM, HEAD_DIM), F32),
                        jnp.zeros((bp, H_C, HEAD_DIM, HEAD_DIM), F32), 0.0, p, wts, alpha)
        xs, ss = _layer(xs, state_rwkv_shift[l], state_rwkv_wkv[l], state_ret[l], float(PAST_LEN), p, wts,
                        alpha, caches=(ckt, cvt, l, new_caches))
        new_caches = (ss[2], ss[3])
        p_states.append(sp)
        s_states.append(ss)
    stack = lambda states, j: jnp.stack([s[j] for s in states])
    p_out = [stack(p_states, j) for j in range(5)]
    s_out = [stack(s_states, j) for j in (0, 1)] + [new_caches[0], new_caches[1], stack(s_states, 4)]
    for out in (p_out, s_out):
        out[2], out[3] = _token_major(out[2]), _token_major(out[3])
    return (xp, xs) + tuple(p_out) + tuple(s_out)
```

```python
import functools
import math

import jax
import jax.numpy as jnp
from jax import lax
from jax.experimental import pallas as pl
from jax.experimental.pallas import tpu as pltpu

F32 = jnp.float32
BF16 = jnp.bfloat16
HI = lax.Precision.HIGHEST

PAST_LEN = 8192
HEAD_DIM = 64
H_A, H_B, H_C = 6, 6, 4
W_A, W_B, W_C = H_A * HEAD_DIM, H_B * HEAD_DIM, H_C * HEAD_DIM
LORA_W, LORA_A, LORA_G = 64, 64, 128
A_COLS = 3 * W_A + LORA_W + LORA_A + LORA_G
B_COLS = 3 * W_B
C_COLS = 4 * W_C
DILATED = ((128, 1), (512, 4), (2048, 16))
WIN_MAX = max(w for w, _ in DILATED)
ATT_BLK = 128
ROT_DIM = HEAD_DIM // 4
ROPE_THETA = 500000.0
RET_THETA = 10000.0
RET_CHUNK = 128
RET_SUBCHUNKS = 2
RWKV_CHUNK = 64
RWKV_SUBCHUNKS = 4
RWKV_SEQS = 8
RWKV_DECAY_SCALE = math.exp(-0.5)
RWKV_GN_EPS = 64e-5
LN_EPS = 1e-5
NEG = -1e30
LANES = 128
VMEM_LIMIT = 56 * 1024 * 1024


def _params(sem, vmem=VMEM_LIMIT):
    return pltpu.CompilerParams(dimension_semantics=sem, vmem_limit_bytes=vmem)


def _dot(a, b, precision=None):
    return jnp.dot(a, b, preferred_element_type=F32, precision=precision)


def _dot_nt(a, b, precision=None):
    return lax.dot_general(a, b, (((1,), (1,)), ((), ())), preferred_element_type=F32, precision=precision)


def _dot_tn(a, b, precision=None):
    return lax.dot_general(a, b, (((0,), (0,)), ((), ())), preferred_element_type=F32, precision=precision)


def _sigmoid(x):
    return 1.0 / (1.0 + jnp.exp(-x))


def _layer_norm(x, g, b, eps):
    mu = jnp.mean(x, axis=-1, keepdims=True)
    xc = x - mu
    var = jnp.mean(xc * xc, axis=-1, keepdims=True)
    return xc * lax.rsqrt(var + eps) * g + b


def _rot_tables(pos, inv_freq, rot_width):
    half = rot_width // 2
    ang = pos[:, None] * inv_freq[None, :]
    cos, sin = jnp.cos(ang), jnp.sin(ang)
    lane = jnp.arange(LANES) % HEAD_DIM
    idx = lane % half
    cos_l = jnp.where(lane[None, :] < rot_width, cos[:, idx], 1.0)
    sin_l = sin[:, idx]
    sin_up = jnp.where(lane[None, :] < half, -sin_l, 0.0)
    sin_dn = jnp.where((lane[None, :] >= half) & (lane[None, :] < rot_width), sin_l, 0.0)
    return cos_l.astype(F32), sin_up.astype(F32), sin_dn.astype(F32)


def _rotate_slab(x, cos, sin_up, sin_dn, half):
    up = pltpu.roll(x, LANES - half, 1)
    dn = pltpu.roll(x, half, 1)
    return x * cos + up * sin_up + dn * sin_dn


def _in_proj_kernel(x_ref, w_ref, cb_ref, ub_ref, db_ref, cc_ref, uc_ref, dc_ref,
                    ha_ref, hb_ref, hc_ref):
    h = _dot(x_ref[...].astype(BF16), w_ref[...])
    ha_ref[...] = h[:, :A_COLS]
    cb, ub, db = cb_ref[...], ub_ref[...], db_ref[...]
    for j in range(B_COLS // LANES):
        slab = h[:, A_COLS + j * LANES:A_COLS + (j + 1) * LANES]
        if j < 2 * W_B // LANES:
            slab = _rotate_slab(slab, cb, ub, db, ROT_DIM // 2)
        hb_ref[j] = slab
    cc, uc, dc = cc_ref[...], uc_ref[...], dc_ref[...]
    c0 = A_COLS + B_COLS
    for j in range(C_COLS // LANES):
        slab = h[:, c0 + j * LANES:c0 + (j + 1) * LANES]
        if j < 2 * W_C // LANES:
            slab = _rotate_slab(slab, cc, uc, dc, HEAD_DIM // 2)
        if W_C // LANES <= j < 2 * W_C // LANES:
            slab = slab * (HEAD_DIM ** -0.5)
        hc_ref[:, j * LANES:(j + 1) * LANES] = slab


def _in_proj(x2d, w_bf16, tabs_b, tabs_c, bm):
    n, d = x2d.shape
    in_cols = w_bf16.shape[1]
    t_rows = tabs_b[0].shape[0]
    nt = t_rows // bm
    tab_spec = pl.BlockSpec((bm, LANES), lambda i: (i % nt, 0))
    row = lambda w: pl.BlockSpec((bm, w), lambda i: (i, 0))
    return pl.pallas_call(
        _in_proj_kernel,
        grid=(n // bm,),
        in_specs=[row(d), pl.BlockSpec((d, in_cols), lambda i: (0, 0))] + [tab_spec] * 6,
        out_specs=[row(A_COLS), pl.BlockSpec((B_COLS // LANES, bm, LANES), lambda i: (0, i, 0)), row(C_COLS)],
        out_shape=[jax.ShapeDtypeStruct((n, A_COLS), F32),
                   jax.ShapeDtypeStruct((B_COLS // LANES, n, LANES), F32),
                   jax.ShapeDtypeStruct((n, C_COLS), F32)],
        compiler_params=_params(("parallel",)),
        name="in_proj",
    )(x2d, w_bf16, *tabs_b, *tabs_c)


assert RWKV_CHUNK == HEAD_DIM


def _split(x):
    hi = x.astype(BF16)
    return hi, (x - hi.astype(F32)).astype(BF16)


def _mm3(dot, a, b):
    return dot(a[0], b[0]) + dot(a[0], b[1]) + dot(a[1], b[0])


def _block_diag(pair, lane_masks):
    first, second = lane_masks
    return tuple(jnp.concatenate([p * first, p * second], axis=0) for p in pair)


def _pair_diag(res, low):
    half = res.shape[0] // 2
    return jnp.where(low, res[:half], res[half:])


def _head_sum(z, ones):
    hi, lo = _split(z)
    return _dot(hi, ones) + _dot(lo, ones)


def _rwkv_kernel(ha_ref, shift_ref, s0_ref, mu_ref, w0_ref, wl_ref, a0_ref, al_ref, gl_ref,
                 kk_ref, ka_ref, rk_ref, gng_ref, gnb_ref, ones_ref, o_ref, sfin_ref,
                 hp_ref, st_ref, oraw_ref, *, t_valid, n_sub, independent):
    c = pl.program_id(1)
    ch = RWKV_CHUNK
    rows = n_sub * ch
    seg_row = lax.broadcasted_iota(jnp.int32, (rows, 1), 0) & (ch - 1)

    @pl.when(c == 0)
    def _():
        if independent:
            hp_ref[0:8, :] = jnp.zeros((8, A_COLS), F32)
        else:
            hp_ref[0:8, :] = jnp.broadcast_to(shift_ref[0], (8, A_COLS))
            st_ref[...] = s0_ref[0]

    h = ha_ref[0]
    hp_ref[8:8 + rows, :] = h
    prev = hp_ref[7:7 + rows, :]
    if independent:
        own_shift = jnp.concatenate([jnp.broadcast_to(shift_ref[0, s:s + 1, :], (ch, A_COLS))
                                     for s in range(n_sub)], axis=0)
        prev = jnp.where(seg_row == 0, own_shift, prev)
    else:
        hp_ref[0:8, :] = h[rows - 8:rows, :]
    xs = h + (prev - h) * mu_ref[...]

    r = xs[:, :W_A]
    k = xs[:, W_A:2 * W_A]
    v = xs[:, 2 * W_A:3 * W_A]
    o0 = 3 * W_A
    xw = xs[:, o0:o0 + LORA_W]
    xa = xs[:, o0 + LORA_W:o0 + LORA_W + LORA_A]
    xg = xs[:, o0 + LORA_W + LORA_A:]
    log_w = -RWKV_DECAY_SCALE * _sigmoid(w0_ref[...] + _dot(jnp.tanh(xw).astype(BF16), wl_ref[...]))
    a = _sigmoid(a0_ref[...] + _dot(xa.astype(BF16), al_ref[...]))
    g = _dot(_sigmoid(xg).astype(BF16), gl_ref[...])
    kk = k * kk_ref[...]
    kmod = k * (1.0 + (a - 1.0) * ka_ref[...])

    if t_valid is not None:
        tok = seg_row if independent else c * rows + lax.broadcasted_iota(jnp.int32, (rows, 1), 0)
        valid = tok < t_valid
        log_w = jnp.where(valid, log_w, 0.0)
        kk = jnp.where(valid, kk, 0.0)
        kmod = jnp.where(valid, kmod, 0.0)
    ones = ones_ref[...]
    kk = kk * lax.rsqrt(jnp.maximum(_head_sum(kk * kk, ones), 1e-12))
    b = kk * a

    ri = lax.broadcasted_iota(jnp.int32, (rows, rows), 0)
    ci = lax.broadcasted_iota(jnp.int32, (rows, rows), 1)
    shift = int(math.log2(ch))
    same = (ri >> shift) == (ci >> shift)
    lower = (same & (ci <= ri)).astype(BF16)
    whole = same.astype(BF16)
    lw_hi = log_w.astype(BF16)
    rem = log_w - lw_hi.astype(F32)
    lw_mid = rem.astype(BF16)
    lw_lo = (rem - lw_mid.astype(F32)).astype(BF16)
    cum = _dot(lower, lw_hi) + _dot(lower, lw_mid) + _dot(lower, lw_lo)
    cum_end = _dot(whole, lw_hi) + _dot(whole, lw_mid) + _dot(whole, lw_lo)
    p_in = jnp.exp(cum)
    p_inv = jnp.exp(-cum)
    to_end = jnp.exp(cum_end - cum)
    p_end = jnp.exp(cum_end)
    kkt = kk * jnp.exp(cum - log_w)
    bt = b * p_inv
    kt = kmod * p_inv
    rt = r * p_in
    b_end = b * to_end
    k_end = kmod * to_end

    row = lax.broadcasted_iota(jnp.int32, (ch, LANES), 0)
    lane = lax.broadcasted_iota(jnp.int32, (ch, LANES), 1)
    col = lane & (ch - 1)
    low = lane < HEAD_DIM
    strict = col < row
    incl = col <= row
    diag = col == row
    off = [((row >> (lvl + 1)) == (col >> (lvl + 1))) & (((row >> lvl) & 1) == 1) & (((col >> lvl) & 1) == 0)
           for lvl in range(shift)]
    halves = (jnp.where(low, 1.0, 0.0).astype(BF16), jnp.where(low, 0.0, 1.0).astype(BF16))
    blk = lambda pair: _block_diag(_split(pair), halves)
    side = lambda u, w: tuple(jnp.concatenate([p, q], axis=1) for p, q in zip(u, w))

    n_pair = W_A // LANES
    chains = [(slice(sub * ch, (sub + 1) * ch), slice(j * LANES, (j + 1) * LANES))
              for sub in range(n_sub) for j in range(n_pair)]
    each = lambda fn, *lists: [fn(*args) for args in zip(*lists)]
    cut = lambda z: [z[rs, sl] for rs, sl in chains]
    kkt_c, rt_c, v_c = cut(kkt), cut(rt), cut(v)
    v_bd = each(blk, v_c)

    def scores(x_kk, x_r, x_b, x_k):
        rhs = tuple(jnp.concatenate([pb * halves[0], pb * halves[1], pk * halves[0], pk * halves[1]], axis=0)
                    for pb, pk in zip(_split(x_b), _split(x_k)))
        return _mm3(_dot_nt, _split(jnp.concatenate([x_kk, x_r], axis=0)), rhs)

    aa = each(scores, kkt_c, rt_c, cut(bt), cut(kt))
    a_kb = each(lambda m: m[:ch, :LANES], aa)
    a_kb_s = each(_split, a_kb)
    a_kk = each(lambda m: _split(jnp.where(strict, m[:ch, LANES:], 0.0)), aa)
    a_rb = each(lambda m: _split(jnp.where(incl, m[ch:, :LANES], 0.0)), aa)
    a_rk = each(lambda m: _split(jnp.where(incl, m[ch:, LANES:], 0.0)), aa)
    t_inv = each(lambda m: jnp.where(diag, 1.0, 0.0) - jnp.where(off[0], m, 0.0), a_kb)
    for lvl in range(1, shift):
        off_b = jnp.where(off[lvl], 1.0, 0.0).astype(BF16)
        inner = each(lambda m, t: blk(_mm3(_dot, (m[0] * off_b, m[1] * off_b), blk(t))), a_kb_s, t_inv)
        t_inv = each(lambda t, inn: t - _mm3(_dot, _split(t), inn), t_inv, inner)
    akv = each(lambda m, vb: _mm3(_dot, m, vb), a_kk, v_bd)
    y = each(lambda t, x, w: _mm3(_dot, _split(t), side(blk(x), blk(w))), t_inv, kkt_c, akv)
    z = each(lambda m, yy: _mm3(_dot, m, side(blk(yy[:, :LANES]), blk(yy[:, LANES:]))), a_rb, y)
    gz = each(lambda m, yy: _mm3(_dot_tn, _split(m), _split(yy)), cut(b_end), y)
    q_eff = each(lambda x, zz: _split(x - zz[:, :LANES]), rt_c, z)
    o_loc = each(lambda m, vb, zz: _mm3(_dot, m, vb) - zz[:, LANES:], a_rk, v_bd, z)
    g_mat = each(lambda pe, gg: _split(jnp.where(diag, jnp.broadcast_to(pe[0:1, :], (ch, LANES)), 0.0)
                                       - _pair_diag(gg[:, :LANES], low)), cut(p_end), gz)
    h_mat = each(lambda m, x, gg: _pair_diag(_mm3(_dot_tn, _split(m), _split(x)), low)
                 - _pair_diag(gg[:, LANES:], low), cut(k_end), v_c, gz)

    if independent:
        state = [s0_ref[sub, j] for sub in range(n_sub) for j in range(n_pair)]
    else:
        state = [st_ref[j] for j in range(n_pair)]
    for sub in range(n_sub):
        ids = range(sub * n_pair, (sub + 1) * n_pair)
        st_bd = each(blk, [state[i] for i in ids] if independent else state)
        for j, (i, s) in enumerate(zip(ids, st_bd)):
            oraw_ref[sub * ch:(sub + 1) * ch, j * LANES:(j + 1) * LANES] = _mm3(_dot, q_eff[i], s) + o_loc[i]
        new = [_mm3(_dot, g_mat[i], s) + h_mat[i] for i, s in zip(ids, st_bd)]
        if independent:
            for j in range(n_pair):
                sfin_ref[sub, j] = new[j]
        else:
            state = new
    if not independent:
        for j in range(n_pair):
            st_ref[j] = state[j]

    o_raw = oraw_ref[...]
    oc = o_raw - _head_sum(o_raw, ones) * (1.0 / HEAD_DIM)
    var = _head_sum(oc * oc, ones) * (1.0 / HEAD_DIM)
    o_n = oc * lax.rsqrt(var + RWKV_GN_EPS) * gng_ref[...] + gnb_ref[...]
    bonus = _head_sum(r * kmod * rk_ref[...], ones)
    o_ref[0] = (o_n + bonus * v) * g

    if not independent:
        @pl.when(c == pl.num_programs(1) - 1)
        def _():
            sfin_ref[0] = st_ref[...]


def _state_to_pairs(s):
    b = s.shape[0]
    s = jnp.swapaxes(s, -1, -2).reshape(b, H_A // 2, 2, HEAD_DIM, HEAD_DIM)
    return jnp.swapaxes(s, 2, 3).reshape(b, H_A // 2, HEAD_DIM, 2 * HEAD_DIM)


def _state_from_pairs(s):
    b = s.shape[0]
    s = jnp.swapaxes(s.reshape(b, H_A // 2, HEAD_DIM, 2, HEAD_DIM), 2, 3)
    return jnp.swapaxes(s.reshape(b, H_A, HEAD_DIM, HEAD_DIM), -1, -2)


def _rwkv(ha, shift_prev, s0_pairs, p, t_valid, n_sub, independent):
    b, tp, _ = ha.shape
    st_shape = (W_A // LANES, HEAD_DIM, LANES)
    rows = n_sub * RWKV_CHUNK
    if independent:
        assert tp == RWKV_CHUNK and b % n_sub == 0
        groups, nc, per = b // n_sub, 1, n_sub
        ha = ha.reshape(groups, rows, A_COLS)
        shift_prev = shift_prev.reshape(groups, n_sub, A_COLS)
    else:
        groups, nc, per = b, tp // rows, 1
        shift_prev = shift_prev[:, None, :]
    vec = lambda w: pl.BlockSpec((1, w), lambda i, c: (0, 0))
    mat = lambda r, w: pl.BlockSpec((r, w), lambda i, c: (0, 0))
    st_spec = pl.BlockSpec((per,) + st_shape, lambda i, c: (i, 0, 0, 0))
    lane_head = jnp.arange(W_A) // HEAD_DIM
    ones = (lane_head[:, None] == lane_head[None, :]).astype(BF16)
    kern = functools.partial(_rwkv_kernel, t_valid=None if t_valid == tp else t_valid, n_sub=n_sub,
                             independent=independent)
    o, s_new = pl.pallas_call(
        kern,
        grid=(groups, nc),
        in_specs=[pl.BlockSpec((1, rows, A_COLS), lambda i, c: (i, c, 0)),
                  pl.BlockSpec((1, per, A_COLS), lambda i, c: (i, 0, 0)),
                  st_spec,
                  vec(A_COLS), vec(W_A), mat(LORA_W, W_A), vec(W_A), mat(LORA_A, W_A), mat(LORA_G, W_A),
                  vec(W_A), vec(W_A), vec(W_A), vec(W_A), vec(W_A), mat(W_A, W_A)],
        out_specs=[pl.BlockSpec((1, rows, W_A), lambda i, c: (i, c, 0)), st_spec],
        out_shape=[jax.ShapeDtypeStruct((groups, nc * rows, W_A), F32),
                   jax.ShapeDtypeStruct((b,) + st_shape, F32)],
        scratch_shapes=[pltpu.VMEM((rows + 8, A_COLS), F32),
                        pltpu.VMEM(st_shape, F32),
                        pltpu.VMEM((rows, W_A), F32)],
        compiler_params=_params(("parallel", "arbitrary")),
        name="rwkv7",
    )(ha, shift_prev, s0_pairs,
      p['rwkv_mu'][None], p['rwkv_w0'][None], p['rwkv_w_lora'].astype(BF16), p['rwkv_a0'][None],
      p['rwkv_a_lora'].astype(BF16), p['rwkv_g_lora'].astype(BF16), p['rwkv_k_k'][None],
      p['rwkv_k_a'][None], p['rwkv_r_k'].reshape(1, W_A), p['rwkv_gn_g'][None], p['rwkv_gn_b'][None], ones)
    return o.reshape(b, tp, W_A), s_new


ATT_SEG = WIN_MAX


def _attn_prompt_kernel(q_ref, k_ref, v_ref, o_ref, m_ref, l_ref, n_ref):
    seg = pl.program_id(1)
    blk = ATT_BLK
    n_slab = W_B // LANES
    heads = [(j, half) for j in range(n_slab) for half in (0, 1)]
    low = lax.broadcasted_iota(jnp.int32, (blk, LANES), 1) < HEAD_DIM
    qi = lax.broadcasted_iota(jnp.int32, (blk, blk), 0)
    ki = lax.broadcasted_iota(jnp.int32, (blk, blk), 1)
    ok_own = ki <= qi
    ok_prev = ki >= qi
    scale = HEAD_DIM ** -0.5

    def unit(u, carry, *, win, dil, first, last):
        log_d = int(math.log2(dil))
        start = (u >> log_d) * (blk * dil) + (u & (dil - 1))
        k_start = seg * ATT_SEG + start
        p_start = jnp.maximum(k_start - win, 0)
        no_prev = jnp.where(k_start >= win, 0.0, NEG)
        ds = (lambda s: pl.ds(s, blk, stride=dil)) if dil > 1 else (lambda s: pl.ds(s, blk))
        q = [q_ref[j, ds(start), :] for j in range(n_slab)]
        k_own = [k_ref[j, ds(k_start), :].astype(BF16) for j in range(n_slab)]
        k_prev = [k_ref[j, ds(p_start), :].astype(BF16) for j in range(n_slab)]
        v_own = [v_ref[j, ds(k_start), :].astype(BF16) for j in range(n_slab)]
        v_prev = [v_ref[j, ds(p_start), :].astype(BF16) for j in range(n_slab)]
        qh = [jnp.where(low if half == 0 else ~low, q[j], 0.0).astype(BF16) for j, half in heads]
        s_own = [_dot_nt(x, k_own[j]) * scale for x, (j, _) in zip(qh, heads)]
        s_prev = [_dot_nt(x, k_prev[j]) * scale + no_prev for x, (j, _) in zip(qh, heads)]
        s_own = [jnp.where(ok_own, s, NEG) for s in s_own]
        s_prev = [jnp.where(ok_prev, s, NEG) for s in s_prev]
        m = [jnp.max(jnp.maximum(a, b), axis=-1, keepdims=True) for a, b in zip(s_own, s_prev)]
        p_own = [jnp.exp(s - mm) for s, mm in zip(s_own, m)]
        p_prev = [jnp.exp(s - mm) for s, mm in zip(s_prev, m)]
        l = [jnp.sum(a + b, axis=-1, keepdims=True) for a, b in zip(p_own, p_prev)]
        acc = [_dot(b.astype(BF16), v_prev[j]) + _dot(a.astype(BF16), v_own[j])
               for a, b, (j, _) in zip(p_own, p_prev, heads)]
        for j in range(n_slab):
            rows = ds(start)
            m_t = jnp.where(low, m[2 * j], m[2 * j + 1])
            l_t = jnp.where(low, l[2 * j], l[2 * j + 1])
            n_t = jnp.where(low, acc[2 * j], acc[2 * j + 1])
            if not first:
                m_o = m_ref[j, rows, :]
                m_n = jnp.maximum(m_o, m_t)
                e_o, e_t = jnp.exp(m_o - m_n), jnp.exp(m_t - m_n)
                l_t = l_ref[j, rows, :] * e_o + l_t * e_t
                n_t = n_ref[j, rows, :] * e_o + n_t * e_t
                m_t = m_n
            if last:
                o_ref[j, rows, :] = n_t / l_t
            else:
                m_ref[j, rows, :] = m_t
                l_ref[j, rows, :] = l_t
                n_ref[j, rows, :] = n_t
        return carry

    for idx, (win, dil) in enumerate(DILATED):
        lax.fori_loop(0, ATT_SEG // blk,
                      functools.partial(unit, win=win, dil=dil, first=idx == 0, last=idx == len(DILATED) - 1), 0)


def _attn_prompt(hb_slabs, bsz, t):
    n_slab = W_B // LANES
    assert t % ATT_SEG == 0 and all(win // dil == ATT_BLK and ATT_SEG % win == 0 for win, dil in DILATED)
    nseg = t // ATT_SEG
    state = pltpu.VMEM((n_slab, ATT_SEG, LANES), F32)
    whole = lambda part: pl.BlockSpec((n_slab, t, LANES), lambda i, s: (part, i, 0), pipeline_mode=pl.Buffered(1))
    seg_spec = pl.BlockSpec((n_slab, ATT_SEG, LANES), lambda i, s: (0, i * nseg + s, 0))
    return pl.pallas_call(
        _attn_prompt_kernel,
        grid=(bsz, nseg),
        in_specs=[seg_spec, whole(1), whole(2)],
        out_specs=seg_spec,
        out_shape=jax.ShapeDtypeStruct((n_slab, bsz * t, LANES), F32),
        scratch_shapes=[state, state, state],
        compiler_params=_params(("parallel", "arbitrary")),
        name="attn_prompt",
    )(hb_slabs, hb_slabs, hb_slabs)


def _kv_tail_kernel(k_ref, v_ref, kt_ref, vt_ref):
    for j in range(W_B // LANES):
        kt_ref[0, j * LANES:(j + 1) * LANES, :] = k_ref[j].T
        vt_ref[0, j * LANES:(j + 1) * LANES, :] = v_ref[j].T


def _kv_tail(hb_slabs, bsz, t, w_keep):
    n_slab = W_B // LANES
    assert t % w_keep == 0
    per = t // w_keep
    tail = lambda part: pl.BlockSpec((n_slab, w_keep, LANES), lambda i: (part, i * per + per - 1, 0))
    out = pl.BlockSpec((1, W_B, w_keep), lambda i: (i, 0, 0))
    return pl.pallas_call(
        _kv_tail_kernel,
        grid=(bsz,),
        in_specs=[tail(1), tail(2)],
        out_specs=[out, out],
        out_shape=[jax.ShapeDtypeStruct((bsz, W_B, w_keep), F32)] * 2,
        compiler_params=_params(("parallel",)),
        name="kv_tail",
    )(hb_slabs, hb_slabs)


def _attn_sample_kernel(hb_ref, ck_ref, cv_ref, *rest, t_new, rows):
    o_ref, nk_ref, nv_ref = rest[-3:]
    w_buf = ck_ref.shape[3]
    hb = hb_ref[0]
    q, k_new, v_new = hb[:, :W_B], hb[:, W_B:2 * W_B], hb[:, 2 * W_B:]
    k_t, v_t = ck_ref[0, 0], cv_ref[0, 0]
    scale = HEAD_DIM ** -0.5
    nrow = H_B * rows

    def iota3(shape, dim):
        return lax.broadcasted_iota(jnp.int32, (H_B, rows) + shape, dim).reshape((nrow,) + shape)

    head_of_lane = iota3((W_B,), 2) >> int(math.log2(HEAD_DIM))
    own_head = iota3((W_B,), 0) == head_of_lane
    q_bd = jnp.where(own_head, jnp.concatenate([q] * H_B, axis=0), 0.0).astype(BF16)
    s_c = _dot(q_bd, k_t.astype(BF16)) * scale
    s_n = _dot_nt(q_bd, k_new.astype(BF16)) * scale
    d_c = w_buf + iota3((w_buf,), 1) - iota3((w_buf,), 2)
    d_n = iota3((rows,), 1) - iota3((rows,), 2)
    new_ok = (d_n >= 0) & (iota3((rows,), 2) < t_new)
    ms, ls, pcs, pns = [], [], [], []
    for win, dil in DILATED:
        sc = jnp.where(((d_c & (dil - 1)) == 0) & (d_c <= win), s_c, NEG)
        sn = jnp.where(new_ok & ((d_n & (dil - 1)) == 0) & (d_n <= win), s_n, NEG)
        m = jnp.maximum(jnp.max(sc, axis=-1, keepdims=True), jnp.max(sn, axis=-1, keepdims=True))
        pc, pn = jnp.exp(sc - m), jnp.exp(sn - m)
        ms.append(m)
        ls.append(jnp.sum(pc, axis=-1, keepdims=True) + jnp.sum(pn, axis=-1, keepdims=True))
        pcs.append(pc.astype(BF16))
        pns.append(pn.astype(BF16))
    acc = _dot_nt(jnp.concatenate(pcs, axis=0), v_t.astype(BF16)) + \
        _dot(jnp.concatenate(pns, axis=0), v_new.astype(BF16))
    m_all = jnp.maximum(jnp.maximum(ms[0], ms[1]), ms[2])
    es = [jnp.exp(m - m_all) for m in ms]
    num = sum(e * acc[g * nrow:(g + 1) * nrow] for g, e in enumerate(es))
    den = sum(e * l for e, l in zip(es, ls))
    o_rows = jnp.where(own_head, num / den, 0.0)
    o_ref[0] = sum(o_rows[h * rows:(h + 1) * rows] for h in range(H_B))

    lane = lax.broadcasted_iota(jnp.int32, (rows, LANES), 1)
    tok = lax.broadcasted_iota(jnp.int32, (rows, LANES), 0)
    place = ((lane == tok + (LANES - t_new)) & (tok < t_new)).astype(BF16)

    def transposed_tail(x):
        hi = x.astype(BF16)
        rem = x - hi.astype(F32)
        mid = rem.astype(BF16)
        lo = (rem - mid.astype(F32)).astype(BF16)
        return _dot_tn(hi, place) + _dot_tn(mid, place) + _dot_tn(lo, place)

    tail_lane = lax.broadcasted_iota(jnp.int32, (W_B, LANES), 1) >= LANES - t_new
    for src, new, dst in ((k_t, k_new, nk_ref), (v_t, v_new, nv_ref)):
        rolled = pltpu.roll(src, w_buf - t_new, 1)
        dst[0, 0, :, 0:w_buf - LANES] = rolled[:, 0:w_buf - LANES]
        dst[0, 0, :, w_buf - LANES:] = jnp.where(tail_lane, transposed_tail(new), rolled[:, w_buf - LANES:])


def _attn_sample(hb, cache_kt, cache_vt, layer, t_new, prev):
    b, rows, _ = hb.shape
    depth, _, _, w_buf = cache_kt.shape
    cspec = pl.BlockSpec((1, 1, W_B, w_buf), lambda i: (layer, i, 0, 0))
    in_specs = [pl.BlockSpec((1, rows, B_COLS), lambda i: (i, 0, 0)), cspec, cspec]
    args = [hb, cache_kt, cache_vt]
    aliases = {}
    if prev is not None:
        in_specs += [pl.BlockSpec(memory_space=pl.ANY)] * 2
        args += list(prev)
        aliases = {3: 1, 4: 2}
    new_shape = jax.ShapeDtypeStruct((depth, b, W_B, w_buf), F32)
    return pl.pallas_call(
        functools.partial(_attn_sample_kernel, t_new=t_new, rows=rows),
        grid=(b,),
        in_specs=in_specs,
        out_specs=[pl.BlockSpec((1, rows, W_B), lambda i: (i, 0, 0)), cspec, cspec],
        out_shape=[jax.ShapeDtypeStruct((b, rows, W_B), F32), new_shape, new_shape],
        input_output_aliases=aliases,
        compiler_params=_params(("parallel",)),
        name="attn_sample",
    )(*args)


def _retention_kernel(q_ref, k_ref, v_ref, g_ref, r0_ref, gng_ref, gnb_ref, ones_ref, o_ref, rfin_ref, st_ref,
                      *, ch, rows, n_sub):
    c = pl.program_id(1)
    n_pair = W_C // LANES

    @pl.when(c == 0)
    def _():
        st_ref[...] = r0_ref[0]

    qi = lax.broadcasted_iota(jnp.int32, (rows, rows), 0)
    kj = lax.broadcasted_iota(jnp.int32, (rows, rows), 1)
    rel = (qi - kj).astype(F32)
    pos = lax.broadcasted_iota(jnp.int32, (rows, LANES), 0).astype(F32)
    low = lax.broadcasted_iota(jnp.int32, (rows, LANES), 1) < HEAD_DIM
    low_st = lax.broadcasted_iota(jnp.int32, (HEAD_DIM, LANES), 1) < HEAD_DIM
    halves = (jnp.where(low_st, 1.0, 0.0).astype(BF16), jnp.where(low_st, 0.0, 1.0).astype(BF16))
    log_gamma = [math.log(1.0 - 2.0 ** (-5.0 - hd)) for hd in range(H_C)]
    dmask = [jnp.where(rel >= 0, jnp.exp(lg * jnp.maximum(rel, 0.0)), 0.0) for lg in log_gamma]
    lg_lane = [jnp.where(low, log_gamma[2 * j], log_gamma[2 * j + 1]) for j in range(n_pair)]
    lg_st = [jnp.where(low_st, log_gamma[2 * j], log_gamma[2 * j + 1]) for j in range(n_pair)]
    ones = ones_ref[...]

    blocks = [(sub, j) for sub in range(n_sub) for j in range(n_pair)]
    tile = lambda ref, sub, j: ref[0, sub * rows:(sub + 1) * rows, j * LANES:(j + 1) * LANES]
    q = [tile(q_ref, sub, j) for sub, j in blocks]
    k = [tile(k_ref, sub, j) for sub, j in blocks]
    v = [tile(v_ref, sub, j).astype(BF16) for sub, j in blocks]
    k_b = [x.astype(BF16) for x in k]
    att = [[_dot_nt(jnp.where(low if half == 0 else ~low, x, 0.0).astype(BF16), kb) * dmask[2 * j + half]
            for half in (0, 1)] for x, kb, (_, j) in zip(q, k_b, blocks)]
    o_intra = [jnp.where(low, _dot(a0.astype(BF16), vv), _dot(a1.astype(BF16), vv)) for (a0, a1), vv in zip(att, v)]
    q_dec = [(x * jnp.exp(lg_lane[j] * (pos + 1.0))).astype(BF16) for x, (_, j) in zip(q, blocks)]
    kv = [_pair_diag(_dot_tn((x * jnp.exp(lg_lane[j] * (ch - 1.0 - pos))).astype(BF16), vv), low_st)
          for x, vv, (_, j) in zip(k, v, blocks)]

    state = [st_ref[j] for j in range(n_pair)]
    outs = []
    for i, (sub, j) in enumerate(blocks):
        st_b = state[j].astype(BF16)
        st_bd = jnp.concatenate([st_b * halves[0], st_b * halves[1]], axis=0)
        outs.append(o_intra[i] + _dot(q_dec[i], st_bd))
        state[j] = state[j] * jnp.exp(lg_st[j] * ch) + kv[i]
    for j in range(n_pair):
        st_ref[j] = state[j]

    for o, (sub, j) in zip(outs, blocks):
        sl = slice(j * LANES, (j + 1) * LANES)
        oc = o - _head_sum(o, ones) * (1.0 / HEAD_DIM)
        var = _head_sum(oc * oc, ones) * (1.0 / HEAD_DIM)
        gate = tile(g_ref, sub, j)
        o_ref[0, sub * rows:(sub + 1) * rows, sl] = \
            (oc * lax.rsqrt(var + LN_EPS) * gng_ref[:, sl] + gnb_ref[:, sl]) * (gate * _sigmoid(gate))

    @pl.when(c == pl.num_programs(1) - 1)
    def _():
        rfin_ref[0] = st_ref[...]


def _ret_to_pairs(s):
    b = s.shape[0]
    s = jnp.swapaxes(s.reshape(b, H_C // 2, 2, HEAD_DIM, HEAD_DIM), 2, 3)
    return s.reshape(b, H_C // 2, HEAD_DIM, 2 * HEAD_DIM)


def _ret_from_pairs(s):
    b = s.shape[0]
    s = jnp.swapaxes(s.reshape(b, H_C // 2, HEAD_DIM, 2, HEAD_DIM), 2, 3)
    return s.reshape(b, H_C, HEAD_DIM, HEAD_DIM)


def _retention(hc, r0, gn_g, gn_b, rows, ch, n_sub):
    b, tp, _ = hc.shape
    nc = tp // (rows * n_sub)
    st_shape = (W_C // LANES, HEAD_DIM, LANES)
    part = lambda j: pl.BlockSpec((1, rows * n_sub, W_C), lambda i, c: (i, c, j))
    st_spec = pl.BlockSpec((1,) + st_shape, lambda i, c: (i, 0, 0, 0))
    vec = pl.BlockSpec((1, W_C), lambda i, c: (0, 0))
    lane_head = jnp.arange(LANES) // HEAD_DIM
    ones = (lane_head[:, None] == lane_head[None, :]).astype(BF16)
    o, r_fin = pl.pallas_call(
        functools.partial(_retention_kernel, ch=ch, rows=rows, n_sub=n_sub),
        grid=(b, nc),
        in_specs=[part(0), part(1), part(2), part(3), st_spec, vec, vec,
                  pl.BlockSpec((LANES, LANES), lambda i, c: (0, 0))],
        out_specs=[pl.BlockSpec((1, rows * n_sub, W_C), lambda i, c: (i, c, 0)), st_spec],
        out_shape=[jax.ShapeDtypeStruct((b, tp, W_C), F32), jax.ShapeDtypeStruct((b,) + st_shape, F32)],
        scratch_shapes=[pltpu.VMEM(st_shape, F32)],
        compiler_params=_params(("parallel", "arbitrary")),
        name="retention",
    )(hc, hc, hc, hc, _ret_to_pairs(r0), gn_g[None], gn_b[None], ones)
    return o, _ret_from_pairs(r_fin)


def _out_proj_kernel(oa_ref, ob_ref, oc_ref, x_ref, w_ref, g_ref, b_ref, y_ref, *, alpha):
    mix = _dot(oa_ref[...].astype(BF16), w_ref[0:W_A, :])
    for j in range(W_B // LANES):
        mix += _dot(ob_ref[j].astype(BF16), w_ref[W_A + j * LANES:W_A + (j + 1) * LANES, :])
    mix += _dot(oc_ref[...].astype(BF16), w_ref[W_A + W_B:, :])
    y_ref[...] = _layer_norm(alpha * x_ref[...] + mix, g_ref[...], b_ref[...], LN_EPS)


def _out_proj(oa, ob, oc, x2d, w_bf16, g, b, alpha, bm):
    n, d = x2d.shape
    row = lambda w: pl.BlockSpec((bm, w), lambda i: (i, 0))
    vec = pl.BlockSpec((1, d), lambda i: (0, 0))
    return pl.pallas_call(
        functools.partial(_out_proj_kernel, alpha=alpha),
        grid=(n // bm,),
        in_specs=[row(W_A), pl.BlockSpec((W_B // LANES, bm, LANES), lambda i: (0, i, 0)), row(W_C), row(d),
                  pl.BlockSpec(w_bf16.shape, lambda i: (0, 0)), vec, vec],
        out_specs=row(d),
        out_shape=jax.ShapeDtypeStruct((n, d), F32),
        compiler_params=_params(("parallel",)),
        name="out_proj",
    )(oa, ob, oc, x2d, w_bf16, g[None], b[None])


FFN_CHUNK = 256


def _ffn_kernel(x_ref, wg_ref, wu_ref, wd_ref, g_ref, b_ref, y_ref, *, alpha):
    x = x_ref[...]
    xb = x.astype(BF16)
    d_ff = wg_ref.shape[1]
    acc = alpha * x
    for j in range(d_ff // FFN_CHUNK):
        cs = slice(j * FFN_CHUNK, (j + 1) * FFN_CHUNK)
        gate = _dot(xb, wg_ref[:, cs])
        up = _dot(xb, wu_ref[:, cs])
        act = gate * _sigmoid(gate) * up
        acc += _dot(act.astype(BF16), wd_ref[cs, :])
    y_ref[...] = _layer_norm(acc, g_ref[...], b_ref[...], LN_EPS)


def _ffn(x2d, wg, wu, wd, g, b, alpha, bm):
    n, d = x2d.shape
    row = pl.BlockSpec((bm, d), lambda i: (i, 0))
    vec = pl.BlockSpec((1, d), lambda i: (0, 0))
    full = lambda w: pl.BlockSpec(w.shape, lambda i: (0, 0), pipeline_mode=pl.Buffered(1))
    return pl.pallas_call(
        functools.partial(_ffn_kernel, alpha=alpha),
        grid=(n // bm,),
        in_specs=[row, full(wg), full(wu), full(wd), vec, vec],
        out_specs=row,
        out_shape=jax.ShapeDtypeStruct((n, d), F32),
        compiler_params=_params(("parallel",)),
        name="ffn",
    )(x2d, wg, wu, wd, g[None], b[None])


def _pad_rows(z, rows):
    return jnp.pad(z, ((0, 0), (0, rows - z.shape[1]), (0, 0)))


def _layer(x, shift_prev, wkv0, ret0, pos0, p, wts, alpha, caches=None):
    bsz, t, d = x.shape
    n = bsz * t
    x2d = x.reshape(n, d)
    bm = min(512, n)
    n_slab = W_B // LANES
    pos = pos0 + jnp.arange(t, dtype=F32)
    reps = max(bm // t, 1)
    tile = lambda tabs: tuple(jnp.tile(z, (reps, 1)) for z in tabs)
    inv_b = ROPE_THETA ** (-jnp.arange(0, ROT_DIM, 2, dtype=F32) / ROT_DIM)
    inv_c = 1.0 / (RET_THETA ** jnp.linspace(0.0, 1.0, HEAD_DIM // 2, dtype=F32))
    ha, hb_slabs, hc = _in_proj(x2d, wts['w_in'], tile(_rot_tables(pos, inv_b, ROT_DIM)),
                                tile(_rot_tables(pos, inv_c, HEAD_DIM)), bm)
    ha = ha.reshape(bsz, t, A_COLS)
    hc = hc.reshape(bsz, t, C_COLS)
    shift_new = ha[:, -1]

    tp = -(-t // RWKV_CHUNK) * RWKV_CHUNK
    if tp % (RWKV_SUBCHUNKS * RWKV_CHUNK) == 0:
        n_sub, independent = RWKV_SUBCHUNKS, False
    elif tp == RWKV_CHUNK and bsz % RWKV_SEQS == 0:
        n_sub, independent = RWKV_SEQS, True
    else:
        n_sub, independent = 1, False
    o_a, wkv_pairs = _rwkv(_pad_rows(ha, tp), shift_prev, _state_to_pairs(wkv0), p, t, n_sub, independent)
    o_a = o_a[:, :t].reshape(n, W_A)
    wkv_new = _state_from_pairs(wkv_pairs)

    if caches is None:
        o_b = _attn_prompt(hb_slabs, bsz, t)
        k_keep, v_keep = _kv_tail(hb_slabs, bsz, t, min(WIN_MAX, t))
    else:
        cache_kt, cache_vt, layer, prev = caches
        rows = -(-t // 8) * 8
        hb = jnp.swapaxes(hb_slabs, 0, 1).reshape(bsz, t, B_COLS)
        o_b, k_keep, v_keep = _attn_sample(_pad_rows(hb, rows), cache_kt, cache_vt, layer, t, prev)
        o_b = jnp.swapaxes(o_b[:, :t].reshape(n, n_slab, LANES), 0, 1)

    if t % RET_CHUNK == 0:
        o_c, ret_new = _retention(hc, ret0, p['ret_gn_g'], p['ret_gn_b'], RET_CHUNK, RET_CHUNK,
                                  RET_SUBCHUNKS if t % (RET_SUBCHUNKS * RET_CHUNK) == 0 else 1)
    else:
        rows = -(-t // 8) * 8
        o_c, ret_new = _retention(_pad_rows(hc, rows), ret0, p['ret_gn_g'], p['ret_gn_b'], rows, t, 1)
        o_c = o_c[:, :t]
    o_c = o_c.reshape(n, W_C)

    x1 = _out_proj(o_a, o_b, o_c, x2d, wts['w_out'], p['ln1_g'], p['ln1_b'], alpha, bm)
    x2 = _ffn(x1, wts['w_ffn_gate'], wts['w_ffn_up'], wts['w_ffn_down'], p['ln2_g'], p['ln2_b'], alpha, bm)
    return x2.reshape(bsz, t, d), (shift_new, wkv_new, k_keep, v_keep, ret_new)


def _token_major(z):
    z = z.reshape(z.shape[:-2] + (H_B, HEAD_DIM, z.shape[-1]))
    return jnp.moveaxis(z, -1, -3)


def kernel(x_prompt, x_sample, state_rwkv_shift, state_rwkv_wkv, cache_win_k, cache_win_v, state_ret, w_in, rwkv_mu, rwkv_w0, rwkv_w_lora, rwkv_a0, rwkv_a_lora, rwkv_g_lora, rwkv_k_k, rwkv_k_a, rwkv_r_k, rwkv_gn_g, rwkv_gn_b, ret_gn_g, ret_gn_b, w_out, ln1_g, ln1_b, w_ffn_gate, w_ffn_up, w_ffn_down, ln2_g, ln2_b):
    depth = w_in.shape[0]
    alpha = (2 * depth) ** 0.25
    names = ('rwkv_mu', 'rwkv_w0', 'rwkv_w_lora', 'rwkv_a0', 'rwkv_a_lora', 'rwkv_g_lora', 'rwkv_k_k',
             'rwkv_k_a', 'rwkv_r_k', 'rwkv_gn_g', 'rwkv_gn_b', 'ret_gn_g', 'ret_gn_b', 'ln1_g', 'ln1_b',
             'ln2_g', 'ln2_b')
    vals = (rwkv_mu, rwkv_w0, rwkv_w_lora, rwkv_a0, rwkv_a_lora, rwkv_g_lora, rwkv_k_k, rwkv_k_a, rwkv_r_k,
            rwkv_gn_g, rwkv_gn_b, ret_gn_g, ret_gn_b, ln1_g, ln1_b, ln2_g, ln2_b)
    bp = x_prompt.shape[0]
    to_minor = lambda c: jnp.moveaxis(c, 2, -1).reshape(c.shape[:2] + (W_B, c.shape[2]))
    ckt, cvt = to_minor(cache_win_k), to_minor(cache_win_v)
    xp, xs = x_prompt, x_sample
    p_states, s_states = [], []
    new_caches = None
    for l in range(depth):
        p = {k: v[l] for k, v in zip(names, vals)}
        wts = {'w_in': w_in[l].astype(BF16), 'w_out': w_out[l].astype(BF16),
               'w_ffn_gate': w_ffn_gate[l].astype(BF16), 'w_ffn_up': w_ffn_up[l].astype(BF16),
               'w_ffn_down': w_ffn_down[l].astype(BF16)}
        xp, sp = _layer(xp, jnp.zeros((bp, A_COLS), F32), jnp.zeros((bp, H_A, HEAD_DIM, HEAD_DIM), F32),
                        jnp.zeros((bp, H_C, HEAD_DIM, HEAD_DIM), F32), 0.0, p, wts, alpha)
        xs, ss = _layer(xs, state_rwkv_shift[l], state_rwkv_wkv[l], state_ret[l], float(PAST_LEN), p, wts,
                        alpha, caches=(ckt, cvt, l, new_caches))
        new_caches = (ss[2], ss[3])
        p_states.append(sp)
        s_states.append(ss)
    stack = lambda states, j: jnp.stack([s[j] for s in states])
    p_out = [stack(p_states, j) for j in range(5)]
    s_out = [stack(s_states, j) for j in (0, 1)] + [new_caches[0], new_caches[1], stack(s_states, 4)]
    for out in (p_out, s_out):
        out[2], out[3] = _token_major(out[2]), _token_major(out[3])
    return (xp, xs) + tuple(p_out) + tuple(s_out)
```

```python
import functools
import math

import jax
import jax.numpy as jnp
from jax import lax
from jax.experimental import pallas as pl
from jax.experimental.pallas import tpu as pltpu

F32 = jnp.float32
BF16 = jnp.bfloat16
HI = lax.Precision.HIGHEST

PAST_LEN = 8192
HEAD_DIM = 64
H_A, H_B, H_C = 6, 6, 4
W_A, W_B, W_C = H_A * HEAD_DIM, H_B * HEAD_DIM, H_C * HEAD_DIM
LORA_W, LORA_A, LORA_G = 64, 64, 128
A_COLS = 3 * W_A + LORA_W + LORA_A + LORA_G
B_COLS = 3 * W_B
C_COLS = 4 * W_C
DILATED = ((128, 1), (512, 4), (2048, 16))
WIN_MAX = max(w for w, _ in DILATED)
ATT_BLK = 128
ROT_DIM = HEAD_DIM // 4
ROPE_THETA = 500000.0
RET_THETA = 10000.0
RET_CHUNK = 128
RET_SUBCHUNKS = 2
RWKV_CHUNK = 64
RWKV_SUBCHUNKS = 4
RWKV_SEQS = 8
RWKV_DECAY_SCALE = math.exp(-0.5)
RWKV_GN_EPS = 64e-5
LN_EPS = 1e-5
NEG = -1e30
LANES = 128
VMEM_LIMIT = 56 * 1024 * 1024


def _params(sem, vmem=VMEM_LIMIT):
    return pltpu.CompilerParams(dimension_semantics=sem, vmem_limit_bytes=vmem)


def _dot(a, b, precision=None):
    return jnp.dot(a, b, preferred_element_type=F32, precision=precision)


def _dot_nt(a, b, precision=None):
    return lax.dot_general(a, b, (((1,), (1,)), ((), ())), preferred_element_type=F32, precision=precision)


def _dot_tn(a, b, precision=None):
    return lax.dot_general(a, b, (((0,), (0,)), ((), ())), preferred_element_type=F32, precision=precision)


def _sigmoid(x):
    return 1.0 / (1.0 + jnp.exp(-x))


def _layer_norm(x, g, b, eps):
    mu = jnp.mean(x, axis=-1, keepdims=True)
    xc = x - mu
    var = jnp.mean(xc * xc, axis=-1, keepdims=True)
    return xc * lax.rsqrt(var + eps) * g + b


def _rot_tables(pos, inv_freq, rot_width):
    half = rot_width // 2
    ang = pos[:, None] * inv_freq[None, :]
    cos, sin = jnp.cos(ang), jnp.sin(ang)
    lane = jnp.arange(LANES) % HEAD_DIM
    idx = lane % half
    cos_l = jnp.where(lane[None, :] < rot_width, cos[:, idx], 1.0)
    sin_l = sin[:, idx]
    sin_up = jnp.where(lane[None, :] < half, -sin_l, 0.0)
    sin_dn = jnp.where((lane[None, :] >= half) & (lane[None, :] < rot_width), sin_l, 0.0)
    return cos_l.astype(F32), sin_up.astype(F32), sin_dn.astype(F32)


def _rotate_slab(x, cos, sin_up, sin_dn, half):
    up = pltpu.roll(x, LANES - half, 1)
    dn = pltpu.roll(x, half, 1)
    return x * cos + up * sin_up + dn * sin_dn


def _in_proj_kernel(x_ref, w_ref, cb_ref, ub_ref, db_ref, cc_ref, uc_ref, dc_ref,
                    ha_ref, hb_ref, hc_ref):
    h = _dot(x_ref[...].astype(BF16), w_ref[...])
    ha_ref[...] = h[:, :A_COLS]
    cb, ub, db = cb_ref[...], ub_ref[...], db_ref[...]
    for j in range(B_COLS // LANES):
        slab = h[:, A_COLS + j * LANES:A_COLS + (j + 1) * LANES]
        if j < 2 * W_B // LANES:
            slab = _rotate_slab(slab, cb, ub, db, ROT_DIM // 2)
        hb_ref[j] = slab
    cc, uc, dc = cc_ref[...], uc_ref[...], dc_ref[...]
    c0 = A_COLS + B_COLS
    for j in range(C_COLS // LANES):
        slab = h[:, c0 + j * LANES:c0 + (j + 1) * LANES]
        if j < 2 * W_C // LANES:
            slab = _rotate_slab(slab, cc, uc, dc, HEAD_DIM // 2)
        if W_C // LANES <= j < 2 * W_C // LANES:
            slab = slab * (HEAD_DIM ** -0.5)
        hc_ref[:, j * LANES:(j + 1) * LANES] = slab


def _in_proj(x2d, w_bf16, tabs_b, tabs_c, bm):
    n, d = x2d.shape
    in_cols = w_bf16.shape[1]
    t_rows = tabs_b[0].shape[0]
    nt = t_rows // bm
    tab_spec = pl.BlockSpec((bm, LANES), lambda i: (i % nt, 0))
    row = lambda w: pl.BlockSpec((bm, w), lambda i: (i, 0))
    return pl.pallas_call(
        _in_proj_kernel,
        grid=(n // bm,),
        in_specs=[row(d), pl.BlockSpec((d, in_cols), lambda i: (0, 0))] + [tab_spec] * 6,
        out_specs=[row(A_COLS), pl.BlockSpec((B_COLS // LANES, bm, LANES), lambda i: (0, i, 0)), row(C_COLS)],
        out_shape=[jax.ShapeDtypeStruct((n, A_COLS), F32),
                   jax.ShapeDtypeStruct((B_COLS // LANES, n, LANES), F32),
                   jax.ShapeDtypeStruct((n, C_COLS), F32)],
        compiler_params=_params(("parallel",)),
        name="in_proj",
    )(x2d, w_bf16, *tabs_b, *tabs_c)


assert RWKV_CHUNK == HEAD_DIM


def _split(x):
    hi = x.astype(BF16)
    return hi, (x - hi.astype(F32)).astype(BF16)


def _mm3(dot, a, b):
    a_axis = 0 if dot is _dot_tn else 1
    b_axis = 1 if dot is _dot_nt else 0
    return dot(jnp.concatenate([a[0], a[1]], axis=a_axis), jnp.concatenate([b[0], b[0]], axis=b_axis)) + \
        dot(a[0], b[1])


def _block_diag(pair, lane_masks):
    first, second = lane_masks
    return tuple(jnp.concatenate([p * first, p * second], axis=0) for p in pair)


def _pair_diag(res, low):
    half = res.shape[0] // 2
    return jnp.where(low, res[:half], res[half:])


def _head_sum(z, ones):
    hi, lo = _split(z)
    return _dot(hi, ones) + _dot(lo, ones)


def _rwkv_kernel(ha_ref, shift_ref, s0_ref, mu_ref, w0_ref, wl_ref, a0_ref, al_ref, gl_ref,
                 kk_ref, ka_ref, rk_ref, gng_ref, gnb_ref, ones_ref, o_ref, sfin_ref,
                 hp_ref, st_ref, oraw_ref, *, t_valid, n_sub, independent):
    c = pl.program_id(1)
    ch = RWKV_CHUNK
    rows = n_sub * ch
    seg_row = lax.broadcasted_iota(jnp.int32, (rows, 1), 0) & (ch - 1)

    @pl.when(c == 0)
    def _():
        if independent:
            hp_ref[0:8, :] = jnp.zeros((8, A_COLS), F32)
        else:
            hp_ref[0:8, :] = jnp.broadcast_to(shift_ref[0], (8, A_COLS))
            st_ref[...] = s0_ref[0]

    h = ha_ref[0]
    hp_ref[8:8 + rows, :] = h
    prev = hp_ref[7:7 + rows, :]
    if independent:
        own_shift = jnp.concatenate([jnp.broadcast_to(shift_ref[0, s:s + 1, :], (ch, A_COLS))
                                     for s in range(n_sub)], axis=0)
        prev = jnp.where(seg_row == 0, own_shift, prev)
    else:
        hp_ref[0:8, :] = h[rows - 8:rows, :]
    xs = h + (prev - h) * mu_ref[...]

    r = xs[:, :W_A]
    k = xs[:, W_A:2 * W_A]
    v = xs[:, 2 * W_A:3 * W_A]
    o0 = 3 * W_A
    xw = xs[:, o0:o0 + LORA_W]
    xa = xs[:, o0 + LORA_W:o0 + LORA_W + LORA_A]
    xg = xs[:, o0 + LORA_W + LORA_A:]
    log_w = -RWKV_DECAY_SCALE * _sigmoid(w0_ref[...] + _dot(jnp.tanh(xw).astype(BF16), wl_ref[...]))
    a = _sigmoid(a0_ref[...] + _dot(xa.astype(BF16), al_ref[...]))
    g = _dot(_sigmoid(xg).astype(BF16), gl_ref[...])
    kk = k * kk_ref[...]
    kmod = k * (1.0 + (a - 1.0) * ka_ref[...])

    if t_valid is not None:
        tok = seg_row if independent else c * rows + lax.broadcasted_iota(jnp.int32, (rows, 1), 0)
        valid = tok < t_valid
        log_w = jnp.where(valid, log_w, 0.0)
        kk = jnp.where(valid, kk, 0.0)
        kmod = jnp.where(valid, kmod, 0.0)
    ones = ones_ref[...]
    kk = kk * lax.rsqrt(jnp.maximum(_head_sum(kk * kk, ones), 1e-12))
    b = kk * a

    ri = lax.broadcasted_iota(jnp.int32, (rows, rows), 0)
    ci = lax.broadcasted_iota(jnp.int32, (rows, rows), 1)
    shift = int(math.log2(ch))
    same = (ri >> shift) == (ci >> shift)
    lower = (same & (ci <= ri)).astype(BF16)
    whole = same.astype(BF16)
    lw_hi = log_w.astype(BF16)
    rem = log_w - lw_hi.astype(F32)
    lw_mid = rem.astype(BF16)
    lw_lo = (rem - lw_mid.astype(F32)).astype(BF16)
    cum = _dot(lower, lw_hi) + _dot(lower, lw_mid) + _dot(lower, lw_lo)
    cum_end = _dot(whole, lw_hi) + _dot(whole, lw_mid) + _dot(whole, lw_lo)
    p_in = jnp.exp(cum)
    p_inv = jnp.exp(-cum)
    to_end = jnp.exp(cum_end - cum)
    p_end = jnp.exp(cum_end)
    kkt = kk * jnp.exp(cum - log_w)
    bt = b * p_inv
    kt = kmod * p_inv
    rt = r * p_in
    b_end = b * to_end
    k_end = kmod * to_end

    row = lax.broadcasted_iota(jnp.int32, (ch, LANES), 0)
    lane = lax.broadcasted_iota(jnp.int32, (ch, LANES), 1)
    col = lane & (ch - 1)
    low = lane < HEAD_DIM
    strict = col < row
    incl = col <= row
    diag = col == row
    off = [((row >> (lvl + 1)) == (col >> (lvl + 1))) & (((row >> lvl) & 1) == 1) & (((col >> lvl) & 1) == 0)
           for lvl in range(shift)]
    halves = (jnp.where(low, 1.0, 0.0).astype(BF16), jnp.where(low, 0.0, 1.0).astype(BF16))
    blk = lambda pair: _block_diag(_split(pair), halves)
    side = lambda u, w: tuple(jnp.concatenate([p, q], axis=1) for p, q in zip(u, w))

    n_pair = W_A // LANES
    chains = [(slice(sub * ch, (sub + 1) * ch), slice(j * LANES, (j + 1) * LANES))
              for sub in range(n_sub) for j in range(n_pair)]
    each = lambda fn, *lists: [fn(*args) for args in zip(*lists)]
    cut = lambda z: [z[rs, sl] for rs, sl in chains]
    kkt_c, rt_c, v_c = cut(kkt), cut(rt), cut(v)
    v_bd = each(blk, v_c)

    def scores(x_kk, x_r, x_b, x_k):
        rhs = tuple(jnp.concatenate([pb * halves[0], pb * halves[1], pk * halves[0], pk * halves[1]], axis=0)
                    for pb, pk in zip(_split(x_b), _split(x_k)))
        return _mm3(_dot_nt, _split(jnp.concatenate([x_kk, x_r], axis=0)), rhs)

    aa = each(scores, kkt_c, rt_c, cut(bt), cut(kt))
    a_kb = each(lambda m: m[:ch, :LANES], aa)
    a_kb_s = each(_split, a_kb)
    a_kk = each(lambda m: _split(jnp.where(strict, m[:ch, LANES:], 0.0)), aa)
    a_rb = each(lambda m: _split(jnp.where(incl, m[ch:, :LANES], 0.0)), aa)
    a_rk = each(lambda m: _split(jnp.where(incl, m[ch:, LANES:], 0.0)), aa)
    t_inv = each(lambda m: jnp.where(diag, 1.0, 0.0) - jnp.where(off[0], m, 0.0), a_kb)
    for lvl in range(1, shift):
        off_b = jnp.where(off[lvl], 1.0, 0.0).astype(BF16)
        inner = each(lambda m, t: blk(_mm3(_dot, (m[0] * off_b, m[1] * off_b), blk(t))), a_kb_s, t_inv)
        t_inv = each(lambda t, inn: t - _mm3(_dot, _split(t), inn), t_inv, inner)
    akv = each(lambda m, vb: _mm3(_dot, m, vb), a_kk, v_bd)
    y = each(lambda t, x, w: _mm3(_dot, _split(t), side(blk(x), blk(w))), t_inv, kkt_c, akv)
    z = each(lambda m, yy: _mm3(_dot, m, side(blk(yy[:, :LANES]), blk(yy[:, LANES:]))), a_rb, y)
    gz = each(lambda m, yy: _mm3(_dot_tn, _split(m), _split(yy)), cut(b_end), y)
    q_eff = each(lambda x, zz: _split(x - zz[:, :LANES]), rt_c, z)
    o_loc = each(lambda m, vb, zz: _mm3(_dot, m, vb) - zz[:, LANES:], a_rk, v_bd, z)
    g_mat = each(lambda pe, gg: _split(jnp.where(diag, jnp.broadcast_to(pe[0:1, :], (ch, LANES)), 0.0)
                                       - _pair_diag(gg[:, :LANES], low)), cut(p_end), gz)
    h_mat = each(lambda m, x, gg: _pair_diag(_mm3(_dot_tn, _split(m), _split(x)), low)
                 - _pair_diag(gg[:, LANES:], low), cut(k_end), v_c, gz)

    if independent:
        state = [s0_ref[sub, j] for sub in range(n_sub) for j in range(n_pair)]
    else:
        state = [st_ref[j] for j in range(n_pair)]
    for sub in range(n_sub):
        ids = range(sub * n_pair, (sub + 1) * n_pair)
        st_bd = each(blk, [state[i] for i in ids] if independent else state)
        for j, (i, s) in enumerate(zip(ids, st_bd)):
            oraw_ref[sub * ch:(sub + 1) * ch, j * LANES:(j + 1) * LANES] = _mm3(_dot, q_eff[i], s) + o_loc[i]
        new = [_mm3(_dot, g_mat[i], s) + h_mat[i] for i, s in zip(ids, st_bd)]
        if independent:
            for j in range(n_pair):
                sfin_ref[sub, j] = new[j]
        else:
            state = new
    if not independent:
        for j in range(n_pair):
            st_ref[j] = state[j]

    o_raw = oraw_ref[...]
    oc = o_raw - _head_sum(o_raw, ones) * (1.0 / HEAD_DIM)
    var = _head_sum(oc * oc, ones) * (1.0 / HEAD_DIM)
    o_n = oc * lax.rsqrt(var + RWKV_GN_EPS) * gng_ref[...] + gnb_ref[...]
    bonus = _head_sum(r * kmod * rk_ref[...], ones)
    o_ref[0] = (o_n + bonus * v) * g

    if not independent:
        @pl.when(c == pl.num_programs(1) - 1)
        def _():
            sfin_ref[0] = st_ref[...]


def _state_to_pairs(s):
    b = s.shape[0]
    s = jnp.swapaxes(s, -1, -2).reshape(b, H_A // 2, 2, HEAD_DIM, HEAD_DIM)
    return jnp.swapaxes(s, 2, 3).reshape(b, H_A // 2, HEAD_DIM, 2 * HEAD_DIM)


def _state_from_pairs(s):
    b = s.shape[0]
    s = jnp.swapaxes(s.reshape(b, H_A // 2, HEAD_DIM, 2, HEAD_DIM), 2, 3)
    return jnp.swapaxes(s.reshape(b, H_A, HEAD_DIM, HEAD_DIM), -1, -2)


def _rwkv(ha, shift_prev, s0_pairs, p, t_valid, n_sub, independent):
    b, tp, _ = ha.shape
    st_shape = (W_A // LANES, HEAD_DIM, LANES)
    rows = n_sub * RWKV_CHUNK
    if independent:
        assert tp == RWKV_CHUNK and b % n_sub == 0
        groups, nc, per = b // n_sub, 1, n_sub
        ha = ha.reshape(groups, rows, A_COLS)
        shift_prev = shift_prev.reshape(groups, n_sub, A_COLS)
    else:
        groups, nc, per = b, tp // rows, 1
        shift_prev = shift_prev[:, None, :]
    vec = lambda w: pl.BlockSpec((1, w), lambda i, c: (0, 0))
    mat = lambda r, w: pl.BlockSpec((r, w), lambda i, c: (0, 0))
    st_spec = pl.BlockSpec((per,) + st_shape, lambda i, c: (i, 0, 0, 0))
    lane_head = jnp.arange(W_A) // HEAD_DIM
    ones = (lane_head[:, None] == lane_head[None, :]).astype(BF16)
    kern = functools.partial(_rwkv_kernel, t_valid=None if t_valid == tp else t_valid, n_sub=n_sub,
                             independent=independent)
    o, s_new = pl.pallas_call(
        kern,
        grid=(groups, nc),
        in_specs=[pl.BlockSpec((1, rows, A_COLS), lambda i, c: (i, c, 0)),
                  pl.BlockSpec((1, per, A_COLS), lambda i, c: (i, 0, 0)),
                  st_spec,
                  vec(A_COLS), vec(W_A), mat(LORA_W, W_A), vec(W_A), mat(LORA_A, W_A), mat(LORA_G, W_A),
                  vec(W_A), vec(W_A), vec(W_A), vec(W_A), vec(W_A), mat(W_A, W_A)],
        out_specs=[pl.BlockSpec((1, rows, W_A), lambda i, c: (i, c, 0)), st_spec],
        out_shape=[jax.ShapeDtypeStruct((groups, nc * rows, W_A), F32),
                   jax.ShapeDtypeStruct((b,) + st_shape, F32)],
        scratch_shapes=[pltpu.VMEM((rows + 8, A_COLS), F32),
                        pltpu.VMEM(st_shape, F32),
                        pltpu.VMEM((rows, W_A), F32)],
        compiler_params=_params(("parallel", "arbitrary")),
        name="rwkv7",
    )(ha, shift_prev, s0_pairs,
      p['rwkv_mu'][None], p['rwkv_w0'][None], p['rwkv_w_lora'].astype(BF16), p['rwkv_a0'][None],
      p['rwkv_a_lora'].astype(BF16), p['rwkv_g_lora'].astype(BF16), p['rwkv_k_k'][None],
      p['rwkv_k_a'][None], p['rwkv_r_k'].reshape(1, W_A), p['rwkv_gn_g'][None], p['rwkv_gn_b'][None], ones)
    return o.reshape(b, tp, W_A), s_new


ATT_SEG = WIN_MAX
ATT_UNITS = 2


def _attn_prompt_kernel(q_ref, k_ref, v_ref, o_ref, m_ref, l_ref, n_ref):
    seg = pl.program_id(1)
    blk = ATT_BLK
    n_slab = W_B // LANES
    heads = [(j, half) for j in range(n_slab) for half in (0, 1)]
    low = lax.broadcasted_iota(jnp.int32, (blk, LANES), 1) < HEAD_DIM
    qi = lax.broadcasted_iota(jnp.int32, (blk, blk), 0)
    ki = lax.broadcasted_iota(jnp.int32, (blk, blk), 1)
    bias_own = jnp.where(ki <= qi, 0.0, NEG)
    ok_prev = ki >= qi
    scale = HEAD_DIM ** -0.5

    def step(it, carry, *, win, dil, first, last):
        log_d = int(math.log2(dil))
        ds = (lambda s: pl.ds(s, blk, stride=dil)) if dil > 1 else (lambda s: pl.ds(s, blk))
        units = [it * ATT_UNITS + i for i in range(ATT_UNITS)]
        start = [(u >> log_d) * (blk * dil) + (u & (dil - 1)) for u in units]
        k_start = [seg * ATT_SEG + s for s in start]
        p_start = [jnp.maximum(s - win, 0) for s in k_start]
        bias = [jnp.concatenate([jnp.where(ok_prev, jnp.where(s >= win, 0.0, NEG), NEG), bias_own], axis=1)
                for s in k_start]
        tiles = [(i, j) for i in range(ATT_UNITS) for j in range(n_slab)]
        chains = [(t, i, j, half) for t, (i, j) in enumerate(tiles) for half in (0, 1)]
        both = lambda ref, i, j: jnp.concatenate([ref[j, ds(p_start[i]), :].astype(BF16),
                                                  ref[j, ds(k_start[i]), :].astype(BF16)], axis=0)
        q = [q_ref[j, ds(start[i]), :] * scale for i, j in tiles]
        k_cat = [both(k_ref, i, j) for i, j in tiles]
        v_cat = [both(v_ref, i, j) for i, j in tiles]
        qh = [jnp.where(low if half == 0 else ~low, q[t], 0.0).astype(BF16) for t, _, _, half in chains]
        s = [_dot_nt(x, k_cat[t]) + bias[i] for x, (t, i, _, _) in zip(qh, chains)]
        m = [jnp.max(x, axis=-1, keepdims=True) for x in s]
        p = [jnp.exp(x - mm) for x, mm in zip(s, m)]
        l = [jnp.sum(x, axis=-1, keepdims=True) for x in p]
        acc = [_dot(x.astype(BF16), v_cat[t]) for x, (t, _, _, _) in zip(p, chains)]
        for t, (i, j) in enumerate(tiles):
            rows = ds(start[i])
            m_t = jnp.where(low, m[2 * t], m[2 * t + 1])
            l_t = jnp.where(low, l[2 * t], l[2 * t + 1])
            n_t = jnp.where(low, acc[2 * t], acc[2 * t + 1])
            if not first:
                m_o = m_ref[j, rows, :]
                m_n = jnp.maximum(m_o, m_t)
                e_o, e_t = jnp.exp(m_o - m_n), jnp.exp(m_t - m_n)
                l_t = l_ref[j, rows, :] * e_o + l_t * e_t
                n_t = n_ref[j, rows, :] * e_o + n_t * e_t
                m_t = m_n
            if last:
                o_ref[j, rows, :] = n_t / l_t
            else:
                m_ref[j, rows, :] = m_t
                l_ref[j, rows, :] = l_t
                n_ref[j, rows, :] = n_t
        return carry

    order = sorted(DILATED, key=lambda wd: -wd[1])
    for idx, (win, dil) in enumerate(order):
        lax.fori_loop(0, ATT_SEG // (blk * ATT_UNITS),
                      functools.partial(step, win=win, dil=dil, first=idx == 0, last=idx == len(order) - 1), 0)


def _attn_prompt(hb_slabs, bsz, t):
    n_slab = W_B // LANES
    assert t % ATT_SEG == 0 and all(win // dil == ATT_BLK and ATT_SEG % win == 0 for win, dil in DILATED)
    nseg = t // ATT_SEG
    state = pltpu.VMEM((n_slab, ATT_SEG, LANES), F32)
    whole = lambda part: pl.BlockSpec((n_slab, t, LANES), lambda i, s: (part, i, 0), pipeline_mode=pl.Buffered(1))
    seg_spec = pl.BlockSpec((n_slab, ATT_SEG, LANES), lambda i, s: (0, i * nseg + s, 0))
    return pl.pallas_call(
        _attn_prompt_kernel,
        grid=(bsz, nseg),
        in_specs=[seg_spec, whole(1), whole(2)],
        out_specs=seg_spec,
        out_shape=jax.ShapeDtypeStruct((n_slab, bsz * t, LANES), F32),
        scratch_shapes=[state, state, state],
        compiler_params=_params(("parallel", "arbitrary")),
        name="attn_prompt",
    )(hb_slabs, hb_slabs, hb_slabs)


def _kv_tail_kernel(k_ref, v_ref, kt_ref, vt_ref):
    for j in range(W_B // LANES):
        kt_ref[0, j * LANES:(j + 1) * LANES, :] = k_ref[j].T
        vt_ref[0, j * LANES:(j + 1) * LANES, :] = v_ref[j].T


def _kv_tail(hb_slabs, bsz, t, w_keep):
    n_slab = W_B // LANES
    assert t % w_keep == 0
    per = t // w_keep
    tail = lambda part: pl.BlockSpec((n_slab, w_keep, LANES), lambda i: (part, i * per + per - 1, 0))
    out = pl.BlockSpec((1, W_B, w_keep), lambda i: (i, 0, 0))
    return pl.pallas_call(
        _kv_tail_kernel,
        grid=(bsz,),
        in_specs=[tail(1), tail(2)],
        out_specs=[out, out],
        out_shape=[jax.ShapeDtypeStruct((bsz, W_B, w_keep), F32)] * 2,
        compiler_params=_params(("parallel",)),
        name="kv_tail",
    )(hb_slabs, hb_slabs)


def _attn_sample_kernel(hb_ref, ck_ref, cv_ref, *rest, t_new, rows):
    o_ref, nk_ref, nv_ref = rest[-3:]
    w_buf = ck_ref.shape[3]
    hb = hb_ref[0]
    q, k_new, v_new = hb[:, :W_B], hb[:, W_B:2 * W_B], hb[:, 2 * W_B:]
    k_t, v_t = ck_ref[0, 0], cv_ref[0, 0]
    scale = HEAD_DIM ** -0.5
    nrow = H_B * rows

    def iota3(shape, dim):
        return lax.broadcasted_iota(jnp.int32, (H_B, rows) + shape, dim).reshape((nrow,) + shape)

    head_of_lane = iota3((W_B,), 2) >> int(math.log2(HEAD_DIM))
    own_head = iota3((W_B,), 0) == head_of_lane
    q_bd = jnp.where(own_head, jnp.concatenate([q] * H_B, axis=0), 0.0).astype(BF16)
    s_c = _dot(q_bd, k_t.astype(BF16)) * scale
    s_n = _dot_nt(q_bd, k_new.astype(BF16)) * scale
    d_c = w_buf + iota3((w_buf,), 1) - iota3((w_buf,), 2)
    d_n = iota3((rows,), 1) - iota3((rows,), 2)
    new_ok = (d_n >= 0) & (iota3((rows,), 2) < t_new)
    ms, ls, pcs, pns = [], [], [], []
    for win, dil in DILATED:
        sc = jnp.where(((d_c & (dil - 1)) == 0) & (d_c <= win), s_c, NEG)
        sn = jnp.where(new_ok & ((d_n & (dil - 1)) == 0) & (d_n <= win), s_n, NEG)
        m = jnp.maximum(jnp.max(sc, axis=-1, keepdims=True), jnp.max(sn, axis=-1, keepdims=True))
        pc, pn = jnp.exp(sc - m), jnp.exp(sn - m)
        ms.append(m)
        ls.append(jnp.sum(pc, axis=-1, keepdims=True) + jnp.sum(pn, axis=-1, keepdims=True))
        pcs.append(pc.astype(BF16))
        pns.append(pn.astype(BF16))
    acc = _dot_nt(jnp.concatenate(pcs, axis=0), v_t.astype(BF16)) + \
        _dot(jnp.concatenate(pns, axis=0), v_new.astype(BF16))
    m_all = jnp.maximum(jnp.maximum(ms[0], ms[1]), ms[2])
    es = [jnp.exp(m - m_all) for m in ms]
    num = sum(e * acc[g * nrow:(g + 1) * nrow] for g, e in enumerate(es))
    den = sum(e * l for e, l in zip(es, ls))
    o_rows = jnp.where(own_head, num / den, 0.0)
    o_ref[0] = sum(o_rows[h * rows:(h + 1) * rows] for h in range(H_B))

    lane = lax.broadcasted_iota(jnp.int32, (rows, LANES), 1)
    tok = lax.broadcasted_iota(jnp.int32, (rows, LANES), 0)
    place = ((lane == tok + (LANES - t_new)) & (tok < t_new)).astype(BF16)

    def transposed_tail(x):
        hi = x.astype(BF16)
        rem = x - hi.astype(F32)
        mid = rem.astype(BF16)
        lo = (rem - mid.astype(F32)).astype(BF16)
        return _dot_tn(hi, place) + _dot_tn(mid, place) + _dot_tn(lo, place)

    tail_lane = lax.broadcasted_iota(jnp.int32, (W_B, LANES), 1) >= LANES - t_new
    for src, new, dst in ((k_t, k_new, nk_ref), (v_t, v_new, nv_ref)):
        rolled = pltpu.roll(src, w_buf - t_new, 1)
        dst[0, 0, :, 0:w_buf - LANES] = rolled[:, 0:w_buf - LANES]
        dst[0, 0, :, w_buf - LANES:] = jnp.where(tail_lane, transposed_tail(new), rolled[:, w_buf - LANES:])


def _attn_sample(hb, cache_kt, cache_vt, layer, t_new, prev):
    b, rows, _ = hb.shape
    depth, _, _, w_buf = cache_kt.shape
    cspec = pl.BlockSpec((1, 1, W_B, w_buf), lambda i: (layer, i, 0, 0))
    in_specs = [pl.BlockSpec((1, rows, B_COLS), lambda i: (i, 0, 0)), cspec, cspec]
    args = [hb, cache_kt, cache_vt]
    aliases = {}
    if prev is not None:
        in_specs += [pl.BlockSpec(memory_space=pl.ANY)] * 2
        args += list(prev)
        aliases = {3: 1, 4: 2}
    new_shape = jax.ShapeDtypeStruct((depth, b, W_B, w_buf), F32)
    return pl.pallas_call(
        functools.partial(_attn_sample_kernel, t_new=t_new, rows=rows),
        grid=(b,),
        in_specs=in_specs,
        out_specs=[pl.BlockSpec((1, rows, W_B), lambda i: (i, 0, 0)), cspec, cspec],
        out_shape=[jax.ShapeDtypeStruct((b, rows, W_B), F32), new_shape, new_shape],
        input_output_aliases=aliases,
        compiler_params=_params(("parallel",)),
        name="attn_sample",
    )(*args)


def _retention_kernel(q_ref, k_ref, v_ref, g_ref, r0_ref, gng_ref, gnb_ref, ones_ref, o_ref, rfin_ref, st_ref,
                      *, ch, rows, n_sub):
    c = pl.program_id(1)
    n_pair = W_C // LANES

    @pl.when(c == 0)
    def _():
        st_ref[...] = r0_ref[0]

    qi = lax.broadcasted_iota(jnp.int32, (rows, rows), 0)
    kj = lax.broadcasted_iota(jnp.int32, (rows, rows), 1)
    rel = (qi - kj).astype(F32)
    pos = lax.broadcasted_iota(jnp.int32, (rows, LANES), 0).astype(F32)
    low = lax.broadcasted_iota(jnp.int32, (rows, LANES), 1) < HEAD_DIM
    low_st = lax.broadcasted_iota(jnp.int32, (HEAD_DIM, LANES), 1) < HEAD_DIM
    halves = (jnp.where(low_st, 1.0, 0.0).astype(BF16), jnp.where(low_st, 0.0, 1.0).astype(BF16))
    log_gamma = [math.log(1.0 - 2.0 ** (-5.0 - hd)) for hd in range(H_C)]
    dmask = [jnp.where(rel >= 0, jnp.exp(lg * jnp.maximum(rel, 0.0)), 0.0) for lg in log_gamma]
    lg_lane = [jnp.where(low, log_gamma[2 * j], log_gamma[2 * j + 1]) for j in range(n_pair)]
    lg_st = [jnp.where(low_st, log_gamma[2 * j], log_gamma[2 * j + 1]) for j in range(n_pair)]
    ones = ones_ref[...]

    blocks = [(sub, j) for sub in range(n_sub) for j in range(n_pair)]
    tile = lambda ref, sub, j: ref[0, sub * rows:(sub + 1) * rows, j * LANES:(j + 1) * LANES]
    q = [tile(q_ref, sub, j) for sub, j in blocks]
    k = [tile(k_ref, sub, j) for sub, j in blocks]
    v = [tile(v_ref, sub, j).astype(BF16) for sub, j in blocks]
    k_b = [x.astype(BF16) for x in k]
    att = [[_dot_nt(jnp.where(low if half == 0 else ~low, x, 0.0).astype(BF16), kb) * dmask[2 * j + half]
            for half in (0, 1)] for x, kb, (_, j) in zip(q, k_b, blocks)]
    o_intra = [jnp.where(low, _dot(a0.astype(BF16), vv), _dot(a1.astype(BF16), vv)) for (a0, a1), vv in zip(att, v)]
    q_dec = [(x * jnp.exp(lg_lane[j] * (pos + 1.0))).astype(BF16) for x, (_, j) in zip(q, blocks)]
    kv = [_pair_diag(_dot_tn((x * jnp.exp(lg_lane[j] * (ch - 1.0 - pos))).astype(BF16), vv), low_st)
          for x, vv, (_, j) in zip(k, v, blocks)]

    state = [st_ref[j] for j in range(n_pair)]
    outs = []
    for i, (sub, j) in enumerate(blocks):
        st_b = state[j].astype(BF16)
        st_bd = jnp.concatenate([st_b * halves[0], st_b * halves[1]], axis=0)
        outs.append(o_intra[i] + _dot(q_dec[i], st_bd))
        state[j] = state[j] * jnp.exp(lg_st[j] * ch) + kv[i]
    for j in range(n_pair):
        st_ref[j] = state[j]

    for o, (sub, j) in zip(outs, blocks):
        sl = slice(j * LANES, (j + 1) * LANES)
        oc = o - _head_sum(o, ones) * (1.0 / HEAD_DIM)
        var = _head_sum(oc * oc, ones) * (1.0 / HEAD_DIM)
        gate = tile(g_ref, sub, j)
        o_ref[0, sub * rows:(sub + 1) * rows, sl] = \
            (oc * lax.rsqrt(var + LN_EPS) * gng_ref[:, sl] + gnb_ref[:, sl]) * (gate * _sigmoid(gate))

    @pl.when(c == pl.num_programs(1) - 1)
    def _():
        rfin_ref[0] = st_ref[...]


def _ret_to_pairs(s):
    b = s.shape[0]
    s = jnp.swapaxes(s.reshape(b, H_C // 2, 2, HEAD_DIM, HEAD_DIM), 2, 3)
    return s.reshape(b, H_C // 2, HEAD_DIM, 2 * HEAD_DIM)


def _ret_from_pairs(s):
    b = s.shape[0]
    s = jnp.swapaxes(s.reshape(b, H_C // 2, HEAD_DIM, 2, HEAD_DIM), 2, 3)
    return s.reshape(b, H_C, HEAD_DIM, HEAD_DIM)


def _retention(hc, r0, gn_g, gn_b, rows, ch, n_sub):
    b, tp, _ = hc.shape
    nc = tp // (rows * n_sub)
    st_shape = (W_C // LANES, HEAD_DIM, LANES)
    part = lambda j: pl.BlockSpec((1, rows * n_sub, W_C), lambda i, c: (i, c, j))
    st_spec = pl.BlockSpec((1,) + st_shape, lambda i, c: (i, 0, 0, 0))
    vec = pl.BlockSpec((1, W_C), lambda i, c: (0, 0))
    lane_head = jnp.arange(LANES) // HEAD_DIM
    ones = (lane_head[:, None] == lane_head[None, :]).astype(BF16)
    o, r_fin = pl.pallas_call(
        functools.partial(_retention_kernel, ch=ch, rows=rows, n_sub=n_sub),
        grid=(b, nc),
        in_specs=[part(0), part(1), part(2), part(3), st_spec, vec, vec,
                  pl.BlockSpec((LANES, LANES), lambda i, c: (0, 0))],
        out_specs=[pl.BlockSpec((1, rows * n_sub, W_C), lambda i, c: (i, c, 0)), st_spec],
        out_shape=[jax.ShapeDtypeStruct((b, tp, W_C), F32), jax.ShapeDtypeStruct((b,) + st_shape, F32)],
        scratch_shapes=[pltpu.VMEM(st_shape, F32)],
        compiler_params=_params(("parallel", "arbitrary")),
        name="retention",
    )(hc, hc, hc, hc, _ret_to_pairs(r0), gn_g[None], gn_b[None], ones)
    return o, _ret_from_pairs(r_fin)


def _out_proj_kernel(oa_ref, ob_ref, oc_ref, x_ref, w_ref, g_ref, b_ref, y_ref, *, alpha):
    mix = _dot(oa_ref[...].astype(BF16), w_ref[0:W_A, :])
    for j in range(W_B // LANES):
        mix += _dot(ob_ref[j].astype(BF16), w_ref[W_A + j * LANES:W_A + (j + 1) * LANES, :])
    mix += _dot(oc_ref[...].astype(BF16), w_ref[W_A + W_B:, :])
    y_ref[...] = _layer_norm(alpha * x_ref[...] + mix, g_ref[...], b_ref[...], LN_EPS)


def _out_proj(oa, ob, oc, x2d, w_bf16, g, b, alpha, bm):
    n, d = x2d.shape
    row = lambda w: pl.BlockSpec((bm, w), lambda i: (i, 0))
    vec = pl.BlockSpec((1, d), lambda i: (0, 0))
    return pl.pallas_call(
        functools.partial(_out_proj_kernel, alpha=alpha),
        grid=(n // bm,),
        in_specs=[row(W_A), pl.BlockSpec((W_B // LANES, bm, LANES), lambda i: (0, i, 0)), row(W_C), row(d),
                  pl.BlockSpec(w_bf16.shape, lambda i: (0, 0)), vec, vec],
        out_specs=row(d),
        out_shape=jax.ShapeDtypeStruct((n, d), F32),
        compiler_params=_params(("parallel",)),
        name="out_proj",
    )(oa, ob, oc, x2d, w_bf16, g[None], b[None])


FFN_CHUNK = 256


def _ffn_kernel(x_ref, wg_ref, wu_ref, wd_ref, g_ref, b_ref, y_ref, *, alpha):
    x = x_ref[...]
    xb = x.astype(BF16)
    d_ff = wg_ref.shape[1]
    acc = alpha * x
    for j in range(d_ff // FFN_CHUNK):
        cs = slice(j * FFN_CHUNK, (j + 1) * FFN_CHUNK)
        gate = _dot(xb, wg_ref[:, cs])
        up = _dot(xb, wu_ref[:, cs])
        act = gate * _sigmoid(gate) * up
        acc += _dot(act.astype(BF16), wd_ref[cs, :])
    y_ref[...] = _layer_norm(acc, g_ref[...], b_ref[...], LN_EPS)


def _ffn(x2d, wg, wu, wd, g, b, alpha, bm):
    n, d = x2d.shape
    row = pl.BlockSpec((bm, d), lambda i: (i, 0))
    vec = pl.BlockSpec((1, d), lambda i: (0, 0))
    full = lambda w: pl.BlockSpec(w.shape, lambda i: (0, 0), pipeline_mode=pl.Buffered(1))
    return pl.pallas_call(
        functools.partial(_ffn_kernel, alpha=alpha),
        grid=(n // bm,),
        in_specs=[row, full(wg), full(wu), full(wd), vec, vec],
        out_specs=row,
        out_shape=jax.ShapeDtypeStruct((n, d), F32),
        compiler_params=_params(("parallel",)),
        name="ffn",
    )(x2d, wg, wu, wd, g[None], b[None])


def _pad_rows(z, rows):
    return jnp.pad(z, ((0, 0), (0, rows - z.shape[1]), (0, 0)))


def _layer(x, shift_prev, wkv0, ret0, pos0, p, wts, alpha, caches=None):
    bsz, t, d = x.shape
    n = bsz * t
    x2d = x.reshape(n, d)
    bm = min(512, n)
    n_slab = W_B // LANES
    pos = pos0 + jnp.arange(t, dtype=F32)
    reps = max(bm // t, 1)
    tile = lambda tabs: tuple(jnp.tile(z, (reps, 1)) for z in tabs)
    inv_b = ROPE_THETA ** (-jnp.arange(0, ROT_DIM, 2, dtype=F32) / ROT_DIM)
    inv_c = 1.0 / (RET_THETA ** jnp.linspace(0.0, 1.0, HEAD_DIM // 2, dtype=F32))
    ha, hb_slabs, hc = _in_proj(x2d, wts['w_in'], tile(_rot_tables(pos, inv_b, ROT_DIM)),
                                tile(_rot_tables(pos, inv_c, HEAD_DIM)), bm)
    ha = ha.reshape(bsz, t, A_COLS)
    hc = hc.reshape(bsz, t, C_COLS)
    shift_new = ha[:, -1]

    tp = -(-t // RWKV_CHUNK) * RWKV_CHUNK
    if tp % (RWKV_SUBCHUNKS * RWKV_CHUNK) == 0:
        n_sub, independent = RWKV_SUBCHUNKS, False
    elif tp == RWKV_CHUNK and bsz % RWKV_SEQS == 0:
        n_sub, independent = RWKV_SEQS, True
    else:
        n_sub, independent = 1, False
    o_a, wkv_pairs = _rwkv(_pad_rows(ha, tp), shift_prev, _state_to_pairs(wkv0), p, t, n_sub, independent)
    o_a = o_a[:, :t].reshape(n, W_A)
    wkv_new = _state_from_pairs(wkv_pairs)

    if caches is None:
        o_b = _attn_prompt(hb_slabs, bsz, t)
        k_keep, v_keep = _kv_tail(hb_slabs, bsz, t, min(WIN_MAX, t))
    else:
        cache_kt, cache_vt, layer, prev = caches
        rows = -(-t // 8) * 8
        hb = jnp.swapaxes(hb_slabs, 0, 1).reshape(bsz, t, B_COLS)
        o_b, k_keep, v_keep = _attn_sample(_pad_rows(hb, rows), cache_kt, cache_vt, layer, t, prev)
        o_b = jnp.swapaxes(o_b[:, :t].reshape(n, n_slab, LANES), 0, 1)

    if t % RET_CHUNK == 0:
        o_c, ret_new = _retention(hc, ret0, p['ret_gn_g'], p['ret_gn_b'], RET_CHUNK, RET_CHUNK,
                                  RET_SUBCHUNKS if t % (RET_SUBCHUNKS * RET_CHUNK) == 0 else 1)
    else:
        rows = -(-t // 8) * 8
        o_c, ret_new = _retention(_pad_rows(hc, rows), ret0, p['ret_gn_g'], p['ret_gn_b'], rows, t, 1)
        o_c = o_c[:, :t]
    o_c = o_c.reshape(n, W_C)

    x1 = _out_proj(o_a, o_b, o_c, x2d, wts['w_out'], p['ln1_g'], p['ln1_b'], alpha, bm)
    x2 = _ffn(x1, wts['w_ffn_gate'], wts['w_ffn_up'], wts['w_ffn_down'], p['ln2_g'], p['ln2_b'], alpha, bm)
    return x2.reshape(bsz, t, d), (shift_new, wkv_new, k_keep, v_keep, ret_new)


def _token_major(z):
    z = z.reshape(z.shape[:-2] + (H_B, HEAD_DIM, z.shape[-1]))
    return jnp.moveaxis(z, -1, -3)


def kernel(x_prompt, x_sample, state_rwkv_shift, state_rwkv_wkv, cache_win_k, cache_win_v, state_ret, w_in, rwkv_mu, rwkv_w0, rwkv_w_lora, rwkv_a0, rwkv_a_lora, rwkv_g_lora, rwkv_k_k, rwkv_k_a, rwkv_r_k, rwkv_gn_g, rwkv_gn_b, ret_gn_g, ret_gn_b, w_out, ln1_g, ln1_b, w_ffn_gate, w_ffn_up, w_ffn_down, ln2_g, ln2_b):
    depth = w_in.shape[0]
    alpha = (2 * depth) ** 0.25
    names = ('rwkv_mu', 'rwkv_w0', 'rwkv_w_lora', 'rwkv_a0', 'rwkv_a_lora', 'rwkv_g_lora', 'rwkv_k_k',
             'rwkv_k_a', 'rwkv_r_k', 'rwkv_gn_g', 'rwkv_gn_b', 'ret_gn_g', 'ret_gn_b', 'ln1_g', 'ln1_b',
             'ln2_g', 'ln2_b')
    vals = (rwkv_mu, rwkv_w0, rwkv_w_lora, rwkv_a0, rwkv_a_lora, rwkv_g_lora, rwkv_k_k, rwkv_k_a, rwkv_r_k,
            rwkv_gn_g, rwkv_gn_b, ret_gn_g, ret_gn_b, ln1_g, ln1_b, ln2_g, ln2_b)
    bp = x_prompt.shape[0]
    to_minor = lambda c: jnp.moveaxis(c, 2, -1).reshape(c.shape[:2] + (W_B, c.shape[2]))
    ckt, cvt = to_minor(cache_win_k), to_minor(cache_win_v)
    xp, xs = x_prompt, x_sample
    p_states, s_states = [], []
    new_caches = None
    for l in range(depth):
        p = {k: v[l] for k, v in zip(names, vals)}
        wts = {'w_in': w_in[l].astype(BF16), 'w_out': w_out[l].astype(BF16),
               'w_ffn_gate': w_ffn_gate[l].astype(BF16), 'w_ffn_up': w_ffn_up[l].astype(BF16),
               'w_ffn_down': w_ffn_down[l].astype(BF16)}
        xp, sp = _layer(xp, jnp.zeros((bp, A_COLS), F32), jnp.zeros((bp, H_A, HEAD_DIM, HEAD_DIM), F32),
                        jnp.zeros((bp, H_C, HEAD_DIM, HEAD_DIM), F32), 0.0, p, wts, alpha)
        xs, ss = _layer(xs, state_rwkv_shift[l], state_rwkv_wkv[l], state_ret[l], float(PAST_LEN), p, wts,
                        alpha, caches=(ckt, cvt, l, new_caches))
        new_caches = (ss[2], ss[3])
        p_states.append(sp)
        s_states.append(ss)
    stack = lambda states, j: jnp.stack([s[j] for s in states])
    p_out = [stack(p_states, j) for j in range(5)]
    s_out = [stack(s_states, j) for j in (0, 1)] + [new_caches[0], new_caches[1], stack(s_states, 4)]
    for out in (p_out, s_out):
        out[2], out[3] = _token_major(out[2]), _token_major(out[3])
    return (xp, xs) + tuple(p_out) + tuple(s_out)
```

```python
import functools
import math

import jax
import jax.numpy as jnp
from jax import lax
from jax.experimental import pallas as pl
from jax.experimental.pallas import tpu as pltpu

F32 = jnp.float32
BF16 = jnp.bfloat16
HI = lax.Precision.HIGHEST

PAST_LEN = 8192
HEAD_DIM = 64
H_A, H_B, H_C = 6, 6, 4
W_A, W_B, W_C = H_A * HEAD_DIM, H_B * HEAD_DIM, H_C * HEAD_DIM
LORA_W, LORA_A, LORA_G = 64, 64, 128
A_COLS = 3 * W_A + LORA_W + LORA_A + LORA_G
B_COLS = 3 * W_B
C_COLS = 4 * W_C
DILATED = ((128, 1), (512, 4), (2048, 16))
WIN_MAX = max(w for w, _ in DILATED)
ATT_BLK = 128
ROT_DIM = HEAD_DIM // 4
ROPE_THETA = 500000.0
RET_THETA = 10000.0
RET_CHUNK = 128
RET_SUBCHUNKS = 2
RWKV_CHUNK = 64
RWKV_SUBCHUNKS = 4
RWKV_SEQS = 8
RWKV_DECAY_SCALE = math.exp(-0.5)
RWKV_GN_EPS = 64e-5
LN_EPS = 1e-5
NEG = -1e30
LANES = 128
VMEM_LIMIT = 56 * 1024 * 1024


def _params(sem, vmem=VMEM_LIMIT):
    return pltpu.CompilerParams(dimension_semantics=sem, vmem_limit_bytes=vmem)


def _layer_spec(arr, layer, **kw):
    zeros = (0,) * (arr.ndim - 1)
    return pl.BlockSpec((None,) + arr.shape[1:], lambda *_: (layer,) + zeros, **kw)


def _dot(a, b, precision=None):
    return jnp.dot(a, b, preferred_element_type=F32, precision=precision)


def _dot_nt(a, b, precision=None):
    return lax.dot_general(a, b, (((1,), (1,)), ((), ())), preferred_element_type=F32, precision=precision)


def _dot_tn(a, b, precision=None):
    return lax.dot_general(a, b, (((0,), (0,)), ((), ())), preferred_element_type=F32, precision=precision)


def _sigmoid(x):
    return 1.0 / (1.0 + jnp.exp(-x))


def _layer_norm(x, g, b, eps):
    mu = jnp.mean(x, axis=-1, keepdims=True)
    xc = x - mu
    var = jnp.mean(xc * xc, axis=-1, keepdims=True)
    return xc * lax.rsqrt(var + eps) * g + b


def _rot_tables(pos, inv_freq, rot_width):
    half = rot_width // 2
    ang = pos[:, None] * inv_freq[None, :]
    cos, sin = jnp.cos(ang), jnp.sin(ang)
    lane = jnp.arange(LANES) % HEAD_DIM
    idx = lane % half
    cos_l = jnp.where(lane[None, :] < rot_width, cos[:, idx], 1.0)
    sin_l = sin[:, idx]
    sin_up = jnp.where(lane[None, :] < half, -sin_l, 0.0)
    sin_dn = jnp.where((lane[None, :] >= half) & (lane[None, :] < rot_width), sin_l, 0.0)
    return cos_l.astype(F32), sin_up.astype(F32), sin_dn.astype(F32)


def _rotate_slab(x, cos, sin_up, sin_dn, half):
    up = pltpu.roll(x, LANES - half, 1)
    dn = pltpu.roll(x, half, 1)
    return x * cos + up * sin_up + dn * sin_dn


def _in_proj_kernel(x_ref, w_ref, cb_ref, ub_ref, db_ref, cc_ref, uc_ref, dc_ref,
                    ha_ref, hb_ref, hc_ref):
    h = _dot(x_ref[...].astype(BF16), w_ref[...])
    ha_ref[...] = h[:, :A_COLS]
    cb, ub, db = cb_ref[...], ub_ref[...], db_ref[...]
    for j in range(B_COLS // LANES):
        slab = h[:, A_COLS + j * LANES:A_COLS + (j + 1) * LANES]
        if j < 2 * W_B // LANES:
            slab = _rotate_slab(slab, cb, ub, db, ROT_DIM // 2)
        hb_ref[j] = slab
    cc, uc, dc = cc_ref[...], uc_ref[...], dc_ref[...]
    c0 = A_COLS + B_COLS
    for j in range(C_COLS // LANES):
        slab = h[:, c0 + j * LANES:c0 + (j + 1) * LANES]
        if j < 2 * W_C // LANES:
            slab = _rotate_slab(slab, cc, uc, dc, HEAD_DIM // 2)
        if W_C // LANES <= j < 2 * W_C // LANES:
            slab = slab * (HEAD_DIM ** -0.5)
        hc_ref[:, j * LANES:(j + 1) * LANES] = slab


def _in_proj(x2d, w_bf16, layer, tabs_b, tabs_c, bm):
    n, d = x2d.shape
    t_rows = tabs_b[0].shape[0]
    nt = t_rows // bm
    tab_spec = pl.BlockSpec((bm, LANES), lambda i: (i % nt, 0))
    row = lambda w: pl.BlockSpec((bm, w), lambda i: (i, 0))
    return pl.pallas_call(
        _in_proj_kernel,
        grid=(n // bm,),
        in_specs=[row(d), _layer_spec(w_bf16, layer)] + [tab_spec] * 6,
        out_specs=[row(A_COLS), pl.BlockSpec((B_COLS // LANES, bm, LANES), lambda i: (0, i, 0)), row(C_COLS)],
        out_shape=[jax.ShapeDtypeStruct((n, A_COLS), F32),
                   jax.ShapeDtypeStruct((B_COLS // LANES, n, LANES), F32),
                   jax.ShapeDtypeStruct((n, C_COLS), F32)],
        compiler_params=_params(("parallel",)),
        name="in_proj",
    )(x2d, w_bf16, *tabs_b, *tabs_c)


assert RWKV_CHUNK == HEAD_DIM


def _split(x):
    hi = x.astype(BF16)
    return hi, (x - hi.astype(F32)).astype(BF16)


def _mm3(dot, a, b):
    a_axis = 0 if dot is _dot_tn else 1
    b_axis = 1 if dot is _dot_nt else 0
    return dot(jnp.concatenate([a[0], a[1]], axis=a_axis), jnp.concatenate([b[0], b[0]], axis=b_axis)) + \
        dot(a[0], b[1])


def _block_diag(pair, lane_masks):
    first, second = lane_masks
    return tuple(jnp.concatenate([p * first, p * second], axis=0) for p in pair)


def _pair_from_state(s_a, s_b):
    return jnp.concatenate([s_a.T, s_b.T], axis=1)


def _store_pair_state(ref, idx, j, pair):
    ref[idx, 2 * j] = pair[:, :HEAD_DIM].T
    ref[idx, 2 * j + 1] = pair[:, HEAD_DIM:].T


def _pair_diag(res, low):
    half = res.shape[0] // 2
    return jnp.where(low, res[:half], res[half:])


def _head_sum(z, ones):
    hi, lo = _split(z)
    return _dot(hi, ones) + _dot(lo, ones)


def _rwkv_kernel(ha_ref, shift_ref, s0_ref, mu_ref, w0_ref, wl_ref, a0_ref, al_ref, gl_ref,
                 kk_ref, ka_ref, rk_ref, gng_ref, gnb_ref, ones_ref, o_ref, sfin_ref,
                 hp_ref, st_ref, oraw_ref, *, t_valid, n_sub, independent):
    c = pl.program_id(1)
    ch = RWKV_CHUNK
    rows = n_sub * ch
    seg_row = lax.broadcasted_iota(jnp.int32, (rows, 1), 0) & (ch - 1)

    @pl.when(c == 0)
    def _():
        if independent:
            hp_ref[0:8, :] = jnp.zeros((8, A_COLS), F32)
        else:
            hp_ref[0:8, :] = jnp.broadcast_to(shift_ref[0], (8, A_COLS))
            for j in range(W_A // LANES):
                st_ref[j] = _pair_from_state(s0_ref[0, 2 * j], s0_ref[0, 2 * j + 1])

    h = ha_ref[0]
    hp_ref[8:8 + rows, :] = h
    prev = hp_ref[7:7 + rows, :]
    if independent:
        own_shift = jnp.concatenate([jnp.broadcast_to(shift_ref[0, s:s + 1, :], (ch, A_COLS))
                                     for s in range(n_sub)], axis=0)
        prev = jnp.where(seg_row == 0, own_shift, prev)
    else:
        hp_ref[0:8, :] = h[rows - 8:rows, :]
    xs = h + (prev - h) * mu_ref[...]

    r = xs[:, :W_A]
    k = xs[:, W_A:2 * W_A]
    v = xs[:, 2 * W_A:3 * W_A]
    o0 = 3 * W_A
    xw = xs[:, o0:o0 + LORA_W]
    xa = xs[:, o0 + LORA_W:o0 + LORA_W + LORA_A]
    xg = xs[:, o0 + LORA_W + LORA_A:]
    log_w = -RWKV_DECAY_SCALE * _sigmoid(w0_ref[...] + _dot(jnp.tanh(xw).astype(BF16), wl_ref[...]))
    a = _sigmoid(a0_ref[...] + _dot(xa.astype(BF16), al_ref[...]))
    g = _dot(_sigmoid(xg).astype(BF16), gl_ref[...])
    kk = k * kk_ref[...]
    kmod = k * (1.0 + (a - 1.0) * ka_ref[...])

    if t_valid is not None:
        tok = seg_row if independent else c * rows + lax.broadcasted_iota(jnp.int32, (rows, 1), 0)
        valid = tok < t_valid
        log_w = jnp.where(valid, log_w, 0.0)
        kk = jnp.where(valid, kk, 0.0)
        kmod = jnp.where(valid, kmod, 0.0)
    ones = ones_ref[...]
    kk = kk * lax.rsqrt(jnp.maximum(_head_sum(kk * kk, ones), 1e-12))
    b = kk * a

    ri = lax.broadcasted_iota(jnp.int32, (rows, rows), 0)
    ci = lax.broadcasted_iota(jnp.int32, (rows, rows), 1)
    shift = int(math.log2(ch))
    same = (ri >> shift) == (ci >> shift)
    lower = (same & (ci <= ri)).astype(BF16)
    whole = same.astype(BF16)
    lw_hi = log_w.astype(BF16)
    rem = log_w - lw_hi.astype(F32)
    lw_mid = rem.astype(BF16)
    lw_lo = (rem - lw_mid.astype(F32)).astype(BF16)
    cum = _dot(lower, lw_hi) + _dot(lower, lw_mid) + _dot(lower, lw_lo)
    cum_end = _dot(whole, lw_hi) + _dot(whole, lw_mid) + _dot(whole, lw_lo)
    p_in = jnp.exp(cum)
    p_inv = jnp.exp(-cum)
    to_end = jnp.exp(cum_end - cum)
    p_end = jnp.exp(cum_end)
    kkt = kk * jnp.exp(cum - log_w)
    bt = b * p_inv
    kt = kmod * p_inv
    rt = r * p_in
    b_end = b * to_end
    k_end = kmod * to_end

    row = lax.broadcasted_iota(jnp.int32, (ch, LANES), 0)
    lane = lax.broadcasted_iota(jnp.int32, (ch, LANES), 1)
    col = lane & (ch - 1)
    low = lane < HEAD_DIM
    strict = col < row
    incl = col <= row
    diag = col == row
    off = [((row >> (lvl + 1)) == (col >> (lvl + 1))) & (((row >> lvl) & 1) == 1) & (((col >> lvl) & 1) == 0)
           for lvl in range(shift)]
    halves = (jnp.where(low, 1.0, 0.0).astype(BF16), jnp.where(low, 0.0, 1.0).astype(BF16))
    blk = lambda pair: _block_diag(_split(pair), halves)
    side = lambda u, w: tuple(jnp.concatenate([p, q], axis=1) for p, q in zip(u, w))

    n_pair = W_A // LANES
    chains = [(slice(sub * ch, (sub + 1) * ch), slice(j * LANES, (j + 1) * LANES))
              for sub in range(n_sub) for j in range(n_pair)]
    each = lambda fn, *lists: [fn(*args) for args in zip(*lists)]
    cut = lambda z: [z[rs, sl] for rs, sl in chains]
    kkt_c, rt_c, v_c = cut(kkt), cut(rt), cut(v)
    v_bd = each(blk, v_c)

    def scores(x_kk, x_r, x_b, x_k):
        rhs = tuple(jnp.concatenate([pb * halves[0], pb * halves[1], pk * halves[0], pk * halves[1]], axis=0)
                    for pb, pk in zip(_split(x_b), _split(x_k)))
        return _mm3(_dot_nt, _split(jnp.concatenate([x_kk, x_r], axis=0)), rhs)

    aa = each(scores, kkt_c, rt_c, cut(bt), cut(kt))
    a_kb = each(lambda m: m[:ch, :LANES], aa)
    a_kb_s = each(_split, a_kb)
    a_kk = each(lambda m: _split(jnp.where(strict, m[:ch, LANES:], 0.0)), aa)
    a_rb = each(lambda m: _split(jnp.where(incl, m[ch:, :LANES], 0.0)), aa)
    a_rk = each(lambda m: _split(jnp.where(incl, m[ch:, LANES:], 0.0)), aa)
    t_inv = each(lambda m: jnp.where(diag, 1.0, 0.0) - jnp.where(off[0], m, 0.0), a_kb)
    for lvl in range(1, shift):
        off_b = jnp.where(off[lvl], 1.0, 0.0).astype(BF16)
        inner = each(lambda m, t: blk(_mm3(_dot, (m[0] * off_b, m[1] * off_b), blk(t))), a_kb_s, t_inv)
        t_inv = each(lambda t, inn: t - _mm3(_dot, _split(t), inn), t_inv, inner)
    akv = each(lambda m, vb: _mm3(_dot, m, vb), a_kk, v_bd)
    y = each(lambda t, x, w: _mm3(_dot, _split(t), side(blk(x), blk(w))), t_inv, kkt_c, akv)
    z = each(lambda m, yy: _mm3(_dot, m, side(blk(yy[:, :LANES]), blk(yy[:, LANES:]))), a_rb, y)
    gz = each(lambda m, yy: _mm3(_dot_tn, _split(m), _split(yy)), cut(b_end), y)
    q_eff = each(lambda x, zz: _split(x - zz[:, :LANES]), rt_c, z)
    o_loc = each(lambda m, vb, zz: _mm3(_dot, m, vb) - zz[:, LANES:], a_rk, v_bd, z)
    g_mat = each(lambda pe, gg: _split(jnp.where(diag, jnp.broadcast_to(pe[0:1, :], (ch, LANES)), 0.0)
                                       - _pair_diag(gg[:, :LANES], low)), cut(p_end), gz)
    h_mat = each(lambda m, x, gg: _pair_diag(_mm3(_dot_tn, _split(m), _split(x)), low)
                 - _pair_diag(gg[:, LANES:], low), cut(k_end), v_c, gz)

    if independent:
        state = [_pair_from_state(s0_ref[sub, 2 * j], s0_ref[sub, 2 * j + 1])
                 for sub in range(n_sub) for j in range(n_pair)]
    else:
        state = [st_ref[j] for j in range(n_pair)]
    for sub in range(n_sub):
        ids = range(sub * n_pair, (sub + 1) * n_pair)
        st_bd = each(blk, [state[i] for i in ids] if independent else state)
        for j, (i, s) in enumerate(zip(ids, st_bd)):
            oraw_ref[sub * ch:(sub + 1) * ch, j * LANES:(j + 1) * LANES] = _mm3(_dot, q_eff[i], s) + o_loc[i]
        new = [_mm3(_dot, g_mat[i], s) + h_mat[i] for i, s in zip(ids, st_bd)]
        if independent:
            for j in range(n_pair):
                _store_pair_state(sfin_ref, sub, j, new[j])
        else:
            state = new
    if not independent:
        for j in range(n_pair):
            st_ref[j] = state[j]

    o_raw = oraw_ref[...]
    oc = o_raw - _head_sum(o_raw, ones) * (1.0 / HEAD_DIM)
    var = _head_sum(oc * oc, ones) * (1.0 / HEAD_DIM)
    o_n = oc * lax.rsqrt(var + RWKV_GN_EPS) * gng_ref[...] + gnb_ref[...]
    bonus = _head_sum(r * kmod * rk_ref[...], ones)
    o_ref[0] = (o_n + bonus * v) * g

    if not independent:
        @pl.when(c == pl.num_programs(1) - 1)
        def _():
            for j in range(n_pair):
                _store_pair_state(sfin_ref, 0, j, st_ref[j])


def _rwkv(ha, shift_all, s0_all, st_idx, p, layer, t_valid, n_sub, independent):
    b, tp, _ = ha.shape
    rows = n_sub * RWKV_CHUNK
    if independent:
        assert tp == RWKV_CHUNK and b % n_sub == 0
        groups, nc, per = b // n_sub, 1, n_sub
        ha = ha.reshape(groups, rows, A_COLS)
    else:
        groups, nc, per = b, tp // rows, 1
    shift_all = shift_all.reshape(shift_all.shape[0], groups, per, A_COLS)
    st_spec = pl.BlockSpec((per, H_A, HEAD_DIM, HEAD_DIM), lambda i, c: (i, 0, 0, 0))
    lane_head = jnp.arange(W_A) // HEAD_DIM
    ones = (lane_head[:, None] == lane_head[None, :]).astype(BF16)
    names = ('rwkv_mu', 'rwkv_w0', 'rwkv_w_lora', 'rwkv_a0', 'rwkv_a_lora', 'rwkv_g_lora', 'rwkv_k_k',
             'rwkv_k_a', 'rwkv_r_k', 'rwkv_gn_g', 'rwkv_gn_b')
    kern = functools.partial(_rwkv_kernel, t_valid=None if t_valid == tp else t_valid, n_sub=n_sub,
                             independent=independent)
    o, s_new = pl.pallas_call(
        kern,
        grid=(groups, nc),
        in_specs=[pl.BlockSpec((1, rows, A_COLS), lambda i, c: (i, c, 0)),
                  pl.BlockSpec((None, 1, per, A_COLS), lambda i, c: (st_idx, i, 0, 0)),
                  pl.BlockSpec((None, per, H_A, HEAD_DIM, HEAD_DIM), lambda i, c: (st_idx, i, 0, 0, 0))] +
                 [_layer_spec(p[k], layer) for k in names] +
                 [pl.BlockSpec((W_A, W_A), lambda i, c: (0, 0))],
        out_specs=[pl.BlockSpec((1, rows, W_A), lambda i, c: (i, c, 0)), st_spec],
        out_shape=[jax.ShapeDtypeStruct((groups, nc * rows, W_A), F32),
                   jax.ShapeDtypeStruct((b, H_A, HEAD_DIM, HEAD_DIM), F32)],
        scratch_shapes=[pltpu.VMEM((rows + 8, A_COLS), F32),
                        pltpu.VMEM((W_A // LANES, HEAD_DIM, LANES), F32),
                        pltpu.VMEM((rows, W_A), F32)],
        compiler_params=_params(("parallel", "arbitrary")),
        name="rwkv7",
    )(ha, shift_all, s0_all, *[p[k] for k in names], ones)
    return o.reshape(b, tp, W_A), s_new


ATT_SEG = WIN_MAX
ATT_UNITS = 2


def _attn_prompt_kernel(q_ref, k_ref, v_ref, o_ref, m_ref, l_ref, n_ref):
    seg = pl.program_id(1)
    blk = ATT_BLK
    n_slab = W_B // LANES
    heads = [(j, half) for j in range(n_slab) for half in (0, 1)]
    low = lax.broadcasted_iota(jnp.int32, (blk, LANES), 1) < HEAD_DIM
    qi = lax.broadcasted_iota(jnp.int32, (blk, blk), 0)
    ki = lax.broadcasted_iota(jnp.int32, (blk, blk), 1)
    bias_own = jnp.where(ki <= qi, 0.0, NEG)
    ok_prev = ki >= qi
    scale = HEAD_DIM ** -0.5

    def step(it, carry, *, win, dil, first, last):
        log_d = int(math.log2(dil))
        ds = (lambda s: pl.ds(s, blk, stride=dil)) if dil > 1 else (lambda s: pl.ds(s, blk))
        units = [it * ATT_UNITS + i for i in range(ATT_UNITS)]
        start = [(u >> log_d) * (blk * dil) + (u & (dil - 1)) for u in units]
        k_start = [seg * ATT_SEG + s for s in start]
        p_start = [jnp.maximum(s - win, 0) for s in k_start]
        bias = [jnp.concatenate([jnp.where(ok_prev, jnp.where(s >= win, 0.0, NEG), NEG), bias_own], axis=1)
                for s in k_start]
        tiles = [(i, j) for i in range(ATT_UNITS) for j in range(n_slab)]
        chains = [(t, i, j, half) for t, (i, j) in enumerate(tiles) for half in (0, 1)]
        both = lambda ref, i, j: jnp.concatenate([ref[j, ds(p_start[i]), :].astype(BF16),
                                                  ref[j, ds(k_start[i]), :].astype(BF16)], axis=0)
        q = [q_ref[j, ds(start[i]), :] * scale for i, j in tiles]
        k_cat = [both(k_ref, i, j) for i, j in tiles]
        v_cat = [both(v_ref, i, j) for i, j in tiles]
        qh = [jnp.where(low if half == 0 else ~low, q[t], 0.0).astype(BF16) for t, _, _, half in chains]
        s = [_dot_nt(x, k_cat[t]) + bias[i] for x, (t, i, _, _) in zip(qh, chains)]
        m = [jnp.max(x, axis=-1, keepdims=True) for x in s]
        p = [jnp.exp(x - mm) for x, mm in zip(s, m)]
        l = [jnp.sum(x, axis=-1, keepdims=True) for x in p]
        acc = [_dot(x.astype(BF16), v_cat[t]) for x, (t, _, _, _) in zip(p, chains)]
        for t, (i, j) in enumerate(tiles):
            rows = ds(start[i])
            m_t = jnp.where(low, m[2 * t], m[2 * t + 1])
            l_t = jnp.where(low, l[2 * t], l[2 * t + 1])
            n_t = jnp.where(low, acc[2 * t], acc[2 * t + 1])
            if not first:
                m_o = m_ref[j, rows, :]
                m_n = jnp.maximum(m_o, m_t)
                e_o, e_t = jnp.exp(m_o - m_n), jnp.exp(m_t - m_n)
                l_t = l_ref[j, rows, :] * e_o + l_t * e_t
                n_t = n_ref[j, rows, :] * e_o + n_t * e_t
                m_t = m_n
            if last:
                o_ref[j, rows, :] = n_t / l_t
            else:
                m_ref[j, rows, :] = m_t
                l_ref[j, rows, :] = l_t
                n_ref[j, rows, :] = n_t
        return carry

    order = sorted(DILATED, key=lambda wd: -wd[1])
    for idx, (win, dil) in enumerate(order):
        lax.fori_loop(0, ATT_SEG // (blk * ATT_UNITS),
                      functools.partial(step, win=win, dil=dil, first=idx == 0, last=idx == len(order) - 1), 0)


def _attn_prompt(hb_slabs, bsz, t):
    n_slab = W_B // LANES
    assert t % ATT_SEG == 0 and all(win // dil == ATT_BLK and ATT_SEG % win == 0 for win, dil in DILATED)
    nseg = t // ATT_SEG
    state = pltpu.VMEM((n_slab, ATT_SEG, LANES), F32)
    whole = lambda part: pl.BlockSpec((n_slab, t, LANES), lambda i, s: (part, i, 0), pipeline_mode=pl.Buffered(1))
    seg_spec = pl.BlockSpec((n_slab, ATT_SEG, LANES), lambda i, s: (0, i * nseg + s, 0))
    return pl.pallas_call(
        _attn_prompt_kernel,
        grid=(bsz, nseg),
        in_specs=[seg_spec, whole(1), whole(2)],
        out_specs=seg_spec,
        out_shape=jax.ShapeDtypeStruct((n_slab, bsz * t, LANES), F32),
        scratch_shapes=[state, state, state],
        compiler_params=_params(("parallel", "arbitrary")),
        name="attn_prompt",
    )(hb_slabs, hb_slabs, hb_slabs)


def _kv_tail_kernel(k_ref, v_ref, *rest):
    kt_ref, vt_ref = rest[-2:]
    for j in range(W_B // LANES):
        kt_ref[0, 0, j * LANES:(j + 1) * LANES, :] = k_ref[j].T
        vt_ref[0, 0, j * LANES:(j + 1) * LANES, :] = v_ref[j].T


def _kv_tail(hb_slabs, bsz, t, w_keep, layer, depth, prev):
    n_slab = W_B // LANES
    assert t % w_keep == 0
    per = t // w_keep
    tail = lambda part: pl.BlockSpec((n_slab, w_keep, LANES), lambda i: (part, i * per + per - 1, 0))
    out = pl.BlockSpec((1, 1, W_B, w_keep), lambda i: (layer, i, 0, 0))
    in_specs, args, aliases = [tail(1), tail(2)], [hb_slabs, hb_slabs], {}
    if prev is not None:
        in_specs += [pl.BlockSpec(memory_space=pl.ANY)] * 2
        args += list(prev)
        aliases = {2: 0, 3: 1}
    return pl.pallas_call(
        _kv_tail_kernel,
        grid=(bsz,),
        in_specs=in_specs,
        out_specs=[out, out],
        out_shape=[jax.ShapeDtypeStruct((depth, bsz, W_B, w_keep), F32)] * 2,
        input_output_aliases=aliases,
        compiler_params=_params(("parallel",)),
        name="kv_tail",
    )(*args)


def _attn_sample_kernel(hb_ref, ck_ref, cv_ref, *rest, t_new, rows):
    o_ref, nk_ref, nv_ref = rest[-3:]
    w_buf = ck_ref.shape[3]
    hb = hb_ref[0]
    q, k_new, v_new = hb[:, :W_B], hb[:, W_B:2 * W_B], hb[:, 2 * W_B:]
    k_t, v_t = ck_ref[0, 0], cv_ref[0, 0]
    scale = HEAD_DIM ** -0.5
    nrow = H_B * rows

    def iota3(shape, dim):
        return lax.broadcasted_iota(jnp.int32, (H_B, rows) + shape, dim).reshape((nrow,) + shape)

    head_of_lane = iota3((W_B,), 2) >> int(math.log2(HEAD_DIM))
    own_head = iota3((W_B,), 0) == head_of_lane
    q_bd = jnp.where(own_head, jnp.concatenate([q] * H_B, axis=0), 0.0).astype(BF16)
    s_c = _dot(q_bd, k_t.astype(BF16)) * scale
    s_n = _dot_nt(q_bd, k_new.astype(BF16)) * scale
    d_c = w_buf + iota3((w_buf,), 1) - iota3((w_buf,), 2)
    d_n = iota3((rows,), 1) - iota3((rows,), 2)
    new_ok = (d_n >= 0) & (iota3((rows,), 2) < t_new)
    ms, ls, pcs, pns = [], [], [], []
    for win, dil in DILATED:
        sc = jnp.where(((d_c & (dil - 1)) == 0) & (d_c <= win), s_c, NEG)
        sn = jnp.where(new_ok & ((d_n & (dil - 1)) == 0) & (d_n <= win), s_n, NEG)
        m = jnp.maximum(jnp.max(sc, axis=-1, keepdims=True), jnp.max(sn, axis=-1, keepdims=True))
        pc, pn = jnp.exp(sc - m), jnp.exp(sn - m)
        ms.append(m)
        ls.append(jnp.sum(pc, axis=-1, keepdims=True) + jnp.sum(pn, axis=-1, keepdims=True))
        pcs.append(pc.astype(BF16))
        pns.append(pn.astype(BF16))
    acc = _dot_nt(jnp.concatenate(pcs, axis=0), v_t.astype(BF16)) + \
        _dot(jnp.concatenate(pns, axis=0), v_new.astype(BF16))
    m_all = jnp.maximum(jnp.maximum(ms[0], ms[1]), ms[2])
    es = [jnp.exp(m - m_all) for m in ms]
    num = sum(e * acc[g * nrow:(g + 1) * nrow] for g, e in enumerate(es))
    den = sum(e * l for e, l in zip(es, ls))
    o_rows = jnp.where(own_head, num / den, 0.0)
    o_ref[0] = sum(o_rows[h * rows:(h + 1) * rows] for h in range(H_B))

    lane = lax.broadcasted_iota(jnp.int32, (rows, LANES), 1)
    tok = lax.broadcasted_iota(jnp.int32, (rows, LANES), 0)
    place = ((lane == tok + (LANES - t_new)) & (tok < t_new)).astype(BF16)

    def transposed_tail(x):
        hi = x.astype(BF16)
        rem = x - hi.astype(F32)
        mid = rem.astype(BF16)
        lo = (rem - mid.astype(F32)).astype(BF16)
        return _dot_tn(hi, place) + _dot_tn(mid, place) + _dot_tn(lo, place)

    tail_lane = lax.broadcasted_iota(jnp.int32, (W_B, LANES), 1) >= LANES - t_new
    for src, new, dst in ((k_t, k_new, nk_ref), (v_t, v_new, nv_ref)):
        rolled = pltpu.roll(src, w_buf - t_new, 1)
        dst[0, 0, :, 0:w_buf - LANES] = rolled[:, 0:w_buf - LANES]
        dst[0, 0, :, w_buf - LANES:] = jnp.where(tail_lane, transposed_tail(new), rolled[:, w_buf - LANES:])


def _attn_sample(hb, cache_kt, cache_vt, layer, t_new, prev):
    b, rows, _ = hb.shape
    depth, _, _, w_buf = cache_kt.shape
    cspec = pl.BlockSpec((1, 1, W_B, w_buf), lambda i: (layer, i, 0, 0))
    in_specs = [pl.BlockSpec((1, rows, B_COLS), lambda i: (i, 0, 0)), cspec, cspec]
    args = [hb, cache_kt, cache_vt]
    aliases = {}
    if prev is not None:
        in_specs += [pl.BlockSpec(memory_space=pl.ANY)] * 2
        args += list(prev)
        aliases = {3: 1, 4: 2}
    new_shape = jax.ShapeDtypeStruct((depth, b, W_B, w_buf), F32)
    return pl.pallas_call(
        functools.partial(_attn_sample_kernel, t_new=t_new, rows=rows),
        grid=(b,),
        in_specs=in_specs,
        out_specs=[pl.BlockSpec((1, rows, W_B), lambda i: (i, 0, 0)), cspec, cspec],
        out_shape=[jax.ShapeDtypeStruct((b, rows, W_B), F32), new_shape, new_shape],
        input_output_aliases=aliases,
        compiler_params=_params(("parallel",)),
        name="attn_sample",
    )(*args)


def _retention_kernel(q_ref, k_ref, v_ref, g_ref, r0_ref, gng_ref, gnb_ref, ones_ref, o_ref, rfin_ref, st_ref,
                      *, ch, rows, n_sub):
    c = pl.program_id(1)
    n_pair = W_C // LANES

    @pl.when(c == 0)
    def _():
        for j in range(n_pair):
            st_ref[j] = jnp.concatenate([r0_ref[0, 2 * j], r0_ref[0, 2 * j + 1]], axis=1)

    qi = lax.broadcasted_iota(jnp.int32, (rows, rows), 0)
    kj = lax.broadcasted_iota(jnp.int32, (rows, rows), 1)
    rel = (qi - kj).astype(F32)
    pos = lax.broadcasted_iota(jnp.int32, (rows, LANES), 0).astype(F32)
    low = lax.broadcasted_iota(jnp.int32, (rows, LANES), 1) < HEAD_DIM
    low_st = lax.broadcasted_iota(jnp.int32, (HEAD_DIM, LANES), 1) < HEAD_DIM
    halves = (jnp.where(low_st, 1.0, 0.0).astype(BF16), jnp.where(low_st, 0.0, 1.0).astype(BF16))
    log_gamma = [math.log(1.0 - 2.0 ** (-5.0 - hd)) for hd in range(H_C)]
    dmask = [jnp.where(rel >= 0, jnp.exp(lg * jnp.maximum(rel, 0.0)), 0.0) for lg in log_gamma]
    lg_lane = [jnp.where(low, log_gamma[2 * j], log_gamma[2 * j + 1]) for j in range(n_pair)]
    lg_st = [jnp.where(low_st, log_gamma[2 * j], log_gamma[2 * j + 1]) for j in range(n_pair)]
    ones = ones_ref[...]

    blocks = [(sub, j) for sub in range(n_sub) for j in range(n_pair)]
    tile = lambda ref, sub, j: ref[0, sub * rows:(sub + 1) * rows, j * LANES:(j + 1) * LANES]
    q = [tile(q_ref, sub, j) for sub, j in blocks]
    k = [tile(k_ref, sub, j) for sub, j in blocks]
    v = [tile(v_ref, sub, j).astype(BF16) for sub, j in blocks]
    k_b = [x.astype(BF16) for x in k]
    att = [[_dot_nt(jnp.where(low if half == 0 else ~low, x, 0.0).astype(BF16), kb) * dmask[2 * j + half]
            for half in (0, 1)] for x, kb, (_, j) in zip(q, k_b, blocks)]
    o_intra = [jnp.where(low, _dot(a0.astype(BF16), vv), _dot(a1.astype(BF16), vv)) for (a0, a1), vv in zip(att, v)]
    q_dec = [(x * jnp.exp(lg_lane[j] * (pos + 1.0))).astype(BF16) for x, (_, j) in zip(q, blocks)]
    kv = [_pair_diag(_dot_tn((x * jnp.exp(lg_lane[j] * (ch - 1.0 - pos))).astype(BF16), vv), low_st)
          for x, vv, (_, j) in zip(k, v, blocks)]

    state = [st_ref[j] for j in range(n_pair)]
    outs = []
    for i, (sub, j) in enumerate(blocks):
        st_b = state[j].astype(BF16)
        st_bd = jnp.concatenate([st_b * halves[0], st_b * halves[1]], axis=0)
        outs.append(o_intra[i] + _dot(q_dec[i], st_bd))
        state[j] = state[j] * jnp.exp(lg_st[j] * ch) + kv[i]
    for j in range(n_pair):
        st_ref[j] = state[j]

    for o, (sub, j) in zip(outs, blocks):
        sl = slice(j * LANES, (j + 1) * LANES)
        oc = o - _head_sum(o, ones) * (1.0 / HEAD_DIM)
        var = _head_sum(oc * oc, ones) * (1.0 / HEAD_DIM)
        gate = tile(g_ref, sub, j)
        o_ref[0, sub * rows:(sub + 1) * rows, sl] = \
            (oc * lax.rsqrt(var + LN_EPS) * gng_ref[:, sl] + gnb_ref[:, sl]) * (gate * _sigmoid(gate))

    @pl.when(c == pl.num_programs(1) - 1)
    def _():
        for j in range(n_pair):
            rfin_ref[0, 2 * j] = st_ref[j][:, :HEAD_DIM]
            rfin_ref[0, 2 * j + 1] = st_ref[j][:, HEAD_DIM:]


def _retention(hc, r0_all, st_idx, p, layer, rows, ch, n_sub):
    b, tp, _ = hc.shape
    nc = tp // (rows * n_sub)
    part = lambda j: pl.BlockSpec((1, rows * n_sub, W_C), lambda i, c: (i, c, j))
    st_spec = pl.BlockSpec((1, H_C, HEAD_DIM, HEAD_DIM), lambda i, c: (i, 0, 0, 0))
    lane_head = jnp.arange(LANES) // HEAD_DIM
    ones = (lane_head[:, None] == lane_head[None, :]).astype(BF16)
    return pl.pallas_call(
        functools.partial(_retention_kernel, ch=ch, rows=rows, n_sub=n_sub),
        grid=(b, nc),
        in_specs=[part(0), part(1), part(2), part(3),
                  pl.BlockSpec((None, 1, H_C, HEAD_DIM, HEAD_DIM), lambda i, c: (st_idx, i, 0, 0, 0)),
                  _layer_spec(p['ret_gn_g'], layer), _layer_spec(p['ret_gn_b'], layer),
                  pl.BlockSpec((LANES, LANES), lambda i, c: (0, 0))],
        out_specs=[pl.BlockSpec((1, rows * n_sub, W_C), lambda i, c: (i, c, 0)), st_spec],
        out_shape=[jax.ShapeDtypeStruct((b, tp, W_C), F32),
                   jax.ShapeDtypeStruct((b, H_C, HEAD_DIM, HEAD_DIM), F32)],
        scratch_shapes=[pltpu.VMEM((W_C // LANES, HEAD_DIM, LANES), F32)],
        compiler_params=_params(("parallel", "arbitrary")),
        name="retention",
    )(hc, hc, hc, hc, r0_all, p['ret_gn_g'], p['ret_gn_b'], ones)


def _out_proj_kernel(oa_ref, ob_ref, oc_ref, x_ref, w_ref, g_ref, b_ref, y_ref, *, alpha):
    mix = _dot(oa_ref[...].astype(BF16), w_ref[0:W_A, :])
    for j in range(W_B // LANES):
        mix += _dot(ob_ref[j].astype(BF16), w_ref[W_A + j * LANES:W_A + (j + 1) * LANES, :])
    mix += _dot(oc_ref[...].astype(BF16), w_ref[W_A + W_B:, :])
    y_ref[...] = _layer_norm(alpha * x_ref[...] + mix, g_ref[...], b_ref[...], LN_EPS)


def _out_proj(oa, ob, oc, x2d, w_bf16, p, layer, alpha, bm):
    n, d = x2d.shape
    row = lambda w: pl.BlockSpec((bm, w), lambda i: (i, 0))
    return pl.pallas_call(
        functools.partial(_out_proj_kernel, alpha=alpha),
        grid=(n // bm,),
        in_specs=[row(W_A), pl.BlockSpec((W_B // LANES, bm, LANES), lambda i: (0, i, 0)), row(W_C), row(d),
                  _layer_spec(w_bf16, layer), _layer_spec(p['ln1_g'], layer), _layer_spec(p['ln1_b'], layer)],
        out_specs=row(d),
        out_shape=jax.ShapeDtypeStruct((n, d), F32),
        compiler_params=_params(("parallel",)),
        name="out_proj",
    )(oa, ob, oc, x2d, w_bf16, p['ln1_g'], p['ln1_b'])


FFN_CHUNK = 256


def _ffn_kernel(x_ref, wg_ref, wu_ref, wd_ref, g_ref, b_ref, y_ref, *, alpha):
    x = x_ref[...]
    xb = x.astype(BF16)
    d_ff = wg_ref.shape[1]
    acc = alpha * x
    for j in range(d_ff // FFN_CHUNK):
        cs = slice(j * FFN_CHUNK, (j + 1) * FFN_CHUNK)
        gate = _dot(xb, wg_ref[:, cs])
        up = _dot(xb, wu_ref[:, cs])
        act = gate * _sigmoid(gate) * up
        acc += _dot(act.astype(BF16), wd_ref[cs, :])
    y_ref[...] = _layer_norm(acc, g_ref[...], b_ref[...], LN_EPS)


def _ffn(x2d, wg, wu, wd, p, layer, alpha, bm):
    n, d = x2d.shape
    row = pl.BlockSpec((bm, d), lambda i: (i, 0))
    full = lambda w: _layer_spec(w, layer, pipeline_mode=pl.Buffered(1))
    return pl.pallas_call(
        functools.partial(_ffn_kernel, alpha=alpha),
        grid=(n // bm,),
        in_specs=[row, full(wg), full(wu), full(wd), _layer_spec(p['ln2_g'], layer), _layer_spec(p['ln2_b'], layer)],
        out_specs=row,
        out_shape=jax.ShapeDtypeStruct((n, d), F32),
        compiler_params=_params(("parallel",)),
        name="ffn",
    )(x2d, wg, wu, wd, p['ln2_g'], p['ln2_b'])


def _pad_rows(z, rows):
    return jnp.pad(z, ((0, 0), (0, rows - z.shape[1]), (0, 0)))


def _layer(x, state, pos0, p, wts, layer, alpha, prev_kv, caches=None):
    bsz, t, d = x.shape
    n = bsz * t
    depth = wts['w_in'].shape[0]
    x2d = x.reshape(n, d)
    bm = min(512, n)
    n_slab = W_B // LANES
    pos = pos0 + jnp.arange(t, dtype=F32)
    reps = max(bm // t, 1)
    tile = lambda tabs: tuple(jnp.tile(z, (reps, 1)) for z in tabs)
    inv_b = ROPE_THETA ** (-jnp.arange(0, ROT_DIM, 2, dtype=F32) / ROT_DIM)
    inv_c = 1.0 / (RET_THETA ** jnp.linspace(0.0, 1.0, HEAD_DIM // 2, dtype=F32))
    ha, hb_slabs, hc = _in_proj(x2d, wts['w_in'], layer, tile(_rot_tables(pos, inv_b, ROT_DIM)),
                                tile(_rot_tables(pos, inv_c, HEAD_DIM)), bm)
    ha = ha.reshape(bsz, t, A_COLS)
    hc = hc.reshape(bsz, t, C_COLS)
    shift_new = ha[:, -1]
    shift_all, wkv_all, ret_all, st_idx = state

    tp = -(-t // RWKV_CHUNK) * RWKV_CHUNK
    if tp % (RWKV_SUBCHUNKS * RWKV_CHUNK) == 0:
        n_sub, independent = RWKV_SUBCHUNKS, False
    elif tp == RWKV_CHUNK and bsz % RWKV_SEQS == 0:
        n_sub, independent = RWKV_SEQS, True
    else:
        n_sub, independent = 1, False
    o_a, wkv_new = _rwkv(_pad_rows(ha, tp), shift_all, wkv_all, st_idx, p, layer, t, n_sub, independent)
    o_a = o_a[:, :t].reshape(n, W_A)

    if caches is None:
        o_b = _attn_prompt(hb_slabs, bsz, t)
        k_keep, v_keep = _kv_tail(hb_slabs, bsz, t, min(WIN_MAX, t), layer, depth, prev_kv)
    else:
        rows = -(-t // 8) * 8
        hb = jnp.swapaxes(hb_slabs, 0, 1).reshape(bsz, t, B_COLS)
        o_b, k_keep, v_keep = _attn_sample(_pad_rows(hb, rows), caches[0], caches[1], layer, t, prev_kv)
        o_b = jnp.swapaxes(o_b[:, :t].reshape(n, n_slab, LANES), 0, 1)

    if t % RET_CHUNK == 0:
        o_c, ret_new = _retention(hc, ret_all, st_idx, p, layer, RET_CHUNK, RET_CHUNK,
                                  RET_SUBCHUNKS if t % (RET_SUBCHUNKS * RET_CHUNK) == 0 else 1)
    else:
        rows = -(-t // 8) * 8
        o_c, ret_new = _retention(_pad_rows(hc, rows), ret_all, st_idx, p, layer, rows, t, 1)
        o_c = o_c[:, :t]
    o_c = o_c.reshape(n, W_C)

    x1 = _out_proj(o_a, o_b, o_c, x2d, wts['w_out'], p, layer, alpha, bm)
    x2 = _ffn(x1, wts['w_ffn_gate'], wts['w_ffn_up'], wts['w_ffn_down'], p, layer, alpha, bm)
    return x2.reshape(bsz, t, d), (shift_new, wkv_new, k_keep, v_keep, ret_new)


def _token_major(z):
    z = z.reshape(z.shape[:-2] + (H_B, HEAD_DIM, z.shape[-1]))
    return jnp.moveaxis(z, -1, -3)


def kernel(x_prompt, x_sample, state_rwkv_shift, state_rwkv_wkv, cache_win_k, cache_win_v, state_ret, w_in, rwkv_mu, rwkv_w0, rwkv_w_lora, rwkv_a0, rwkv_a_lora, rwkv_g_lora, rwkv_k_k, rwkv_k_a, rwkv_r_k, rwkv_gn_g, rwkv_gn_b, ret_gn_g, ret_gn_b, w_out, ln1_g, ln1_b, w_ffn_gate, w_ffn_up, w_ffn_down, ln2_g, ln2_b):
    depth = w_in.shape[0]
    alpha = (2 * depth) ** 0.25
    vectors = dict(rwkv_mu=rwkv_mu, rwkv_w0=rwkv_w0, rwkv_a0=rwkv_a0, rwkv_k_k=rwkv_k_k, rwkv_k_a=rwkv_k_a,
                   rwkv_r_k=rwkv_r_k, rwkv_gn_g=rwkv_gn_g, rwkv_gn_b=rwkv_gn_b, ret_gn_g=ret_gn_g,
                   ret_gn_b=ret_gn_b, ln1_g=ln1_g, ln1_b=ln1_b, ln2_g=ln2_g, ln2_b=ln2_b)
    p = {k: v.reshape(depth, 1, -1) for k, v in vectors.items()}
    p.update(rwkv_w_lora=rwkv_w_lora.astype(BF16), rwkv_a_lora=rwkv_a_lora.astype(BF16),
             rwkv_g_lora=rwkv_g_lora.astype(BF16))
    wts = {'w_in': w_in.astype(BF16), 'w_out': w_out.astype(BF16), 'w_ffn_gate': w_ffn_gate.astype(BF16),
           'w_ffn_up': w_ffn_up.astype(BF16), 'w_ffn_down': w_ffn_down.astype(BF16)}
    bp = x_prompt.shape[0]
    empty = (jnp.zeros((1, bp, A_COLS), F32), jnp.zeros((1, bp, H_A, HEAD_DIM, HEAD_DIM), F32),
             jnp.zeros((1, bp, H_C, HEAD_DIM, HEAD_DIM), F32), 0)
    to_minor = lambda c: jnp.moveaxis(c, 2, -1).reshape(c.shape[:2] + (W_B, c.shape[2]))
    caches = (to_minor(cache_win_k), to_minor(cache_win_v))
    xp, xs = x_prompt, x_sample
    p_states, s_states = [], []
    p_kv = s_kv = None
    for l in range(depth):
        xp, sp = _layer(xp, empty, 0.0, p, wts, l, alpha, p_kv)
        xs, ss = _layer(xs, (state_rwkv_shift, state_rwkv_wkv, state_ret, l), float(PAST_LEN), p, wts, l, alpha,
                        s_kv, caches=caches)
        p_kv, s_kv = (sp[2], sp[3]), (ss[2], ss[3])
        p_states.append(sp)
        s_states.append(ss)
    stack = lambda states, j: jnp.stack([s[j] for s in states])
    outs = []
    for states, kv in ((p_states, p_kv), (s_states, s_kv)):
        outs += [stack(states, 0), stack(states, 1), _token_major(kv[0]), _token_major(kv[1]), stack(states, 4)]
    return (xp, xs) + tuple(outs)
```

```python
import functools
import math

import jax
import jax.numpy as jnp
from jax import lax
from jax.experimental import pallas as pl
from jax.experimental.pallas import tpu as pltpu

F32 = jnp.float32
BF16 = jnp.bfloat16
HI = lax.Precision.HIGHEST

PAST_LEN = 8192
HEAD_DIM = 64
H_A, H_B, H_C = 6, 6, 4
W_A, W_B, W_C = H_A * HEAD_DIM, H_B * HEAD_DIM, H_C * HEAD_DIM
LORA_W, LORA_A, LORA_G = 64, 64, 128
A_COLS = 3 * W_A + LORA_W + LORA_A + LORA_G
B_COLS = 3 * W_B
C_COLS = 4 * W_C
DILATED = ((128, 1), (512, 4), (2048, 16))
WIN_MAX = max(w for w, _ in DILATED)
ATT_BLK = 128
ROT_DIM = HEAD_DIM // 4
ROPE_THETA = 500000.0
RET_THETA = 10000.0
RET_CHUNK = 128
RET_SUBCHUNKS = 4
RWKV_CHUNK = 64
RWKV_SUBCHUNKS = 4
RWKV_SEQS = 8
RWKV_DECAY_SCALE = math.exp(-0.5)
RWKV_GN_EPS = 64e-5
LN_EPS = 1e-5
NEG = -1e30
LANES = 128
VMEM_LIMIT = 56 * 1024 * 1024


def _params(sem, vmem=VMEM_LIMIT):
    return pltpu.CompilerParams(dimension_semantics=sem, vmem_limit_bytes=vmem)


def _layer_spec(arr, layer, **kw):
    zeros = (0,) * (arr.ndim - 1)
    return pl.BlockSpec((None,) + arr.shape[1:], lambda *_: (layer,) + zeros, **kw)


def _dot(a, b, precision=None):
    return jnp.dot(a, b, preferred_element_type=F32, precision=precision)


def _dot_nt(a, b, precision=None):
    return lax.dot_general(a, b, (((1,), (1,)), ((), ())), preferred_element_type=F32, precision=precision)


def _dot_tn(a, b, precision=None):
    return lax.dot_general(a, b, (((0,), (0,)), ((), ())), preferred_element_type=F32, precision=precision)


def _sigmoid(x):
    return 1.0 / (1.0 + jnp.exp(-x))


def _layer_norm(x, g, b, eps):
    mu = jnp.mean(x, axis=-1, keepdims=True)
    xc = x - mu
    var = jnp.mean(xc * xc, axis=-1, keepdims=True)
    return xc * lax.rsqrt(var + eps) * g + b


def _rot_tables(pos, inv_freq, rot_width):
    half = rot_width // 2
    ang = pos[:, None] * inv_freq[None, :]
    cos, sin = jnp.cos(ang), jnp.sin(ang)
    lane = jnp.arange(LANES) % HEAD_DIM
    idx = lane % half
    cos_l = jnp.where(lane[None, :] < rot_width, cos[:, idx], 1.0)
    sin_l = sin[:, idx]
    sin_up = jnp.where(lane[None, :] < half, -sin_l, 0.0)
    sin_dn = jnp.where((lane[None, :] >= half) & (lane[None, :] < rot_width), sin_l, 0.0)
    return cos_l.astype(F32), sin_up.astype(F32), sin_dn.astype(F32)


def _rotate_slab(x, cos, sin_up, sin_dn, half):
    up = pltpu.roll(x, LANES - half, 1)
    dn = pltpu.roll(x, half, 1)
    return x * cos + up * sin_up + dn * sin_dn


def _in_proj_kernel(x_ref, w_ref, cb_ref, ub_ref, db_ref, cc_ref, uc_ref, dc_ref,
                    ha_ref, hb_ref, hc_ref):
    h = _dot(x_ref[...].astype(BF16), w_ref[...])
    ha_ref[...] = h[:, :A_COLS]
    cb, ub, db = cb_ref[...], ub_ref[...], db_ref[...]
    for j in range(B_COLS // LANES):
        slab = h[:, A_COLS + j * LANES:A_COLS + (j + 1) * LANES]
        if j < 2 * W_B // LANES:
            slab = _rotate_slab(slab, cb, ub, db, ROT_DIM // 2)
        hb_ref[j] = slab
    cc, uc, dc = cc_ref[...], uc_ref[...], dc_ref[...]
    c0 = A_COLS + B_COLS
    for j in range(C_COLS // LANES):
        slab = h[:, c0 + j * LANES:c0 + (j + 1) * LANES]
        if j < 2 * W_C // LANES:
            slab = _rotate_slab(slab, cc, uc, dc, HEAD_DIM // 2)
        if W_C // LANES <= j < 2 * W_C // LANES:
            slab = slab * (HEAD_DIM ** -0.5)
        hc_ref[:, j * LANES:(j + 1) * LANES] = slab


def _in_proj(x2d, w_bf16, layer, tabs_b, tabs_c, bm):
    n, d = x2d.shape
    t_rows = tabs_b[0].shape[0]
    nt = t_rows // bm
    tab_spec = pl.BlockSpec((bm, LANES), lambda i: (i % nt, 0))
    row = lambda w: pl.BlockSpec((bm, w), lambda i: (i, 0))
    return pl.pallas_call(
        _in_proj_kernel,
        grid=(n // bm,),
        in_specs=[row(d), _layer_spec(w_bf16, layer)] + [tab_spec] * 6,
        out_specs=[row(A_COLS), pl.BlockSpec((B_COLS // LANES, bm, LANES), lambda i: (0, i, 0)), row(C_COLS)],
        out_shape=[jax.ShapeDtypeStruct((n, A_COLS), F32),
                   jax.ShapeDtypeStruct((B_COLS // LANES, n, LANES), F32),
                   jax.ShapeDtypeStruct((n, C_COLS), F32)],
        compiler_params=_params(("parallel",)),
        name="in_proj",
    )(x2d, w_bf16, *tabs_b, *tabs_c)


assert RWKV_CHUNK == HEAD_DIM


def _split(x):
    hi = x.astype(BF16)
    return hi, (x - hi.astype(F32)).astype(BF16)


def _mm3(dot, a, b):
    a_axis = 0 if dot is _dot_tn else 1
    b_axis = 1 if dot is _dot_nt else 0
    return dot(jnp.concatenate([a[0], a[1]], axis=a_axis), jnp.concatenate([b[0], b[0]], axis=b_axis)) + \
        dot(a[0], b[1])


def _block_diag(pair, lane_masks):
    first, second = lane_masks
    return tuple(jnp.concatenate([p * first, p * second], axis=0) for p in pair)


def _pair_from_state(s_a, s_b):
    return jnp.concatenate([s_a.T, s_b.T], axis=1)


def _store_pair_state(ref, idx, j, pair):
    ref[idx, 2 * j] = pair[:, :HEAD_DIM].T
    ref[idx, 2 * j + 1] = pair[:, HEAD_DIM:].T


def _pair_diag(res, low):
    half = res.shape[0] // 2
    return jnp.where(low, res[:half], res[half:])


def _head_sum(z, ones):
    hi, lo = _split(z)
    return _dot(hi, ones) + _dot(lo, ones)


def _rwkv_kernel(ha_ref, shift_ref, s0_ref, mu_ref, w0_ref, wl_ref, a0_ref, al_ref, gl_ref,
                 kk_ref, ka_ref, rk_ref, gng_ref, gnb_ref, ones_ref, o_ref, sfin_ref,
                 hp_ref, st_ref, oraw_ref, *, t_valid, n_sub, independent):
    c = pl.program_id(1)
    ch = RWKV_CHUNK
    rows = n_sub * ch
    seg_row = lax.broadcasted_iota(jnp.int32, (rows, 1), 0) & (ch - 1)

    @pl.when(c == 0)
    def _():
        if independent:
            hp_ref[0:8, :] = jnp.zeros((8, A_COLS), F32)
        else:
            hp_ref[0:8, :] = jnp.broadcast_to(shift_ref[0], (8, A_COLS))
            for j in range(W_A // LANES):
                st_ref[j] = _pair_from_state(s0_ref[0, 2 * j], s0_ref[0, 2 * j + 1])

    h = ha_ref[0]
    hp_ref[8:8 + rows, :] = h
    prev = hp_ref[7:7 + rows, :]
    if independent:
        own_shift = jnp.concatenate([jnp.broadcast_to(shift_ref[0, s:s + 1, :], (ch, A_COLS))
                                     for s in range(n_sub)], axis=0)
        prev = jnp.where(seg_row == 0, own_shift, prev)
    else:
        hp_ref[0:8, :] = h[rows - 8:rows, :]
    xs = h + (prev - h) * mu_ref[...]

    r = xs[:, :W_A]
    k = xs[:, W_A:2 * W_A]
    v = xs[:, 2 * W_A:3 * W_A]
    o0 = 3 * W_A
    xw = xs[:, o0:o0 + LORA_W]
    xa = xs[:, o0 + LORA_W:o0 + LORA_W + LORA_A]
    xg = xs[:, o0 + LORA_W + LORA_A:]
    log_w = -RWKV_DECAY_SCALE * _sigmoid(w0_ref[...] + _dot(jnp.tanh(xw).astype(BF16), wl_ref[...]))
    a = _sigmoid(a0_ref[...] + _dot(xa.astype(BF16), al_ref[...]))
    g = _dot(_sigmoid(xg).astype(BF16), gl_ref[...])
    kk = k * kk_ref[...]
    kmod = k * (1.0 + (a - 1.0) * ka_ref[...])

    if t_valid is not None:
        tok = seg_row if independent else c * rows + lax.broadcasted_iota(jnp.int32, (rows, 1), 0)
        valid = tok < t_valid
        log_w = jnp.where(valid, log_w, 0.0)
        kk = jnp.where(valid, kk, 0.0)
        kmod = jnp.where(valid, kmod, 0.0)
    ones = ones_ref[...]
    kk = kk * lax.rsqrt(jnp.maximum(_head_sum(kk * kk, ones), 1e-12))
    b = kk * a

    ri = lax.broadcasted_iota(jnp.int32, (rows, rows), 0)
    ci = lax.broadcasted_iota(jnp.int32, (rows, rows), 1)
    shift = int(math.log2(ch))
    same = (ri >> shift) == (ci >> shift)
    lower = (same & (ci <= ri)).astype(BF16)
    whole = same.astype(BF16)
    lw_hi = log_w.astype(BF16)
    rem = log_w - lw_hi.astype(F32)
    lw_mid = rem.astype(BF16)
    lw_lo = (rem - lw_mid.astype(F32)).astype(BF16)
    cum = _dot(lower, lw_hi) + _dot(lower, lw_mid) + _dot(lower, lw_lo)
    cum_end = _dot(whole, lw_hi) + _dot(whole, lw_mid) + _dot(whole, lw_lo)
    p_in = jnp.exp(cum)
    p_inv = jnp.exp(-cum)
    to_end = jnp.exp(cum_end - cum)
    p_end = jnp.exp(cum_end)
    kkt = kk * jnp.exp(cum - log_w)
    bt = b * p_inv
    kt = kmod * p_inv
    rt = r * p_in
    b_end = b * to_end
    k_end = kmod * to_end

    row = lax.broadcasted_iota(jnp.int32, (ch, LANES), 0)
    lane = lax.broadcasted_iota(jnp.int32, (ch, LANES), 1)
    col = lane & (ch - 1)
    low = lane < HEAD_DIM
    strict = col < row
    incl = col <= row
    diag = col == row
    off = [((row >> (lvl + 1)) == (col >> (lvl + 1))) & (((row >> lvl) & 1) == 1) & (((col >> lvl) & 1) == 0)
           for lvl in range(shift)]
    halves = (jnp.where(low, 1.0, 0.0).astype(BF16), jnp.where(low, 0.0, 1.0).astype(BF16))
    blk = lambda pair: _block_diag(_split(pair), halves)
    side = lambda u, w: tuple(jnp.concatenate([p, q], axis=1) for p, q in zip(u, w))

    n_pair = W_A // LANES
    chains = [(slice(sub * ch, (sub + 1) * ch), slice(j * LANES, (j + 1) * LANES))
              for sub in range(n_sub) for j in range(n_pair)]
    each = lambda fn, *lists: [fn(*args) for args in zip(*lists)]
    cut = lambda z: [z[rs, sl] for rs, sl in chains]
    kkt_c, rt_c, v_c = cut(kkt), cut(rt), cut(v)
    v_bd = each(blk, v_c)

    def scores(x_kk, x_r, x_b, x_k):
        rhs = tuple(jnp.concatenate([pb * halves[0], pb * halves[1], pk * halves[0], pk * halves[1]], axis=0)
                    for pb, pk in zip(_split(x_b), _split(x_k)))
        return _mm3(_dot_nt, _split(jnp.concatenate([x_kk, x_r], axis=0)), rhs)

    aa = each(scores, kkt_c, rt_c, cut(bt), cut(kt))
    a_kb = each(lambda m: m[:ch, :LANES], aa)
    a_kb_s = each(_split, a_kb)
    a_kk = each(lambda m: _split(jnp.where(strict, m[:ch, LANES:], 0.0)), aa)
    a_rb = each(lambda m: _split(jnp.where(incl, m[ch:, :LANES], 0.0)), aa)
    a_rk = each(lambda m: _split(jnp.where(incl, m[ch:, LANES:], 0.0)), aa)
    t_inv = each(lambda m: jnp.where(diag, 1.0, 0.0) - jnp.where(off[0], m, 0.0), a_kb)
    for lvl in range(1, shift):
        off_b = jnp.where(off[lvl], 1.0, 0.0).astype(BF16)
        inner = each(lambda m, t: blk(_mm3(_dot, (m[0] * off_b, m[1] * off_b), blk(t))), a_kb_s, t_inv)
        t_inv = each(lambda t, inn: t - _mm3(_dot, _split(t), inn), t_inv, inner)
    rows2 = lambda u, w: tuple(jnp.concatenate([a, b], axis=0) for a, b in zip(u, w))
    av = each(lambda m, n, vb: _mm3(_dot, rows2(m, n), vb), a_kk, a_rk, v_bd)
    y = each(lambda t, x, w: _mm3(_dot, _split(t), side(blk(x), blk(w[:ch]))), t_inv, kkt_c, av)
    z = each(lambda m, yy: _mm3(_dot, m, side(blk(yy[:, :LANES]), blk(yy[:, LANES:]))), a_rb, y)
    gz = each(lambda m, yy: _mm3(_dot_tn, _split(m), _split(yy)), cut(b_end), y)
    q_eff = each(lambda x, zz: _split(x - zz[:, :LANES]), rt_c, z)
    o_loc = each(lambda w, zz: w[ch:] - zz[:, LANES:], av, z)
    g_mat = each(lambda pe, gg: _split(jnp.where(diag, jnp.broadcast_to(pe[0:1, :], (ch, LANES)), 0.0)
                                       - _pair_diag(gg[:, :LANES], low)), cut(p_end), gz)
    qg = each(rows2, q_eff, g_mat)
    h_mat = each(lambda m, x, gg: _pair_diag(_mm3(_dot_tn, _split(m), _split(x)), low)
                 - _pair_diag(gg[:, LANES:], low), cut(k_end), v_c, gz)

    if independent:
        state = [_pair_from_state(s0_ref[sub, 2 * j], s0_ref[sub, 2 * j + 1])
                 for sub in range(n_sub) for j in range(n_pair)]
    else:
        state = [st_ref[j] for j in range(n_pair)]
    for sub in range(n_sub):
        ids = range(sub * n_pair, (sub + 1) * n_pair)
        st_bd = each(blk, [state[i] for i in ids] if independent else state)
        both = [_mm3(_dot, qg[i], s) for i, s in zip(ids, st_bd)]
        for j, i in enumerate(ids):
            oraw_ref[sub * ch:(sub + 1) * ch, j * LANES:(j + 1) * LANES] = both[j][:ch] + o_loc[i]
        new = [w[ch:] + h_mat[i] for w, i in zip(both, ids)]
        if independent:
            for j in range(n_pair):
                _store_pair_state(sfin_ref, sub, j, new[j])
        else:
            state = new
    if not independent:
        for j in range(n_pair):
            st_ref[j] = state[j]

    o_raw = oraw_ref[...]
    oc = o_raw - _head_sum(o_raw, ones) * (1.0 / HEAD_DIM)
    var = _head_sum(oc * oc, ones) * (1.0 / HEAD_DIM)
    o_n = oc * lax.rsqrt(var + RWKV_GN_EPS) * gng_ref[...] + gnb_ref[...]
    bonus = _head_sum(r * kmod * rk_ref[...], ones)
    o_ref[0] = (o_n + bonus * v) * g

    if not independent:
        @pl.when(c == pl.num_programs(1) - 1)
        def _():
            for j in range(n_pair):
                _store_pair_state(sfin_ref, 0, j, st_ref[j])


def _rwkv(ha, shift_all, s0_all, st_idx, p, layer, t_valid, n_sub, independent):
    b, tp, _ = ha.shape
    rows = n_sub * RWKV_CHUNK
    if independent:
        assert tp == RWKV_CHUNK and b % n_sub == 0
        groups, nc, per = b // n_sub, 1, n_sub
        ha = ha.reshape(groups, rows, A_COLS)
    else:
        groups, nc, per = b, tp // rows, 1
    shift_all = shift_all.reshape(shift_all.shape[0], groups, per, A_COLS)
    st_spec = pl.BlockSpec((per, H_A, HEAD_DIM, HEAD_DIM), lambda i, c: (i, 0, 0, 0))
    lane_head = jnp.arange(W_A) // HEAD_DIM
    ones = (lane_head[:, None] == lane_head[None, :]).astype(BF16)
    names = ('rwkv_mu', 'rwkv_w0', 'rwkv_w_lora', 'rwkv_a0', 'rwkv_a_lora', 'rwkv_g_lora', 'rwkv_k_k',
             'rwkv_k_a', 'rwkv_r_k', 'rwkv_gn_g', 'rwkv_gn_b')
    kern = functools.partial(_rwkv_kernel, t_valid=None if t_valid == tp else t_valid, n_sub=n_sub,
                             independent=independent)
    o, s_new = pl.pallas_call(
        kern,
        grid=(groups, nc),
        in_specs=[pl.BlockSpec((1, rows, A_COLS), lambda i, c: (i, c, 0)),
                  pl.BlockSpec((None, 1, per, A_COLS), lambda i, c: (st_idx, i, 0, 0)),
                  pl.BlockSpec((None, per, H_A, HEAD_DIM, HEAD_DIM), lambda i, c: (st_idx, i, 0, 0, 0))] +
                 [_layer_spec(p[k], layer) for k in names] +
                 [pl.BlockSpec((W_A, W_A), lambda i, c: (0, 0))],
        out_specs=[pl.BlockSpec((1, rows, W_A), lambda i, c: (i, c, 0)), st_spec],
        out_shape=[jax.ShapeDtypeStruct((groups, nc * rows, W_A), F32),
                   jax.ShapeDtypeStruct((b, H_A, HEAD_DIM, HEAD_DIM), F32)],
        scratch_shapes=[pltpu.VMEM((rows + 8, A_COLS), F32),
                        pltpu.VMEM((W_A // LANES, HEAD_DIM, LANES), F32),
                        pltpu.VMEM((rows, W_A), F32)],
        compiler_params=_params(("parallel", "arbitrary")),
        name="rwkv7",
    )(ha, shift_all, s0_all, *[p[k] for k in names], ones)
    return o.reshape(b, tp, W_A), s_new


ATT_SEG = WIN_MAX
ATT_UNITS = 2


def _attn_prompt_kernel(q_ref, k_ref, v_ref, o_ref, m_ref, l_ref, n_ref):
    seg = pl.program_id(1)
    blk = ATT_BLK
    n_slab = W_B // LANES
    heads = [(j, half) for j in range(n_slab) for half in (0, 1)]
    low = lax.broadcasted_iota(jnp.int32, (blk, LANES), 1) < HEAD_DIM
    qi = lax.broadcasted_iota(jnp.int32, (blk, blk), 0)
    ki = lax.broadcasted_iota(jnp.int32, (blk, blk), 1)
    bias_own = jnp.where(ki <= qi, 0.0, NEG)
    ok_prev = ki >= qi
    scale = HEAD_DIM ** -0.5

    def step(it, carry, *, win, dil, first, last):
        log_d = int(math.log2(dil))
        ds = (lambda s: pl.ds(s, blk, stride=dil)) if dil > 1 else (lambda s: pl.ds(s, blk))
        units = [it * ATT_UNITS + i for i in range(ATT_UNITS)]
        start = [(u >> log_d) * (blk * dil) + (u & (dil - 1)) for u in units]
        k_start = [seg * ATT_SEG + s for s in start]
        p_start = [jnp.maximum(s - win, 0) for s in k_start]
        bias = [jnp.concatenate([jnp.where(ok_prev, jnp.where(s >= win, 0.0, NEG), NEG), bias_own], axis=1)
                for s in k_start]
        tiles = [(i, j) for i in range(ATT_UNITS) for j in range(n_slab)]
        chains = [(t, i, j, half) for t, (i, j) in enumerate(tiles) for half in (0, 1)]
        both = lambda ref, i, j: jnp.concatenate([ref[j, ds(p_start[i]), :].astype(BF16),
                                                  ref[j, ds(k_start[i]), :].astype(BF16)], axis=0)
        q = [q_ref[j, ds(start[i]), :] * scale for i, j in tiles]
        k_cat = [both(k_ref, i, j) for i, j in tiles]
        v_cat = [both(v_ref, i, j) for i, j in tiles]
        qh = [jnp.where(low if half == 0 else ~low, q[t], 0.0).astype(BF16) for t, _, _, half in chains]
        s = [_dot_nt(x, k_cat[t]) + bias[i] for x, (t, i, _, _) in zip(qh, chains)]
        m = [jnp.max(x, axis=-1, keepdims=True) for x in s]
        p = [jnp.exp(x - mm) for x, mm in zip(s, m)]
        l = [jnp.sum(x, axis=-1, keepdims=True) for x in p]
        acc = [_dot(x.astype(BF16), v_cat[t]) for x, (t, _, _, _) in zip(p, chains)]
        for t, (i, j) in enumerate(tiles):
            rows = ds(start[i])
            m_t = jnp.where(low, m[2 * t], m[2 * t + 1])
            l_t = jnp.where(low, l[2 * t], l[2 * t + 1])
            n_t = jnp.where(low, acc[2 * t], acc[2 * t + 1])
            if not first:
                m_o = m_ref[j, rows, :]
                m_n = jnp.maximum(m_o, m_t)
                e_o, e_t = jnp.exp(m_o - m_n), jnp.exp(m_t - m_n)
                l_t = l_ref[j, rows, :] * e_o + l_t * e_t
                n_t = n_ref[j, rows, :] * e_o + n_t * e_t
                m_t = m_n
            if last:
                o_ref[j, rows, :] = n_t / l_t
            else:
                m_ref[j, rows, :] = m_t
                l_ref[j, rows, :] = l_t
                n_ref[j, rows, :] = n_t
        return carry

    order = sorted(DILATED, key=lambda wd: -wd[1])
    for idx, (win, dil) in enumerate(order):
        lax.fori_loop(0, ATT_SEG // (blk * ATT_UNITS),
                      functools.partial(step, win=win, dil=dil, first=idx == 0, last=idx == len(order) - 1), 0)


def _attn_prompt(hb_slabs, bsz, t):
    n_slab = W_B // LANES
    assert t % ATT_SEG == 0 and all(win // dil == ATT_BLK and ATT_SEG % win == 0 for win, dil in DILATED)
    nseg = t // ATT_SEG
    state = pltpu.VMEM((n_slab, ATT_SEG, LANES), F32)
    whole = lambda part: pl.BlockSpec((n_slab, t, LANES), lambda i, s: (part, i, 0))
    seg_spec = pl.BlockSpec((n_slab, ATT_SEG, LANES), lambda i, s: (0, i * nseg + s, 0))
    return pl.pallas_call(
        _attn_prompt_kernel,
        grid=(bsz, nseg),
        in_specs=[seg_spec, whole(1), whole(2)],
        out_specs=seg_spec,
        out_shape=jax.ShapeDtypeStruct((n_slab, bsz * t, LANES), F32),
        scratch_shapes=[state, state, state],
        compiler_params=_params(("parallel", "arbitrary")),
        name="attn_prompt",
    )(hb_slabs, hb_slabs, hb_slabs)


def _kv_tail_kernel(k_ref, v_ref, *rest):
    kt_ref, vt_ref = rest[-2:]
    for j in range(W_B // LANES):
        kt_ref[0, 0, j * LANES:(j + 1) * LANES, :] = k_ref[j].T
        vt_ref[0, 0, j * LANES:(j + 1) * LANES, :] = v_ref[j].T


def _kv_tail(hb_slabs, bsz, t, w_keep, layer, depth, prev):
    n_slab = W_B // LANES
    assert t % w_keep == 0
    per = t // w_keep
    tail = lambda part: pl.BlockSpec((n_slab, w_keep, LANES), lambda i: (part, i * per + per - 1, 0))
    out = pl.BlockSpec((1, 1, W_B, w_keep), lambda i: (layer, i, 0, 0))
    in_specs, args, aliases = [tail(1), tail(2)], [hb_slabs, hb_slabs], {}
    if prev is not None:
        in_specs += [pl.BlockSpec(memory_space=pl.ANY)] * 2
        args += list(prev)
        aliases = {2: 0, 3: 1}
    return pl.pallas_call(
        _kv_tail_kernel,
        grid=(bsz,),
        in_specs=in_specs,
        out_specs=[out, out],
        out_shape=[jax.ShapeDtypeStruct((depth, bsz, W_B, w_keep), F32)] * 2,
        input_output_aliases=aliases,
        compiler_params=_params(("parallel",)),
        name="kv_tail",
    )(*args)


def _attn_sample_kernel(hb_ref, ck_ref, cv_ref, *rest, t_new, rows):
    o_ref, nk_ref, nv_ref = rest[-3:]
    w_buf = ck_ref.shape[3]
    hb = hb_ref[0]
    q, k_new, v_new = hb[:, :W_B], hb[:, W_B:2 * W_B], hb[:, 2 * W_B:]
    k_t, v_t = ck_ref[0, 0], cv_ref[0, 0]
    scale = HEAD_DIM ** -0.5
    nrow = H_B * rows

    def iota3(shape, dim):
        return lax.broadcasted_iota(jnp.int32, (H_B, rows) + shape, dim).reshape((nrow,) + shape)

    head_of_lane = iota3((W_B,), 2) >> int(math.log2(HEAD_DIM))
    own_head = iota3((W_B,), 0) == head_of_lane
    q_bd = jnp.where(own_head, jnp.concatenate([q] * H_B, axis=0), 0.0).astype(BF16)
    s_c = _dot(q_bd, k_t.astype(BF16)) * scale
    s_n = _dot_nt(q_bd, k_new.astype(BF16)) * scale
    d_c = w_buf + iota3((w_buf,), 1) - iota3((w_buf,), 2)
    d_n = iota3((rows,), 1) - iota3((rows,), 2)
    new_ok = (d_n >= 0) & (iota3((rows,), 2) < t_new)
    ms, ls, pcs, pns = [], [], [], []
    for win, dil in DILATED:
        sc = jnp.where(((d_c & (dil - 1)) == 0) & (d_c <= win), s_c, NEG)
        sn = jnp.where(new_ok & ((d_n & (dil - 1)) == 0) & (d_n <= win), s_n, NEG)
        m = jnp.maximum(jnp.max(sc, axis=-1, keepdims=True), jnp.max(sn, axis=-1, keepdims=True))
        pc, pn = jnp.exp(sc - m), jnp.exp(sn - m)
        ms.append(m)
        ls.append(jnp.sum(pc, axis=-1, keepdims=True) + jnp.sum(pn, axis=-1, keepdims=True))
        pcs.append(pc.astype(BF16))
        pns.append(pn.astype(BF16))
    acc = _dot_nt(jnp.concatenate(pcs, axis=0), v_t.astype(BF16)) + \
        _dot(jnp.concatenate(pns, axis=0), v_new.astype(BF16))
    m_all = jnp.maximum(jnp.maximum(ms[0], ms[1]), ms[2])
    es = [jnp.exp(m - m_all) for m in ms]
    num = sum(e * acc[g * nrow:(g + 1) * nrow] for g, e in enumerate(es))
    den = sum(e * l for e, l in zip(es, ls))
    o_rows = jnp.where(own_head, num / den, 0.0)
    o_ref[0] = sum(o_rows[h * rows:(h + 1) * rows] for h in range(H_B))

    lane = lax.broadcasted_iota(jnp.int32, (rows, LANES), 1)
    tok = lax.broadcasted_iota(jnp.int32, (rows, LANES), 0)
    place = ((lane == tok + (LANES - t_new)) & (tok < t_new)).astype(BF16)

    def transposed_tail(x):
        hi = x.astype(BF16)
        rem = x - hi.astype(F32)
        mid = rem.astype(BF16)
        lo = (rem - mid.astype(F32)).astype(BF16)
        return _dot_tn(hi, place) + _dot_tn(mid, place) + _dot_tn(lo, place)

    tail_lane = lax.broadcasted_iota(jnp.int32, (W_B, LANES), 1) >= LANES - t_new
    for src, new, dst in ((k_t, k_new, nk_ref), (v_t, v_new, nv_ref)):
        rolled = pltpu.roll(src, w_buf - t_new, 1)
        dst[0, 0, :, 0:w_buf - LANES] = rolled[:, 0:w_buf - LANES]
        dst[0, 0, :, w_buf - LANES:] = jnp.where(tail_lane, transposed_tail(new), rolled[:, w_buf - LANES:])


def _attn_sample(hb, cache_kt, cache_vt, layer, t_new, prev):
    b, rows, _ = hb.shape
    depth, _, _, w_buf = cache_kt.shape
    cspec = pl.BlockSpec((1, 1, W_B, w_buf), lambda i: (layer, i, 0, 0))
    in_specs = [pl.BlockSpec((1, rows, B_COLS), lambda i: (i, 0, 0)), cspec, cspec]
    args = [hb, cache_kt, cache_vt]
    aliases = {}
    if prev is not None:
        in_specs += [pl.BlockSpec(memory_space=pl.ANY)] * 2
        args += list(prev)
        aliases = {3: 1, 4: 2}
    new_shape = jax.ShapeDtypeStruct((depth, b, W_B, w_buf), F32)
    return pl.pallas_call(
        functools.partial(_attn_sample_kernel, t_new=t_new, rows=rows),
        grid=(b,),
        in_specs=in_specs,
        out_specs=[pl.BlockSpec((1, rows, W_B), lambda i: (i, 0, 0)), cspec, cspec],
        out_shape=[jax.ShapeDtypeStruct((b, rows, W_B), F32), new_shape, new_shape],
        input_output_aliases=aliases,
        compiler_params=_params(("parallel",)),
        name="attn_sample",
    )(*args)


def _retention_kernel(q_ref, k_ref, v_ref, g_ref, r0_ref, gng_ref, gnb_ref, ones_ref, o_ref, rfin_ref, st_ref,
                      *, ch, rows, n_sub):
    c = pl.program_id(1)
    n_pair = W_C // LANES

    @pl.when(c == 0)
    def _():
        for j in range(n_pair):
            st_ref[j] = jnp.concatenate([r0_ref[0, 2 * j], r0_ref[0, 2 * j + 1]], axis=1)

    qi = lax.broadcasted_iota(jnp.int32, (rows, rows), 0)
    kj = lax.broadcasted_iota(jnp.int32, (rows, rows), 1)
    rel = (qi - kj).astype(F32)
    pos = lax.broadcasted_iota(jnp.int32, (rows, LANES), 0).astype(F32)
    low = lax.broadcasted_iota(jnp.int32, (rows, LANES), 1) < HEAD_DIM
    low_st = lax.broadcasted_iota(jnp.int32, (HEAD_DIM, LANES), 1) < HEAD_DIM
    halves = (jnp.where(low_st, 1.0, 0.0).astype(BF16), jnp.where(low_st, 0.0, 1.0).astype(BF16))
    log_gamma = [math.log(1.0 - 2.0 ** (-5.0 - hd)) for hd in range(H_C)]
    dmask = [jnp.where(rel >= 0, jnp.exp(lg * jnp.maximum(rel, 0.0)), 0.0) for lg in log_gamma]
    lg_lane = [jnp.where(low, log_gamma[2 * j], log_gamma[2 * j + 1]) for j in range(n_pair)]
    lg_st = [jnp.where(low_st, log_gamma[2 * j], log_gamma[2 * j + 1]) for j in range(n_pair)]
    ones = ones_ref[...]

    blocks = [(sub, j) for sub in range(n_sub) for j in range(n_pair)]
    tile = lambda ref, sub, j: ref[0, sub * rows:(sub + 1) * rows, j * LANES:(j + 1) * LANES]
    q = [tile(q_ref, sub, j) for sub, j in blocks]
    k = [tile(k_ref, sub, j) for sub, j in blocks]
    v = [tile(v_ref, sub, j).astype(BF16) for sub, j in blocks]
    k_b = [x.astype(BF16) for x in k]
    att = [[_dot_nt(jnp.where(low if half == 0 else ~low, x, 0.0).astype(BF16), kb) * dmask[2 * j + half]
            for half in (0, 1)] for x, kb, (_, j) in zip(q, k_b, blocks)]
    o_intra = [jnp.where(low, _dot(a0.astype(BF16), vv), _dot(a1.astype(BF16), vv)) for (a0, a1), vv in zip(att, v)]
    q_dec = [(x * jnp.exp(lg_lane[j] * (pos + 1.0))).astype(BF16) for x, (_, j) in zip(q, blocks)]
    kv = [_pair_diag(_dot_tn((x * jnp.exp(lg_lane[j] * (ch - 1.0 - pos))).astype(BF16), vv), low_st)
          for x, vv, (_, j) in zip(k, v, blocks)]

    state = [st_ref[j] for j in range(n_pair)]
    outs = []
    for i, (sub, j) in enumerate(blocks):
        st_b = state[j].astype(BF16)
        st_bd = jnp.concatenate([st_b * halves[0], st_b * halves[1]], axis=0)
        outs.append(o_intra[i] + _dot(q_dec[i], st_bd))
        state[j] = state[j] * jnp.exp(lg_st[j] * ch) + kv[i]
    for j in range(n_pair):
        st_ref[j] = state[j]

    for o, (sub, j) in zip(outs, blocks):
        sl = slice(j * LANES, (j + 1) * LANES)
        oc = o - _head_sum(o, ones) * (1.0 / HEAD_DIM)
        var = _head_sum(oc * oc, ones) * (1.0 / HEAD_DIM)
        gate = tile(g_ref, sub, j)
        o_ref[0, sub * rows:(sub + 1) * rows, sl] = \
            (oc * lax.rsqrt(var + LN_EPS) * gng_ref[:, sl] + gnb_ref[:, sl]) * (gate * _sigmoid(gate))

    @pl.when(c == pl.num_programs(1) - 1)
    def _():
        for j in range(n_pair):
            rfin_ref[0, 2 * j] = st_ref[j][:, :HEAD_DIM]
            rfin_ref[0, 2 * j + 1] = st_ref[j][:, HEAD_DIM:]


def _retention(hc, r0_all, st_idx, p, layer, rows, ch, n_sub):
    b, tp, _ = hc.shape
    nc = tp // (rows * n_sub)
    part = lambda j: pl.BlockSpec((1, rows * n_sub, W_C), lambda i, c: (i, c, j))
    st_spec = pl.BlockSpec((1, H_C, HEAD_DIM, HEAD_DIM), lambda i, c: (i, 0, 0, 0))
    lane_head = jnp.arange(LANES) // HEAD_DIM
    ones = (lane_head[:, None] == lane_head[None, :]).astype(BF16)
    return pl.pallas_call(
        functools.partial(_retention_kernel, ch=ch, rows=rows, n_sub=n_sub),
        grid=(b, nc),
        in_specs=[part(0), part(1), part(2), part(3),
                  pl.BlockSpec((None, 1, H_C, HEAD_DIM, HEAD_DIM), lambda i, c: (st_idx, i, 0, 0, 0)),
                  _layer_spec(p['ret_gn_g'], layer), _layer_spec(p['ret_gn_b'], layer),
                  pl.BlockSpec((LANES, LANES), lambda i, c: (0, 0))],
        out_specs=[pl.BlockSpec((1, rows * n_sub, W_C), lambda i, c: (i, c, 0)), st_spec],
        out_shape=[jax.ShapeDtypeStruct((b, tp, W_C), F32),
                   jax.ShapeDtypeStruct((b, H_C, HEAD_DIM, HEAD_DIM), F32)],
        scratch_shapes=[pltpu.VMEM((W_C // LANES, HEAD_DIM, LANES), F32)],
        compiler_params=_params(("parallel", "arbitrary")),
        name="retention",
    )(hc, hc, hc, hc, r0_all, p['ret_gn_g'], p['ret_gn_b'], ones)


def _out_proj_kernel(oa_ref, ob_ref, oc_ref, x_ref, w_ref, g_ref, b_ref, y_ref, *, alpha):
    mixed = jnp.concatenate([oa_ref[...].astype(BF16)] + [ob_ref[j].astype(BF16) for j in range(W_B // LANES)] +
                            [oc_ref[...].astype(BF16)], axis=1)
    y_ref[...] = _layer_norm(alpha * x_ref[...] + _dot(mixed, w_ref[...]), g_ref[...], b_ref[...], LN_EPS)


def _out_proj(oa, ob, oc, x2d, w_bf16, p, layer, alpha, bm):
    n, d = x2d.shape
    row = lambda w: pl.BlockSpec((bm, w), lambda i: (i, 0))
    return pl.pallas_call(
        functools.partial(_out_proj_kernel, alpha=alpha),
        grid=(n // bm,),
        in_specs=[row(W_A), pl.BlockSpec((W_B // LANES, bm, LANES), lambda i: (0, i, 0)), row(W_C), row(d),
                  _layer_spec(w_bf16, layer), _layer_spec(p['ln1_g'], layer), _layer_spec(p['ln1_b'], layer)],
        out_specs=row(d),
        out_shape=jax.ShapeDtypeStruct((n, d), F32),
        compiler_params=_params(("parallel",)),
        name="out_proj",
    )(oa, ob, oc, x2d, w_bf16, p['ln1_g'], p['ln1_b'])


FFN_CHUNK = 256


def _ffn_kernel(x_ref, wg_ref, wu_ref, wd_ref, g_ref, b_ref, y_ref, *, alpha):
    x = x_ref[...]
    xb = x.astype(BF16)
    d_ff = wg_ref.shape[1]
    acc = alpha * x
    for j in range(d_ff // FFN_CHUNK):
        cs = slice(j * FFN_CHUNK, (j + 1) * FFN_CHUNK)
        gate = _dot(xb, wg_ref[:, cs])
        up = _dot(xb, wu_ref[:, cs])
        act = gate * _sigmoid(gate) * up
        acc += _dot(act.astype(BF16), wd_ref[cs, :])
    y_ref[...] = _layer_norm(acc, g_ref[...], b_ref[...], LN_EPS)


def _ffn(x2d, wg, wu, wd, p, layer, alpha, bm):
    n, d = x2d.shape
    row = pl.BlockSpec((bm, d), lambda i: (i, 0))
    full = lambda w: _layer_spec(w, layer, pipeline_mode=pl.Buffered(1))
    return pl.pallas_call(
        functools.partial(_ffn_kernel, alpha=alpha),
        grid=(n // bm,),
        in_specs=[row, full(wg), full(wu), full(wd), _layer_spec(p['ln2_g'], layer), _layer_spec(p['ln2_b'], layer)],
        out_specs=row,
        out_shape=jax.ShapeDtypeStruct((n, d), F32),
        compiler_params=_params(("parallel",)),
        name="ffn",
    )(x2d, wg, wu, wd, p['ln2_g'], p['ln2_b'])


def _pad_rows(z, rows):
    return jnp.pad(z, ((0, 0), (0, rows - z.shape[1]), (0, 0)))


def _layer(x, state, pos0, p, wts, layer, alpha, prev_kv, caches=None):
    bsz, t, d = x.shape
    n = bsz * t
    depth = wts['w_in'].shape[0]
    x2d = x.reshape(n, d)
    bm = min(512, n)
    n_slab = W_B // LANES
    pos = pos0 + jnp.arange(t, dtype=F32)
    reps = max(bm // t, 1)
    tile = lambda tabs: tuple(jnp.tile(z, (reps, 1)) for z in tabs)
    inv_b = ROPE_THETA ** (-jnp.arange(0, ROT_DIM, 2, dtype=F32) / ROT_DIM)
    inv_c = 1.0 / (RET_THETA ** jnp.linspace(0.0, 1.0, HEAD_DIM // 2, dtype=F32))
    ha, hb_slabs, hc = _in_proj(x2d, wts['w_in'], layer, tile(_rot_tables(pos, inv_b, ROT_DIM)),
                                tile(_rot_tables(pos, inv_c, HEAD_DIM)), bm)
    ha = ha.reshape(bsz, t, A_COLS)
    hc = hc.reshape(bsz, t, C_COLS)
    shift_new = ha[:, -1]
    shift_all, wkv_all, ret_all, st_idx = state

    tp = -(-t // RWKV_CHUNK) * RWKV_CHUNK
    if tp % (RWKV_SUBCHUNKS * RWKV_CHUNK) == 0:
        n_sub, independent = RWKV_SUBCHUNKS, False
    elif tp == RWKV_CHUNK and bsz % RWKV_SEQS == 0:
        n_sub, independent = RWKV_SEQS, True
    else:
        n_sub, independent = 1, False
    o_a, wkv_new = _rwkv(_pad_rows(ha, tp), shift_all, wkv_all, st_idx, p, layer, t, n_sub, independent)
    o_a = o_a[:, :t].reshape(n, W_A)

    if caches is None:
        o_b = _attn_prompt(hb_slabs, bsz, t)
        k_keep, v_keep = _kv_tail(hb_slabs, bsz, t, min(WIN_MAX, t), layer, depth, prev_kv)
    else:
        rows = -(-t // 8) * 8
        hb = jnp.swapaxes(hb_slabs, 0, 1).reshape(bsz, t, B_COLS)
        o_b, k_keep, v_keep = _attn_sample(_pad_rows(hb, rows), caches[0], caches[1], layer, t, prev_kv)
        o_b = jnp.swapaxes(o_b[:, :t].reshape(n, n_slab, LANES), 0, 1)

    if t % RET_CHUNK == 0:
        o_c, ret_new = _retention(hc, ret_all, st_idx, p, layer, RET_CHUNK, RET_CHUNK,
                                  RET_SUBCHUNKS if t % (RET_SUBCHUNKS * RET_CHUNK) == 0 else 1)
    else:
        rows = -(-t // 8) * 8
        o_c, ret_new = _retention(_pad_rows(hc, rows), ret_all, st_idx, p, layer, rows, t, 1)
        o_c = o_c[:, :t]
    o_c = o_c.reshape(n, W_C)

    x1 = _out_proj(o_a, o_b, o_c, x2d, wts['w_out'], p, layer, alpha, bm)
    x2 = _ffn(x1, wts['w_ffn_gate'], wts['w_ffn_up'], wts['w_ffn_down'], p, layer, alpha, bm)
    return x2.reshape(bsz, t, d), (shift_new, wkv_new, k_keep, v_keep, ret_new)


def _token_major(z):
    z = z.reshape(z.shape[:-2] + (H_B, HEAD_DIM, z.shape[-1]))
    return jnp.moveaxis(z, -1, -3)


def kernel(x_prompt, x_sample, state_rwkv_shift, state_rwkv_wkv, cache_win_k, cache_win_v, state_ret, w_in, rwkv_mu, rwkv_w0, rwkv_w_lora, rwkv_a0, rwkv_a_lora, rwkv_g_lora, rwkv_k_k, rwkv_k_a, rwkv_r_k, rwkv_gn_g, rwkv_gn_b, ret_gn_g, ret_gn_b, w_out, ln1_g, ln1_b, w_ffn_gate, w_ffn_up, w_ffn_down, ln2_g, ln2_b):
    depth = w_in.shape[0]
    alpha = (2 * depth) ** 0.25
    vectors = dict(rwkv_mu=rwkv_mu, rwkv_w0=rwkv_w0, rwkv_a0=rwkv_a0, rwkv_k_k=rwkv_k_k, rwkv_k_a=rwkv_k_a,
                   rwkv_r_k=rwkv_r_k, rwkv_gn_g=rwkv_gn_g, rwkv_gn_b=rwkv_gn_b, ret_gn_g=ret_gn_g,
                   ret_gn_b=ret_gn_b, ln1_g=ln1_g, ln1_b=ln1_b, ln2_g=ln2_g, ln2_b=ln2_b)
    p = {k: v.reshape(depth, 1, -1) for k, v in vectors.items()}
    p.update(rwkv_w_lora=rwkv_w_lora.astype(BF16), rwkv_a_lora=rwkv_a_lora.astype(BF16),
             rwkv_g_lora=rwkv_g_lora.astype(BF16))
    wts = {'w_in': w_in.astype(BF16), 'w_out': w_out.astype(BF16), 'w_ffn_gate': w_ffn_gate.astype(BF16),
           'w_ffn_up': w_ffn_up.astype(BF16), 'w_ffn_down': w_ffn_down.astype(BF16)}
    bp = x_prompt.shape[0]
    empty = (jnp.zeros((1, bp, A_COLS), F32), jnp.zeros((1, bp, H_A, HEAD_DIM, HEAD_DIM), F32),
             jnp.zeros((1, bp, H_C, HEAD_DIM, HEAD_DIM), F32), 0)
    to_minor = lambda c: jnp.moveaxis(c, 2, -1).reshape(c.shape[:2] + (W_B, c.shape[2]))
    caches = (to_minor(cache_win_k), to_minor(cache_win_v))
    xp, xs = x_prompt, x_sample
    p_states, s_states = [], []
    p_kv = s_kv = None
    for l in range(depth):
        xp, sp = _layer(xp, empty, 0.0, p, wts, l, alpha, p_kv)
        xs, ss = _layer(xs, (state_rwkv_shift, state_rwkv_wkv, state_ret, l), float(PAST_LEN), p, wts, l, alpha,
                        s_kv, caches=caches)
        p_kv, s_kv = (sp[2], sp[3]), (ss[2], ss[3])
        p_states.append(sp)
        s_states.append(ss)
    stack = lambda states, j: jnp.stack([s[j] for s in states])
    outs = []
    for states, kv in ((p_states, p_kv), (s_states, s_kv)):
        outs += [stack(states, 0), stack(states, 1), _token_major(kv[0]), _token_major(kv[1]), stack(states, 4)]
    return (xp, xs) + tuple(outs)
```

```python
import functools
import math

import jax
import jax.numpy as jnp
from jax import lax
from jax.experimental import pallas as pl
from jax.experimental.pallas import tpu as pltpu

F32 = jnp.float32
BF16 = jnp.bfloat16
HI = lax.Precision.HIGHEST

PAST_LEN = 8192
HEAD_DIM = 64
H_A, H_B, H_C = 6, 6, 4
W_A, W_B, W_C = H_A * HEAD_DIM, H_B * HEAD_DIM, H_C * HEAD_DIM
LORA_W, LORA_A, LORA_G = 64, 64, 128
A_COLS = 3 * W_A + LORA_W + LORA_A + LORA_G
B_COLS = 3 * W_B
C_COLS = 4 * W_C
DILATED = ((128, 1), (512, 4), (2048, 16))
WIN_MAX = max(w for w, _ in DILATED)
ATT_BLK = 128
ROT_DIM = HEAD_DIM // 4
ROPE_THETA = 500000.0
RET_THETA = 10000.0
RET_CHUNK = 128
RET_SEQS = 8
RET_SUBCHUNKS = 4
RWKV_CHUNK = 64
RWKV_SUBCHUNKS = 4
RWKV_SEQS = 8
RWKV_DECAY_SCALE = math.exp(-0.5)
RWKV_GN_EPS = 64e-5
LN_EPS = 1e-5
NEG = -1e30
LANES = 128
VMEM_LIMIT = 56 * 1024 * 1024


def _params(sem, vmem=VMEM_LIMIT):
    return pltpu.CompilerParams(dimension_semantics=sem, vmem_limit_bytes=vmem)


def _layer_spec(arr, layer, **kw):
    zeros = (0,) * (arr.ndim - 1)
    return pl.BlockSpec((None,) + arr.shape[1:], lambda *_: (layer,) + zeros, **kw)


def _dot(a, b, precision=None):
    return jnp.dot(a, b, preferred_element_type=F32, precision=precision)


def _dot_nt(a, b, precision=None):
    return lax.dot_general(a, b, (((1,), (1,)), ((), ())), preferred_element_type=F32, precision=precision)


def _dot_tn(a, b, precision=None):
    return lax.dot_general(a, b, (((0,), (0,)), ((), ())), preferred_element_type=F32, precision=precision)


def _sigmoid(x):
    return 1.0 / (1.0 + jnp.exp(-x))


def _layer_norm(x, g, b, eps):
    mu = jnp.mean(x, axis=-1, keepdims=True)
    xc = x - mu
    var = jnp.mean(xc * xc, axis=-1, keepdims=True)
    return xc * lax.rsqrt(var + eps) * g + b


def _rot_tables(pos, inv_freq, rot_width):
    half = rot_width // 2
    ang = pos[:, None] * inv_freq[None, :]
    cos, sin = jnp.cos(ang), jnp.sin(ang)
    lane = jnp.arange(LANES) % HEAD_DIM
    idx = lane % half
    cos_l = jnp.where(lane[None, :] < rot_width, cos[:, idx], 1.0)
    sin_l = sin[:, idx]
    sin_up = jnp.where(lane[None, :] < half, -sin_l, 0.0)
    sin_dn = jnp.where((lane[None, :] >= half) & (lane[None, :] < rot_width), sin_l, 0.0)
    return cos_l.astype(F32), sin_up.astype(F32), sin_dn.astype(F32)


def _rotate_slab(x, cos, sin_up, sin_dn, half):
    up = pltpu.roll(x, LANES - half, 1)
    dn = pltpu.roll(x, half, 1)
    return x * cos + up * sin_up + dn * sin_dn


def _in_proj_kernel(x_ref, w_ref, cb_ref, ub_ref, db_ref, cc_ref, uc_ref, dc_ref,
                    ha_ref, hb_ref, hc_ref):
    h = _dot(x_ref[...].astype(BF16), w_ref[...])
    ha_ref[...] = h[:, :A_COLS]
    cb, ub, db = cb_ref[...], ub_ref[...], db_ref[...]
    for j in range(B_COLS // LANES):
        slab = h[:, A_COLS + j * LANES:A_COLS + (j + 1) * LANES]
        if j < 2 * W_B // LANES:
            slab = _rotate_slab(slab, cb, ub, db, ROT_DIM // 2)
        hb_ref[j] = slab
    cc, uc, dc = cc_ref[...], uc_ref[...], dc_ref[...]
    c0 = A_COLS + B_COLS
    for j in range(C_COLS // LANES):
        slab = h[:, c0 + j * LANES:c0 + (j + 1) * LANES]
        if j < 2 * W_C // LANES:
            slab = _rotate_slab(slab, cc, uc, dc, HEAD_DIM // 2)
        if W_C // LANES <= j < 2 * W_C // LANES:
            slab = slab * (HEAD_DIM ** -0.5)
        hc_ref[:, j * LANES:(j + 1) * LANES] = slab


def _in_proj(x2d, w_bf16, layer, tabs_b, tabs_c, bm):
    n, d = x2d.shape
    t_rows = tabs_b[0].shape[0]
    nt = t_rows // bm
    tab_spec = pl.BlockSpec((bm, LANES), lambda i: (i % nt, 0))
    row = lambda w: pl.BlockSpec((bm, w), lambda i: (i, 0))
    return pl.pallas_call(
        _in_proj_kernel,
        grid=(n // bm,),
        in_specs=[row(d), _layer_spec(w_bf16, layer)] + [tab_spec] * 6,
        out_specs=[row(A_COLS), pl.BlockSpec((B_COLS // LANES, bm, LANES), lambda i: (0, i, 0)), row(C_COLS)],
        out_shape=[jax.ShapeDtypeStruct((n, A_COLS), F32),
                   jax.ShapeDtypeStruct((B_COLS // LANES, n, LANES), F32),
                   jax.ShapeDtypeStruct((n, C_COLS), F32)],
        compiler_params=_params(("parallel",)),
        name="in_proj",
    )(x2d, w_bf16, *tabs_b, *tabs_c)


assert RWKV_CHUNK == HEAD_DIM


def _split(x):
    hi = x.astype(BF16)
    return hi, (x - hi.astype(F32)).astype(BF16)


def _mm3(dot, a, b):
    a_axis = 0 if dot is _dot_tn else 1
    b_axis = 1 if dot is _dot_nt else 0
    return dot(jnp.concatenate([a[0], a[1]], axis=a_axis), jnp.concatenate([b[0], b[0]], axis=b_axis)) + \
        dot(a[0], b[1])


def _block_diag(pair, lane_masks):
    first, second = lane_masks
    return tuple(jnp.concatenate([p * first, p * second], axis=0) for p in pair)


def _pair_from_state(s_a, s_b):
    return jnp.concatenate([s_a.T, s_b.T], axis=1)


def _store_pair_state(ref, idx, j, pair):
    ref[idx, 2 * j] = pair[:, :HEAD_DIM].T
    ref[idx, 2 * j + 1] = pair[:, HEAD_DIM:].T


def _pair_diag(res, low):
    half = res.shape[0] // 2
    return jnp.where(low, res[:half], res[half:])


def _head_sum(z, ones):
    hi, lo = _split(z)
    return _dot(hi, ones) + _dot(lo, ones)


def _rwkv_kernel(ha_ref, shift_ref, s0_ref, mu_ref, w0_ref, wl_ref, a0_ref, al_ref, gl_ref,
                 kk_ref, ka_ref, rk_ref, gng_ref, gnb_ref, ones_ref, o_ref, sfin_ref,
                 hp_ref, st_ref, oraw_ref, *, t_valid, n_sub, independent):
    c = pl.program_id(1)
    ch = RWKV_CHUNK
    rows = n_sub * ch
    seg_row = lax.broadcasted_iota(jnp.int32, (rows, 1), 0) & (ch - 1)

    @pl.when(c == 0)
    def _():
        if independent:
            hp_ref[0:8, :] = jnp.zeros((8, A_COLS), F32)
        else:
            hp_ref[0:8, :] = jnp.broadcast_to(shift_ref[0], (8, A_COLS))
            for j in range(W_A // LANES):
                st_ref[j] = _pair_from_state(s0_ref[0, 2 * j], s0_ref[0, 2 * j + 1])

    h = ha_ref[0]
    hp_ref[8:8 + rows, :] = h
    prev = hp_ref[7:7 + rows, :]
    if independent:
        own_shift = jnp.concatenate([jnp.broadcast_to(shift_ref[0, s:s + 1, :], (ch, A_COLS))
                                     for s in range(n_sub)], axis=0)
        prev = jnp.where(seg_row == 0, own_shift, prev)
    else:
        hp_ref[0:8, :] = h[rows - 8:rows, :]
    xs = h + (prev - h) * mu_ref[...]

    r = xs[:, :W_A]
    k = xs[:, W_A:2 * W_A]
    v = xs[:, 2 * W_A:3 * W_A]
    o0 = 3 * W_A
    xw = xs[:, o0:o0 + LORA_W]
    xa = xs[:, o0 + LORA_W:o0 + LORA_W + LORA_A]
    xg = xs[:, o0 + LORA_W + LORA_A:]
    log_w = -RWKV_DECAY_SCALE * _sigmoid(w0_ref[...] + _dot(jnp.tanh(xw).astype(BF16), wl_ref[...]))
    a = _sigmoid(a0_ref[...] + _dot(xa.astype(BF16), al_ref[...]))
    g = _dot(_sigmoid(xg).astype(BF16), gl_ref[...])
    kk = k * kk_ref[...]
    kmod = k * (1.0 + (a - 1.0) * ka_ref[...])

    if t_valid is not None:
        tok = seg_row if independent else c * rows + lax.broadcasted_iota(jnp.int32, (rows, 1), 0)
        valid = tok < t_valid
        log_w = jnp.where(valid, log_w, 0.0)
        kk = jnp.where(valid, kk, 0.0)
        kmod = jnp.where(valid, kmod, 0.0)
    ones = ones_ref[...]
    kk = kk * lax.rsqrt(jnp.maximum(_head_sum(kk * kk, ones), 1e-12))
    b = kk * a

    ri = lax.broadcasted_iota(jnp.int32, (rows, rows), 0)
    ci = lax.broadcasted_iota(jnp.int32, (rows, rows), 1)
    shift = int(math.log2(ch))
    same = (ri >> shift) == (ci >> shift)
    lower = (same & (ci <= ri)).astype(BF16)
    whole = same.astype(BF16)
    lw_hi = log_w.astype(BF16)
    rem = log_w - lw_hi.astype(F32)
    lw_mid = rem.astype(BF16)
    lw_lo = (rem - lw_mid.astype(F32)).astype(BF16)
    cum = _dot(lower, lw_hi) + _dot(lower, lw_mid) + _dot(lower, lw_lo)
    cum_end = _dot(whole, lw_hi) + _dot(whole, lw_mid) + _dot(whole, lw_lo)
    p_in = jnp.exp(cum)
    p_inv = jnp.exp(-cum)
    to_end = jnp.exp(cum_end - cum)
    p_end = jnp.exp(cum_end)
    kkt = kk * jnp.exp(cum - log_w)
    bt = b * p_inv
    kt = kmod * p_inv
    rt = r * p_in
    b_end = b * to_end
    k_end = kmod * to_end

    row = lax.broadcasted_iota(jnp.int32, (ch, LANES), 0)
    lane = lax.broadcasted_iota(jnp.int32, (ch, LANES), 1)
    col = lane & (ch - 1)
    low = lane < HEAD_DIM
    strict = col < row
    incl = col <= row
    diag = col == row
    off = [((row >> (lvl + 1)) == (col >> (lvl + 1))) & (((row >> lvl) & 1) == 1) & (((col >> lvl) & 1) == 0)
           for lvl in range(shift)]
    halves = (jnp.where(low, 1.0, 0.0).astype(BF16), jnp.where(low, 0.0, 1.0).astype(BF16))
    blk = lambda pair: _block_diag(_split(pair), halves)
    side = lambda u, w: tuple(jnp.concatenate([p, q], axis=1) for p, q in zip(u, w))

    n_pair = W_A // LANES
    chains = [(slice(sub * ch, (sub + 1) * ch), slice(j * LANES, (j + 1) * LANES))
              for sub in range(n_sub) for j in range(n_pair)]
    each = lambda fn, *lists: [fn(*args) for args in zip(*lists)]
    cut = lambda z: [z[rs, sl] for rs, sl in chains]
    kkt_c, rt_c, v_c = cut(kkt), cut(rt), cut(v)
    v_bd = each(blk, v_c)

    def scores(x_kk, x_r, x_b, x_k):
        rhs = tuple(jnp.concatenate([pb * halves[0], pb * halves[1], pk * halves[0], pk * halves[1]], axis=0)
                    for pb, pk in zip(_split(x_b), _split(x_k)))
        return _mm3(_dot_nt, _split(jnp.concatenate([x_kk, x_r], axis=0)), rhs)

    aa = each(scores, kkt_c, rt_c, cut(bt), cut(kt))
    a_kb = each(lambda m: m[:ch, :LANES], aa)
    a_kb_s = each(_split, a_kb)
    a_kk = each(lambda m: _split(jnp.where(strict, m[:ch, LANES:], 0.0)), aa)
    a_rb = each(lambda m: _split(jnp.where(incl, m[ch:, :LANES], 0.0)), aa)
    a_rk = each(lambda m: _split(jnp.where(incl, m[ch:, LANES:], 0.0)), aa)
    t_inv = each(lambda m: jnp.where(diag, 1.0, 0.0) - jnp.where(off[0], m, 0.0), a_kb)
    for lvl in range(1, shift):
        off_b = jnp.where(off[lvl], 1.0, 0.0).astype(BF16)
        inner = each(lambda m, t: blk(_mm3(_dot, (m[0] * off_b, m[1] * off_b), blk(t))), a_kb_s, t_inv)
        t_inv = each(lambda t, inn: t - _mm3(_dot, _split(t), inn), t_inv, inner)
    rows2 = lambda u, w: tuple(jnp.concatenate([a, b], axis=0) for a, b in zip(u, w))
    av = each(lambda m, n, vb: _mm3(_dot, rows2(m, n), vb), a_kk, a_rk, v_bd)
    y = each(lambda t, x, w: _mm3(_dot, _split(t), side(blk(x), blk(w[:ch]))), t_inv, kkt_c, av)
    z = each(lambda m, yy: _mm3(_dot, m, side(blk(yy[:, :LANES]), blk(yy[:, LANES:]))), a_rb, y)
    gz = each(lambda m, yy: _mm3(_dot_tn, _split(m), _split(yy)), cut(b_end), y)
    q_eff = each(lambda x, zz: _split(x - zz[:, :LANES]), rt_c, z)
    o_loc = each(lambda w, zz: w[ch:] - zz[:, LANES:], av, z)
    g_mat = each(lambda pe, gg: _split(jnp.where(diag, jnp.broadcast_to(pe[0:1, :], (ch, LANES)), 0.0)
                                       - _pair_diag(gg[:, :LANES], low)), cut(p_end), gz)
    qg = each(rows2, q_eff, g_mat)
    h_mat = each(lambda m, x, gg: _pair_diag(_mm3(_dot_tn, _split(m), _split(x)), low)
                 - _pair_diag(gg[:, LANES:], low), cut(k_end), v_c, gz)

    if independent:
        state = [_pair_from_state(s0_ref[sub, 2 * j], s0_ref[sub, 2 * j + 1])
                 for sub in range(n_sub) for j in range(n_pair)]
    else:
        state = [st_ref[j] for j in range(n_pair)]
    for sub in range(n_sub):
        ids = range(sub * n_pair, (sub + 1) * n_pair)
        st_bd = each(blk, [state[i] for i in ids] if independent else state)
        both = [_mm3(_dot, qg[i], s) for i, s in zip(ids, st_bd)]
        for j, i in enumerate(ids):
            oraw_ref[sub * ch:(sub + 1) * ch, j * LANES:(j + 1) * LANES] = both[j][:ch] + o_loc[i]
        new = [w[ch:] + h_mat[i] for w, i in zip(both, ids)]
        if independent:
            for j in range(n_pair):
                _store_pair_state(sfin_ref, sub, j, new[j])
        else:
            state = new
    if not independent:
        for j in range(n_pair):
            st_ref[j] = state[j]

    o_raw = oraw_ref[...]
    oc = o_raw - _head_sum(o_raw, ones) * (1.0 / HEAD_DIM)
    var = _head_sum(oc * oc, ones) * (1.0 / HEAD_DIM)
    o_n = oc * lax.rsqrt(var + RWKV_GN_EPS) * gng_ref[...] + gnb_ref[...]
    bonus = _head_sum(r * kmod * rk_ref[...], ones)
    o_ref[0] = (o_n + bonus * v) * g

    if not independent:
        @pl.when(c == pl.num_programs(1) - 1)
        def _():
            for j in range(n_pair):
                _store_pair_state(sfin_ref, 0, j, st_ref[j])


def _rwkv(ha, shift_all, s0_all, st_idx, p, layer, t_valid, n_sub, independent):
    b, tp, _ = ha.shape
    rows = n_sub * RWKV_CHUNK
    if independent:
        assert tp == RWKV_CHUNK and b % n_sub == 0
        groups, nc, per = b // n_sub, 1, n_sub
        ha = ha.reshape(groups, rows, A_COLS)
    else:
        groups, nc, per = b, tp // rows, 1
    shift_all = shift_all.reshape(shift_all.shape[0], groups, per, A_COLS)
    st_spec = pl.BlockSpec((per, H_A, HEAD_DIM, HEAD_DIM), lambda i, c: (i, 0, 0, 0))
    lane_head = jnp.arange(W_A) // HEAD_DIM
    ones = (lane_head[:, None] == lane_head[None, :]).astype(BF16)
    names = ('rwkv_mu', 'rwkv_w0', 'rwkv_w_lora', 'rwkv_a0', 'rwkv_a_lora', 'rwkv_g_lora', 'rwkv_k_k',
             'rwkv_k_a', 'rwkv_r_k', 'rwkv_gn_g', 'rwkv_gn_b')
    kern = functools.partial(_rwkv_kernel, t_valid=None if t_valid == tp else t_valid, n_sub=n_sub,
                             independent=independent)
    o, s_new = pl.pallas_call(
        kern,
        grid=(groups, nc),
        in_specs=[pl.BlockSpec((1, rows, A_COLS), lambda i, c: (i, c, 0)),
                  pl.BlockSpec((None, 1, per, A_COLS), lambda i, c: (st_idx, i, 0, 0)),
                  pl.BlockSpec((None, per, H_A, HEAD_DIM, HEAD_DIM), lambda i, c: (st_idx, i, 0, 0, 0))] +
                 [_layer_spec(p[k], layer) for k in names] +
                 [pl.BlockSpec((W_A, W_A), lambda i, c: (0, 0))],
        out_specs=[pl.BlockSpec((1, rows, W_A), lambda i, c: (i, c, 0)), st_spec],
        out_shape=[jax.ShapeDtypeStruct((groups, nc * rows, W_A), F32),
                   jax.ShapeDtypeStruct((b, H_A, HEAD_DIM, HEAD_DIM), F32)],
        scratch_shapes=[pltpu.VMEM((rows + 8, A_COLS), F32),
                        pltpu.VMEM((W_A // LANES, HEAD_DIM, LANES), F32),
                        pltpu.VMEM((rows, W_A), F32)],
        compiler_params=_params(("parallel", "arbitrary")),
        name="rwkv7",
    )(ha, shift_all, s0_all, *[p[k] for k in names], ones)
    return o.reshape(b, tp, W_A), s_new


ATT_SEG = WIN_MAX
ATT_UNITS = 2


def _attn_prompt_kernel(q_ref, k_ref, v_ref, o_ref, m_ref, l_ref, n_ref):
    seg = pl.program_id(1)
    blk = ATT_BLK
    n_slab = W_B // LANES
    heads = [(j, half) for j in range(n_slab) for half in (0, 1)]
    low = lax.broadcasted_iota(jnp.int32, (blk, LANES), 1) < HEAD_DIM
    qi = lax.broadcasted_iota(jnp.int32, (blk, blk), 0)
    ki = lax.broadcasted_iota(jnp.int32, (blk, blk), 1)
    bias_own = jnp.where(ki <= qi, 0.0, NEG)
    ok_prev = ki >= qi
    scale = HEAD_DIM ** -0.5

    def step(it, carry, *, win, dil, first, last):
        log_d = int(math.log2(dil))
        ds = (lambda s: pl.ds(s, blk, stride=dil)) if dil > 1 else (lambda s: pl.ds(s, blk))
        units = [it * ATT_UNITS + i for i in range(ATT_UNITS)]
        start = [(u >> log_d) * (blk * dil) + (u & (dil - 1)) for u in units]
        k_start = [seg * ATT_SEG + s for s in start]
        p_start = [jnp.maximum(s - win, 0) for s in k_start]
        bias = [jnp.concatenate([jnp.where(ok_prev, jnp.where(s >= win, 0.0, NEG), NEG), bias_own], axis=1)
                for s in k_start]
        tiles = [(i, j) for i in range(ATT_UNITS) for j in range(n_slab)]
        chains = [(t, i, j, half) for t, (i, j) in enumerate(tiles) for half in (0, 1)]
        both = lambda ref, i, j: jnp.concatenate([ref[j, ds(p_start[i]), :].astype(BF16),
                                                  ref[j, ds(k_start[i]), :].astype(BF16)], axis=0)
        q = [q_ref[j, ds(start[i]), :] * scale for i, j in tiles]
        k_cat = [both(k_ref, i, j) for i, j in tiles]
        v_cat = [both(v_ref, i, j) for i, j in tiles]
        qh = [jnp.where(low if half == 0 else ~low, q[t], 0.0).astype(BF16) for t, _, _, half in chains]
        s = [_dot_nt(x, k_cat[t]) + bias[i] for x, (t, i, _, _) in zip(qh, chains)]
        m = [jnp.max(x, axis=-1, keepdims=True) for x in s]
        p = [jnp.exp(x - mm) for x, mm in zip(s, m)]
        l = [jnp.sum(x, axis=-1, keepdims=True) for x in p]
        acc = [_dot(x.astype(BF16), v_cat[t]) for x, (t, _, _, _) in zip(p, chains)]
        for t, (i, j) in enumerate(tiles):
            rows = ds(start[i])
            m_t = jnp.where(low, m[2 * t], m[2 * t + 1])
            l_t = jnp.where(low, l[2 * t], l[2 * t + 1])
            n_t = jnp.where(low, acc[2 * t], acc[2 * t + 1])
            if not first:
                m_o = m_ref[j, rows, :]
                m_n = jnp.maximum(m_o, m_t)
                e_o, e_t = jnp.exp(m_o - m_n), jnp.exp(m_t - m_n)
                l_t = l_ref[j, rows, :] * e_o + l_t * e_t
                n_t = n_ref[j, rows, :] * e_o + n_t * e_t
                m_t = m_n
            if last:
                o_ref[j, rows, :] = n_t / l_t
            else:
                m_ref[j, rows, :] = m_t
                l_ref[j, rows, :] = l_t
                n_ref[j, rows, :] = n_t
        return carry

    order = sorted(DILATED, key=lambda wd: -wd[1])
    for idx, (win, dil) in enumerate(order):
        lax.fori_loop(0, ATT_SEG // (blk * ATT_UNITS),
                      functools.partial(step, win=win, dil=dil, first=idx == 0, last=idx == len(order) - 1), 0)


def _attn_prompt(hb_slabs, bsz, t):
    n_slab = W_B // LANES
    assert t % ATT_SEG == 0 and all(win // dil == ATT_BLK and ATT_SEG % win == 0 for win, dil in DILATED)
    nseg = t // ATT_SEG
    state = pltpu.VMEM((n_slab, ATT_SEG, LANES), F32)
    whole = lambda part: pl.BlockSpec((n_slab, t, LANES), lambda i, s: (part, i, 0))
    seg_spec = pl.BlockSpec((n_slab, ATT_SEG, LANES), lambda i, s: (0, i * nseg + s, 0))
    return pl.pallas_call(
        _attn_prompt_kernel,
        grid=(bsz, nseg),
        in_specs=[seg_spec, whole(1), whole(2)],
        out_specs=seg_spec,
        out_shape=jax.ShapeDtypeStruct((n_slab, bsz * t, LANES), F32),
        scratch_shapes=[state, state, state],
        compiler_params=_params(("parallel", "arbitrary")),
        name="attn_prompt",
    )(hb_slabs, hb_slabs, hb_slabs)


def _kv_tail_kernel(k_ref, v_ref, *rest):
    kt_ref, vt_ref = rest[-2:]
    for j in range(W_B // LANES):
        kt_ref[0, 0, j * LANES:(j + 1) * LANES, :] = k_ref[j].T
        vt_ref[0, 0, j * LANES:(j + 1) * LANES, :] = v_ref[j].T


def _kv_tail(hb_slabs, bsz, t, w_keep, layer, depth, prev):
    n_slab = W_B // LANES
    assert t % w_keep == 0
    per = t // w_keep
    tail = lambda part: pl.BlockSpec((n_slab, w_keep, LANES), lambda i: (part, i * per + per - 1, 0))
    out = pl.BlockSpec((1, 1, W_B, w_keep), lambda i: (layer, i, 0, 0))
    in_specs, args, aliases = [tail(1), tail(2)], [hb_slabs, hb_slabs], {}
    if prev is not None:
        in_specs += [pl.BlockSpec(memory_space=pl.ANY)] * 2
        args += list(prev)
        aliases = {2: 0, 3: 1}
    return pl.pallas_call(
        _kv_tail_kernel,
        grid=(bsz,),
        in_specs=in_specs,
        out_specs=[out, out],
        out_shape=[jax.ShapeDtypeStruct((depth, bsz, W_B, w_keep), F32)] * 2,
        input_output_aliases=aliases,
        compiler_params=_params(("parallel",)),
        name="kv_tail",
    )(*args)


def _attn_sample_kernel(hb_ref, ck_ref, cv_ref, *rest, t_new, rows):
    o_ref, nk_ref, nv_ref = rest[-3:]
    w_buf = ck_ref.shape[3]
    hb = hb_ref[0]
    q, k_new, v_new = hb[:, :W_B], hb[:, W_B:2 * W_B], hb[:, 2 * W_B:]
    k_t, v_t = ck_ref[0, 0], cv_ref[0, 0]
    scale = HEAD_DIM ** -0.5
    nrow = H_B * rows

    def iota3(shape, dim):
        return lax.broadcasted_iota(jnp.int32, (H_B, rows) + shape, dim).reshape((nrow,) + shape)

    head_of_lane = iota3((W_B,), 2) >> int(math.log2(HEAD_DIM))
    own_head = iota3((W_B,), 0) == head_of_lane
    q_bd = jnp.where(own_head, jnp.concatenate([q] * H_B, axis=0), 0.0).astype(BF16)
    s_c = _dot(q_bd, k_t.astype(BF16)) * scale
    s_n = _dot_nt(q_bd, k_new.astype(BF16)) * scale
    d_c = w_buf + iota3((w_buf,), 1) - iota3((w_buf,), 2)
    d_n = iota3((rows,), 1) - iota3((rows,), 2)
    new_ok = (d_n >= 0) & (iota3((rows,), 2) < t_new)
    ms, ls, pcs, pns = [], [], [], []
    for win, dil in DILATED:
        sc = jnp.where(((d_c & (dil - 1)) == 0) & (d_c <= win), s_c, NEG)
        sn = jnp.where(new_ok & ((d_n & (dil - 1)) == 0) & (d_n <= win), s_n, NEG)
        m = jnp.maximum(jnp.max(sc, axis=-1, keepdims=True), jnp.max(sn, axis=-1, keepdims=True))
        pc, pn = jnp.exp(sc - m), jnp.exp(sn - m)
        ms.append(m)
        ls.append(jnp.sum(pc, axis=-1, keepdims=True) + jnp.sum(pn, axis=-1, keepdims=True))
        pcs.append(pc.astype(BF16))
        pns.append(pn.astype(BF16))
    acc = _dot_nt(jnp.concatenate(pcs, axis=0), v_t.astype(BF16)) + \
        _dot(jnp.concatenate(pns, axis=0), v_new.astype(BF16))
    m_all = jnp.maximum(jnp.maximum(ms[0], ms[1]), ms[2])
    es = [jnp.exp(m - m_all) for m in ms]
    num = sum(e * acc[g * nrow:(g + 1) * nrow] for g, e in enumerate(es))
    den = sum(e * l for e, l in zip(es, ls))
    o_rows = jnp.where(own_head, num / den, 0.0)
    o_ref[0] = sum(o_rows[h * rows:(h + 1) * rows] for h in range(H_B))

    lane = lax.broadcasted_iota(jnp.int32, (rows, LANES), 1)
    tok = lax.broadcasted_iota(jnp.int32, (rows, LANES), 0)
    place = ((lane == tok + (LANES - t_new)) & (tok < t_new)).astype(BF16)

    def transposed_tail(x):
        hi = x.astype(BF16)
        rem = x - hi.astype(F32)
        mid = rem.astype(BF16)
        lo = (rem - mid.astype(F32)).astype(BF16)
        return _dot_tn(hi, place) + _dot_tn(mid, place) + _dot_tn(lo, place)

    tail_lane = lax.broadcasted_iota(jnp.int32, (W_B, LANES), 1) >= LANES - t_new
    for src, new, dst in ((k_t, k_new, nk_ref), (v_t, v_new, nv_ref)):
        rolled = pltpu.roll(src, w_buf - t_new, 1)
        dst[0, 0, :, 0:w_buf - LANES] = rolled[:, 0:w_buf - LANES]
        dst[0, 0, :, w_buf - LANES:] = jnp.where(tail_lane, transposed_tail(new), rolled[:, w_buf - LANES:])


def _attn_sample(hb, cache_kt, cache_vt, layer, t_new, prev):
    b, rows, _ = hb.shape
    depth, _, _, w_buf = cache_kt.shape
    cspec = pl.BlockSpec((1, 1, W_B, w_buf), lambda i: (layer, i, 0, 0))
    in_specs = [pl.BlockSpec((1, rows, B_COLS), lambda i: (i, 0, 0)), cspec, cspec]
    args = [hb, cache_kt, cache_vt]
    aliases = {}
    if prev is not None:
        in_specs += [pl.BlockSpec(memory_space=pl.ANY)] * 2
        args += list(prev)
        aliases = {3: 1, 4: 2}
    new_shape = jax.ShapeDtypeStruct((depth, b, W_B, w_buf), F32)
    return pl.pallas_call(
        functools.partial(_attn_sample_kernel, t_new=t_new, rows=rows),
        grid=(b,),
        in_specs=in_specs,
        out_specs=[pl.BlockSpec((1, rows, W_B), lambda i: (i, 0, 0)), cspec, cspec],
        out_shape=[jax.ShapeDtypeStruct((b, rows, W_B), F32), new_shape, new_shape],
        input_output_aliases=aliases,
        compiler_params=_params(("parallel",)),
        name="attn_sample",
    )(*args)


def _retention_kernel(q_ref, k_ref, v_ref, g_ref, r0_ref, gng_ref, gnb_ref, ones_ref, o_ref, rfin_ref, st_ref,
                      *, ch, rows, n_sub, independent):
    c = pl.program_id(1)
    n_pair = W_C // LANES
    pair_of = lambda ref, s, j: jnp.concatenate([ref[s, 2 * j], ref[s, 2 * j + 1]], axis=1)

    if not independent:
        @pl.when(c == 0)
        def _():
            for j in range(n_pair):
                st_ref[j] = pair_of(r0_ref, 0, j)

    qi = lax.broadcasted_iota(jnp.int32, (rows, rows), 0)
    kj = lax.broadcasted_iota(jnp.int32, (rows, rows), 1)
    rel = (qi - kj).astype(F32)
    pos = lax.broadcasted_iota(jnp.int32, (rows, LANES), 0).astype(F32)
    low = lax.broadcasted_iota(jnp.int32, (rows, LANES), 1) < HEAD_DIM
    low_st = lax.broadcasted_iota(jnp.int32, (HEAD_DIM, LANES), 1) < HEAD_DIM
    halves = (jnp.where(low_st, 1.0, 0.0).astype(BF16), jnp.where(low_st, 0.0, 1.0).astype(BF16))
    log_gamma = [math.log(1.0 - 2.0 ** (-5.0 - hd)) for hd in range(H_C)]
    dmask = [jnp.where(rel >= 0, jnp.exp(lg * jnp.maximum(rel, 0.0)), 0.0) for lg in log_gamma]
    lg_lane = [jnp.where(low, log_gamma[2 * j], log_gamma[2 * j + 1]) for j in range(n_pair)]
    lg_st = [jnp.where(low_st, log_gamma[2 * j], log_gamma[2 * j + 1]) for j in range(n_pair)]
    ones = ones_ref[...]

    blocks = [(sub, j) for sub in range(n_sub) for j in range(n_pair)]
    tile = lambda ref, sub, j: ref[sub, :, j * LANES:(j + 1) * LANES]
    q = [tile(q_ref, sub, j) for sub, j in blocks]
    k = [tile(k_ref, sub, j) for sub, j in blocks]
    v = [tile(v_ref, sub, j).astype(BF16) for sub, j in blocks]
    k_b = [x.astype(BF16) for x in k]
    att = [[_dot_nt(jnp.where(low if half == 0 else ~low, x, 0.0).astype(BF16), kb) * dmask[2 * j + half]
            for half in (0, 1)] for x, kb, (_, j) in zip(q, k_b, blocks)]
    o_intra = [jnp.where(low, _dot(a0.astype(BF16), vv), _dot(a1.astype(BF16), vv)) for (a0, a1), vv in zip(att, v)]
    q_dec = [(x * jnp.exp(lg_lane[j] * (pos + 1.0))).astype(BF16) for x, (_, j) in zip(q, blocks)]
    kv = [_pair_diag(_dot_tn((x * jnp.exp(lg_lane[j] * (ch - 1.0 - pos))).astype(BF16), vv), low_st)
          for x, vv, (_, j) in zip(k, v, blocks)]

    def store_state(s, j, pair):
        rfin_ref[s, 2 * j] = pair[:, :HEAD_DIM]
        rfin_ref[s, 2 * j + 1] = pair[:, HEAD_DIM:]

    state = None if independent else [st_ref[j] for j in range(n_pair)]
    outs = []
    for i, (sub, j) in enumerate(blocks):
        st = pair_of(r0_ref, sub, j) if independent else state[j]
        st_b = st.astype(BF16)
        st_bd = jnp.concatenate([st_b * halves[0], st_b * halves[1]], axis=0)
        outs.append(o_intra[i] + _dot(q_dec[i], st_bd))
        st = st * jnp.exp(lg_st[j] * ch) + kv[i]
        if independent:
            store_state(sub, j, st)
        else:
            state[j] = st
    if not independent:
        for j in range(n_pair):
            st_ref[j] = state[j]

    for o, (sub, j) in zip(outs, blocks):
        sl = slice(j * LANES, (j + 1) * LANES)
        oc = o - _head_sum(o, ones) * (1.0 / HEAD_DIM)
        var = _head_sum(oc * oc, ones) * (1.0 / HEAD_DIM)
        gate = tile(g_ref, sub, j)
        o_ref[sub, :, sl] = (oc * lax.rsqrt(var + LN_EPS) * gng_ref[:, sl] + gnb_ref[:, sl]) * (gate * _sigmoid(gate))

    if not independent:
        @pl.when(c == pl.num_programs(1) - 1)
        def _():
            for j in range(n_pair):
                store_state(0, j, st_ref[j])


def _retention(hc, r0_all, st_idx, p, layer, rows, ch, n_sub, independent):
    b, tp, _ = hc.shape
    if independent:
        assert tp == rows and b % n_sub == 0
        groups, nc, per = b // n_sub, 1, n_sub
    else:
        groups, nc, per = b, tp // (rows * n_sub), 1
    hc = hc.reshape(b * tp // rows, rows, C_COLS)
    part = lambda j: pl.BlockSpec((n_sub, rows, W_C), lambda i, c: (i * nc + c, 0, j))
    st_spec = pl.BlockSpec((per, H_C, HEAD_DIM, HEAD_DIM), lambda i, c: (i, 0, 0, 0))
    lane_head = jnp.arange(LANES) // HEAD_DIM
    ones = (lane_head[:, None] == lane_head[None, :]).astype(BF16)
    o, r_fin = pl.pallas_call(
        functools.partial(_retention_kernel, ch=ch, rows=rows, n_sub=n_sub, independent=independent),
        grid=(groups, nc),
        in_specs=[part(0), part(1), part(2), part(3),
                  pl.BlockSpec((None, per, H_C, HEAD_DIM, HEAD_DIM), lambda i, c: (st_idx, i, 0, 0, 0)),
                  _layer_spec(p['ret_gn_g'], layer), _layer_spec(p['ret_gn_b'], layer),
                  pl.BlockSpec((LANES, LANES), lambda i, c: (0, 0))],
        out_specs=[part(0), st_spec],
        out_shape=[jax.ShapeDtypeStruct((b * tp // rows, rows, W_C), F32),
                   jax.ShapeDtypeStruct((b, H_C, HEAD_DIM, HEAD_DIM), F32)],
        scratch_shapes=[pltpu.VMEM((W_C // LANES, HEAD_DIM, LANES), F32)],
        compiler_params=_params(("parallel", "arbitrary")),
        name="retention",
    )(hc, hc, hc, hc, r0_all, p['ret_gn_g'], p['ret_gn_b'], ones)
    return o.reshape(b, tp, W_C), r_fin


def _out_proj_kernel(oa_ref, ob_ref, oc_ref, x_ref, w_ref, g_ref, b_ref, y_ref, *, alpha):
    mixed = jnp.concatenate([oa_ref[...].astype(BF16)] + [ob_ref[j].astype(BF16) for j in range(W_B // LANES)] +
                            [oc_ref[...].astype(BF16)], axis=1)
    y_ref[...] = _layer_norm(alpha * x_ref[...] + _dot(mixed, w_ref[...]), g_ref[...], b_ref[...], LN_EPS)


def _out_proj(oa, ob, oc, x2d, w_bf16, p, layer, alpha, bm):
    n, d = x2d.shape
    row = lambda w: pl.BlockSpec((bm, w), lambda i: (i, 0))
    return pl.pallas_call(
        functools.partial(_out_proj_kernel, alpha=alpha),
        grid=(n // bm,),
        in_specs=[row(W_A), pl.BlockSpec((W_B // LANES, bm, LANES), lambda i: (0, i, 0)), row(W_C), row(d),
                  _layer_spec(w_bf16, layer), _layer_spec(p['ln1_g'], layer), _layer_spec(p['ln1_b'], layer)],
        out_specs=row(d),
        out_shape=jax.ShapeDtypeStruct((n, d), F32),
        compiler_params=_params(("parallel",)),
        name="out_proj",
    )(oa, ob, oc, x2d, w_bf16, p['ln1_g'], p['ln1_b'])


FFN_CHUNK = 256
FFN_ROWS = 1024


def _ffn_kernel(x_ref, wg_ref, wu_ref, wd_ref, g_ref, b_ref, y_ref, *, alpha):
    x = x_ref[...]
    xb = x.astype(BF16)
    d_ff = wg_ref.shape[1]
    acc = alpha * x
    for j in range(d_ff // FFN_CHUNK):
        cs = slice(j * FFN_CHUNK, (j + 1) * FFN_CHUNK)
        gate = _dot(xb, wg_ref[:, cs])
        up = _dot(xb, wu_ref[:, cs])
        act = gate * _sigmoid(gate) * up
        acc += _dot(act.astype(BF16), wd_ref[cs, :])
    y_ref[...] = _layer_norm(acc, g_ref[...], b_ref[...], LN_EPS)


def _ffn(x2d, wg, wu, wd, p, layer, alpha, bm):
    n, d = x2d.shape
    row = pl.BlockSpec((bm, d), lambda i: (i, 0))
    full = lambda w: _layer_spec(w, layer, pipeline_mode=pl.Buffered(1))
    return pl.pallas_call(
        functools.partial(_ffn_kernel, alpha=alpha),
        grid=(n // bm,),
        in_specs=[row, full(wg), full(wu), full(wd), _layer_spec(p['ln2_g'], layer), _layer_spec(p['ln2_b'], layer)],
        out_specs=row,
        out_shape=jax.ShapeDtypeStruct((n, d), F32),
        compiler_params=_params(("parallel",)),
        name="ffn",
    )(x2d, wg, wu, wd, p['ln2_g'], p['ln2_b'])


def _pad_rows(z, rows):
    return jnp.pad(z, ((0, 0), (0, rows - z.shape[1]), (0, 0)))


def _layer(x, state, pos0, p, wts, layer, alpha, prev_kv, caches=None):
    bsz, t, d = x.shape
    n = bsz * t
    depth = wts['w_in'].shape[0]
    x2d = x.reshape(n, d)
    bm = min(512, n)
    n_slab = W_B // LANES
    pos = pos0 + jnp.arange(t, dtype=F32)
    reps = max(bm // t, 1)
    tile = lambda tabs: tuple(jnp.tile(z, (reps, 1)) for z in tabs)
    inv_b = ROPE_THETA ** (-jnp.arange(0, ROT_DIM, 2, dtype=F32) / ROT_DIM)
    inv_c = 1.0 / (RET_THETA ** jnp.linspace(0.0, 1.0, HEAD_DIM // 2, dtype=F32))
    ha, hb_slabs, hc = _in_proj(x2d, wts['w_in'], layer, tile(_rot_tables(pos, inv_b, ROT_DIM)),
                                tile(_rot_tables(pos, inv_c, HEAD_DIM)), bm)
    ha = ha.reshape(bsz, t, A_COLS)
    hc = hc.reshape(bsz, t, C_COLS)
    shift_new = ha[:, -1]
    shift_all, wkv_all, ret_all, st_idx = state

    tp = -(-t // RWKV_CHUNK) * RWKV_CHUNK
    if tp % (RWKV_SUBCHUNKS * RWKV_CHUNK) == 0:
        n_sub, independent = RWKV_SUBCHUNKS, False
    elif tp == RWKV_CHUNK and bsz % RWKV_SEQS == 0:
        n_sub, independent = RWKV_SEQS, True
    else:
        n_sub, independent = 1, False
    o_a, wkv_new = _rwkv(_pad_rows(ha, tp), shift_all, wkv_all, st_idx, p, layer, t, n_sub, independent)
    o_a = o_a[:, :t].reshape(n, W_A)

    if caches is None:
        o_b = _attn_prompt(hb_slabs, bsz, t)
        k_keep, v_keep = _kv_tail(hb_slabs, bsz, t, min(WIN_MAX, t), layer, depth, prev_kv)
    else:
        rows = -(-t // 8) * 8
        hb = jnp.swapaxes(hb_slabs, 0, 1).reshape(bsz, t, B_COLS)
        o_b, k_keep, v_keep = _attn_sample(_pad_rows(hb, rows), caches[0], caches[1], layer, t, prev_kv)
        o_b = jnp.swapaxes(o_b[:, :t].reshape(n, n_slab, LANES), 0, 1)

    if t % RET_CHUNK == 0:
        o_c, ret_new = _retention(hc, ret_all, st_idx, p, layer, RET_CHUNK, RET_CHUNK,
                                  RET_SUBCHUNKS if t % (RET_SUBCHUNKS * RET_CHUNK) == 0 else 1, False)
    else:
        rows = -(-t // 8) * 8
        seqs = RET_SEQS if bsz % RET_SEQS == 0 else 1
        o_c, ret_new = _retention(_pad_rows(hc, rows), ret_all, st_idx, p, layer, rows, t, seqs, seqs > 1)
        o_c = o_c[:, :t]
    o_c = o_c.reshape(n, W_C)

    x1 = _out_proj(o_a, o_b, o_c, x2d, wts['w_out'], p, layer, alpha, bm)
    x2 = _ffn(x1, wts['w_ffn_gate'], wts['w_ffn_up'], wts['w_ffn_down'], p, layer, alpha, min(FFN_ROWS, n))
    return x2.reshape(bsz, t, d), (shift_new, wkv_new, k_keep, v_keep, ret_new)


def _token_major(z):
    z = z.reshape(z.shape[:-2] + (H_B, HEAD_DIM, z.shape[-1]))
    return jnp.moveaxis(z, -1, -3)


def kernel(x_prompt, x_sample, state_rwkv_shift, state_rwkv_wkv, cache_win_k, cache_win_v, state_ret, w_in, rwkv_mu, rwkv_w0, rwkv_w_lora, rwkv_a0, rwkv_a_lora, rwkv_g_lora, rwkv_k_k, rwkv_k_a, rwkv_r_k, rwkv_gn_g, rwkv_gn_b, ret_gn_g, ret_gn_b, w_out, ln1_g, ln1_b, w_ffn_gate, w_ffn_up, w_ffn_down, ln2_g, ln2_b):
    depth = w_in.shape[0]
    alpha = (2 * depth) ** 0.25
    vectors = dict(rwkv_mu=rwkv_mu, rwkv_w0=rwkv_w0, rwkv_a0=rwkv_a0, rwkv_k_k=rwkv_k_k, rwkv_k_a=rwkv_k_a,
                   rwkv_r_k=rwkv_r_k, rwkv_gn_g=rwkv_gn_g, rwkv_gn_b=rwkv_gn_b, ret_gn_g=ret_gn_g,
                   ret_gn_b=ret_gn_b, ln1_g=ln1_g, ln1_b=ln1_b, ln2_g=ln2_g, ln2_b=ln2_b)
    p = {k: v.reshape(depth, 1, -1) for k, v in vectors.items()}
    p.update(rwkv_w_lora=rwkv_w_lora.astype(BF16), rwkv_a_lora=rwkv_a_lora.astype(BF16),
             rwkv_g_lora=rwkv_g_lora.astype(BF16))
    wts = {'w_in': w_in.astype(BF16), 'w_out': w_out.astype(BF16), 'w_ffn_gate': w_ffn_gate.astype(BF16),
           'w_ffn_up': w_ffn_up.astype(BF16), 'w_ffn_down': w_ffn_down.astype(BF16)}
    bp = x_prompt.shape[0]
    empty = (jnp.zeros((1, bp, A_COLS), F32), jnp.zeros((1, bp, H_A, HEAD_DIM, HEAD_DIM), F32),
             jnp.zeros((1, bp, H_C, HEAD_DIM, HEAD_DIM), F32), 0)
    to_minor = lambda c: jnp.moveaxis(c, 2, -1).reshape(c.shape[:2] + (W_B, c.shape[2]))
    caches = (to_minor(cache_win_k), to_minor(cache_win_v))
    xp, xs = x_prompt, x_sample
    p_states, s_states = [], []
    p_kv = s_kv = None
    for l in range(depth):
        xp, sp = _layer(xp, empty, 0.0, p, wts, l, alpha, p_kv)
        xs, ss = _layer(xs, (state_rwkv_shift, state_rwkv_wkv, state_ret, l), float(PAST_LEN), p, wts, l, alpha,
                        s_kv, caches=caches)
        p_kv, s_kv = (sp[2], sp[3]), (ss[2], ss[3])
        p_states.append(sp)
        s_states.append(ss)
    stack = lambda states, j: jnp.stack([s[j] for s in states])
    outs = []
    for states, kv in ((p_states, p_kv), (s_states, s_kv)):
        outs += [stack(states, 0), stack(states, 1), _token_major(kv[0]), _token_major(kv[1]), stack(states, 4)]
    return (xp, xs) + tuple(outs)
```

```python
import functools
import math

import jax
import jax.numpy as jnp
from jax import lax
from jax.experimental import pallas as pl
from jax.experimental.pallas import tpu as pltpu

F32 = jnp.float32
BF16 = jnp.bfloat16

PAST_LEN = 8192
HEAD_DIM = 64
H_A, H_B, H_C = 6, 6, 4
W_A, W_B, W_C = H_A * HEAD_DIM, H_B * HEAD_DIM, H_C * HEAD_DIM
LORA_W, LORA_A, LORA_G = 64, 64, 128
A_COLS = 3 * W_A + LORA_W + LORA_A + LORA_G
B_COLS = 3 * W_B
C_COLS = 4 * W_C
DILATED = ((128, 1), (512, 4), (2048, 16))
WIN_MAX = max(w for w, _ in DILATED)
ATT_BLK = 128
ROT_DIM = HEAD_DIM // 4
ROPE_THETA = 500000.0
RET_THETA = 10000.0
RET_CHUNK = 128
RET_SEQS = 8
RET_SUBCHUNKS = 4
RWKV_CHUNK = 64
RWKV_SUBCHUNKS = 4
RWKV_SEQS = 8
RWKV_DECAY_SCALE = math.exp(-0.5)
RWKV_GN_EPS = 64e-5
LN_EPS = 1e-5
NEG = -1e30
LANES = 128
SUBLANES = 8
VMEM_LIMIT = 56 * 1024 * 1024


def _params(sem, vmem=VMEM_LIMIT):
    return pltpu.CompilerParams(dimension_semantics=sem, vmem_limit_bytes=vmem)


def _layer_spec(arr, layer, **kw):
    zeros = (0,) * (arr.ndim - 1)
    return pl.BlockSpec((None,) + arr.shape[1:], lambda *_: (layer,) + zeros, **kw)


def _dot(a, b):
    return jnp.dot(a, b, preferred_element_type=F32)


def _dot_nt(a, b):
    return lax.dot_general(a, b, (((1,), (1,)), ((), ())), preferred_element_type=F32)


def _dot_tn(a, b):
    return lax.dot_general(a, b, (((0,), (0,)), ((), ())), preferred_element_type=F32)


def _sigmoid(x):
    return 1.0 / (1.0 + jnp.exp(-x))


def _layer_norm(x, g, b, eps):
    mu = jnp.mean(x, axis=-1, keepdims=True)
    xc = x - mu
    var = jnp.mean(xc * xc, axis=-1, keepdims=True)
    return xc * lax.rsqrt(var + eps) * g + b


def _rot_tables(pos, inv_freq, rot_width):
    half = rot_width // 2
    ang = pos[:, None] * inv_freq[None, :]
    cos, sin = jnp.cos(ang), jnp.sin(ang)
    lane = jnp.arange(LANES) % HEAD_DIM
    idx = lane % half
    cos_l = jnp.where(lane[None, :] < rot_width, cos[:, idx], 1.0)
    sin_l = sin[:, idx]
    sin_up = jnp.where(lane[None, :] < half, -sin_l, 0.0)
    sin_dn = jnp.where((lane[None, :] >= half) & (lane[None, :] < rot_width), sin_l, 0.0)
    return cos_l.astype(F32), sin_up.astype(F32), sin_dn.astype(F32)


def _rotate_slab(x, cos, sin_up, sin_dn, half):
    up = pltpu.roll(x, LANES - half, 1)
    dn = pltpu.roll(x, half, 1)
    return x * cos + up * sin_up + dn * sin_dn


def _in_proj_kernel(x_ref, w_ref, cb_ref, ub_ref, db_ref, cc_ref, uc_ref, dc_ref,
                    ha_ref, hb_ref, hc_ref):
    h = _dot(x_ref[...].astype(BF16), w_ref[...])
    ha_ref[...] = h[:, :A_COLS]
    cb, ub, db = cb_ref[...], ub_ref[...], db_ref[...]
    for j in range(B_COLS // LANES):
        slab = h[:, A_COLS + j * LANES:A_COLS + (j + 1) * LANES]
        if j < 2 * W_B // LANES:
            slab = _rotate_slab(slab, cb, ub, db, ROT_DIM // 2)
        hb_ref[j] = slab
    cc, uc, dc = cc_ref[...], uc_ref[...], dc_ref[...]
    c0 = A_COLS + B_COLS
    for j in range(C_COLS // LANES):
        slab = h[:, c0 + j * LANES:c0 + (j + 1) * LANES]
        if j < 2 * W_C // LANES:
            slab = _rotate_slab(slab, cc, uc, dc, HEAD_DIM // 2)
        if W_C // LANES <= j < 2 * W_C // LANES:
            slab = slab * (HEAD_DIM ** -0.5)
        hc_ref[:, j * LANES:(j + 1) * LANES] = slab


def _in_proj(x2d, w_bf16, layer, tabs_b, tabs_c, bm):
    n, d = x2d.shape
    t_rows = tabs_b[0].shape[0]
    nt = t_rows // bm
    tab_spec = pl.BlockSpec((bm, LANES), lambda i: (i % nt, 0))
    row = lambda w: pl.BlockSpec((bm, w), lambda i: (i, 0))
    return pl.pallas_call(
        _in_proj_kernel,
        grid=(n // bm,),
        in_specs=[row(d), _layer_spec(w_bf16, layer)] + [tab_spec] * 6,
        out_specs=[row(A_COLS), pl.BlockSpec((B_COLS // LANES, bm, LANES), lambda i: (0, i, 0)), row(C_COLS)],
        out_shape=[jax.ShapeDtypeStruct((n, A_COLS), F32),
                   jax.ShapeDtypeStruct((B_COLS // LANES, n, LANES), F32),
                   jax.ShapeDtypeStruct((n, C_COLS), F32)],
        compiler_params=_params(("parallel",)),
        name="in_proj",
    )(x2d, w_bf16, *tabs_b, *tabs_c)


assert RWKV_CHUNK == HEAD_DIM


def _split(x):
    hi = x.astype(BF16)
    return hi, (x - hi.astype(F32)).astype(BF16)


def _mm3(dot, a, b):
    a_axis = 0 if dot is _dot_tn else 1
    b_axis = 1 if dot is _dot_nt else 0
    return dot(jnp.concatenate([a[0], a[1]], axis=a_axis), jnp.concatenate([b[0], b[0]], axis=b_axis)) + \
        dot(a[0], b[1])


def _block_diag(pair, lane_masks):
    first, second = lane_masks
    return tuple(jnp.concatenate([p * first, p * second], axis=0) for p in pair)


def _pair_from_state(s_a, s_b):
    return jnp.concatenate([s_a.T, s_b.T], axis=1)


def _store_pair_state(ref, idx, j, pair):
    ref[idx, 2 * j] = pair[:, :HEAD_DIM].T
    ref[idx, 2 * j + 1] = pair[:, HEAD_DIM:].T


def _pair_diag(res, low):
    half = res.shape[0] // 2
    return jnp.where(low, res[:half], res[half:])


def _head_sum(z, ones):
    hi, lo = _split(z)
    return _dot(hi, ones) + _dot(lo, ones)


def _rwkv_kernel(ha_ref, shift_ref, s0_ref, mu_ref, w0_ref, wl_ref, a0_ref, al_ref, gl_ref,
                 kk_ref, ka_ref, rk_ref, gng_ref, gnb_ref, ones_ref, o_ref, sfin_ref,
                 hp_ref, st_ref, oraw_ref, *, t_valid, n_sub, independent):
    c = pl.program_id(1)
    ch = RWKV_CHUNK
    rows = n_sub * ch
    seg_row = lax.broadcasted_iota(jnp.int32, (rows, 1), 0) & (ch - 1)

    @pl.when(c == 0)
    def _():
        if independent:
            hp_ref[0:SUBLANES, :] = jnp.zeros((SUBLANES, A_COLS), F32)
        else:
            hp_ref[0:SUBLANES, :] = jnp.broadcast_to(shift_ref[0], (SUBLANES, A_COLS))
            for j in range(W_A // LANES):
                st_ref[j] = _pair_from_state(s0_ref[0, 2 * j], s0_ref[0, 2 * j + 1])

    h = ha_ref[0]
    hp_ref[SUBLANES:SUBLANES + rows, :] = h
    prev = hp_ref[SUBLANES - 1:SUBLANES - 1 + rows, :]
    if independent:
        own_shift = jnp.concatenate([jnp.broadcast_to(shift_ref[0, s:s + 1, :], (ch, A_COLS))
                                     for s in range(n_sub)], axis=0)
        prev = jnp.where(seg_row == 0, own_shift, prev)
    else:
        hp_ref[0:SUBLANES, :] = h[rows - SUBLANES:rows, :]
    xs = h + (prev - h) * mu_ref[...]

    r = xs[:, :W_A]
    k = xs[:, W_A:2 * W_A]
    v = xs[:, 2 * W_A:3 * W_A]
    o0 = 3 * W_A
    xw = xs[:, o0:o0 + LORA_W]
    xa = xs[:, o0 + LORA_W:o0 + LORA_W + LORA_A]
    xg = xs[:, o0 + LORA_W + LORA_A:]
    log_w = -RWKV_DECAY_SCALE * _sigmoid(w0_ref[...] + _dot(jnp.tanh(xw).astype(BF16), wl_ref[...]))
    a = _sigmoid(a0_ref[...] + _dot(xa.astype(BF16), al_ref[...]))
    g = _dot(_sigmoid(xg).astype(BF16), gl_ref[...])
    kk = k * kk_ref[...]
    kmod = k * (1.0 + (a - 1.0) * ka_ref[...])

    if t_valid is not None:
        tok = seg_row if independent else c * rows + lax.broadcasted_iota(jnp.int32, (rows, 1), 0)
        valid = tok < t_valid
        log_w = jnp.where(valid, log_w, 0.0)
        kk = jnp.where(valid, kk, 0.0)
        kmod = jnp.where(valid, kmod, 0.0)
    ones = ones_ref[...]
    kk = kk * lax.rsqrt(jnp.maximum(_head_sum(kk * kk, ones), 1e-12))
    b = kk * a

    ri = lax.broadcasted_iota(jnp.int32, (rows, rows), 0)
    ci = lax.broadcasted_iota(jnp.int32, (rows, rows), 1)
    shift = int(math.log2(ch))
    same = (ri >> shift) == (ci >> shift)
    lower = (same & (ci <= ri)).astype(BF16)
    whole = same.astype(BF16)
    lw_hi = log_w.astype(BF16)
    rem = log_w - lw_hi.astype(F32)
    lw_mid = rem.astype(BF16)
    lw_lo = (rem - lw_mid.astype(F32)).astype(BF16)
    cum = _dot(lower, lw_hi) + _dot(lower, lw_mid) + _dot(lower, lw_lo)
    cum_end = _dot(whole, lw_hi) + _dot(whole, lw_mid) + _dot(whole, lw_lo)
    p_in = jnp.exp(cum)
    p_inv = jnp.exp(-cum)
    to_end = jnp.exp(cum_end - cum)
    p_end = jnp.exp(cum_end)
    kkt = kk * jnp.exp(cum - log_w)
    bt = b * p_inv
    kt = kmod * p_inv
    rt = r * p_in
    b_end = b * to_end
    k_end = kmod * to_end

    row = lax.broadcasted_iota(jnp.int32, (ch, LANES), 0)
    lane = lax.broadcasted_iota(jnp.int32, (ch, LANES), 1)
    col = lane & (ch - 1)
    low = lane < HEAD_DIM
    strict = col < row
    incl = col <= row
    diag = col == row
    off = [((row >> (lvl + 1)) == (col >> (lvl + 1))) & (((row >> lvl) & 1) == 1) & (((col >> lvl) & 1) == 0)
           for lvl in range(shift)]
    halves = (jnp.where(low, 1.0, 0.0).astype(BF16), jnp.where(low, 0.0, 1.0).astype(BF16))
    blk = lambda pair: _block_diag(_split(pair), halves)
    side = lambda u, w: tuple(jnp.concatenate([p, q], axis=1) for p, q in zip(u, w))

    n_pair = W_A // LANES
    chains = [(slice(sub * ch, (sub + 1) * ch), slice(j * LANES, (j + 1) * LANES))
              for sub in range(n_sub) for j in range(n_pair)]
    each = lambda fn, *lists: [fn(*args) for args in zip(*lists)]
    cut = lambda z: [z[rs, sl] for rs, sl in chains]
    kkt_c, rt_c, v_c = cut(kkt), cut(rt), cut(v)
    v_bd = each(blk, v_c)

    def scores(x_kk, x_r, x_b, x_k):
        rhs = tuple(jnp.concatenate([pb * halves[0], pb * halves[1], pk * halves[0], pk * halves[1]], axis=0)
                    for pb, pk in zip(_split(x_b), _split(x_k)))
        return _mm3(_dot_nt, _split(jnp.concatenate([x_kk, x_r], axis=0)), rhs)

    aa = each(scores, kkt_c, rt_c, cut(bt), cut(kt))
    a_kb = each(lambda m: m[:ch, :LANES], aa)
    a_kb_s = each(_split, a_kb)
    a_kk = each(lambda m: _split(jnp.where(strict, m[:ch, LANES:], 0.0)), aa)
    a_rb = each(lambda m: _split(jnp.where(incl, m[ch:, :LANES], 0.0)), aa)
    a_rk = each(lambda m: _split(jnp.where(incl, m[ch:, LANES:], 0.0)), aa)
    t_inv = each(lambda m: jnp.where(diag, 1.0, 0.0) - jnp.where(off[0], m, 0.0), a_kb)
    for lvl in range(1, shift):
        off_b = jnp.where(off[lvl], 1.0, 0.0).astype(BF16)
        inner = each(lambda m, t: blk(_mm3(_dot, (m[0] * off_b, m[1] * off_b), blk(t))), a_kb_s, t_inv)
        t_inv = each(lambda t, inn: t - _mm3(_dot, _split(t), inn), t_inv, inner)
    rows2 = lambda u, w: tuple(jnp.concatenate([a, b], axis=0) for a, b in zip(u, w))
    av = each(lambda m, n, vb: _mm3(_dot, rows2(m, n), vb), a_kk, a_rk, v_bd)
    y = each(lambda t, x, w: _mm3(_dot, _split(t), side(blk(x), blk(w[:ch]))), t_inv, kkt_c, av)
    z = each(lambda m, yy: _mm3(_dot, m, side(blk(yy[:, :LANES]), blk(yy[:, LANES:]))), a_rb, y)
    gz = each(lambda m, yy: _mm3(_dot_tn, _split(m), _split(yy)), cut(b_end), y)
    q_eff = each(lambda x, zz: _split(x - zz[:, :LANES]), rt_c, z)
    o_loc = each(lambda w, zz: w[ch:] - zz[:, LANES:], av, z)
    g_mat = each(lambda pe, gg: _split(jnp.where(diag, jnp.broadcast_to(pe[0:1, :], (ch, LANES)), 0.0)
                                       - _pair_diag(gg[:, :LANES], low)), cut(p_end), gz)
    qg = each(rows2, q_eff, g_mat)
    h_mat = each(lambda m, x, gg: _pair_diag(_mm3(_dot_tn, _split(m), _split(x)), low)
                 - _pair_diag(gg[:, LANES:], low), cut(k_end), v_c, gz)

    if independent:
        state = [_pair_from_state(s0_ref[sub, 2 * j], s0_ref[sub, 2 * j + 1])
                 for sub in range(n_sub) for j in range(n_pair)]
    else:
        state = [st_ref[j] for j in range(n_pair)]
    for sub in range(n_sub):
        ids = range(sub * n_pair, (sub + 1) * n_pair)
        st_bd = each(blk, [state[i] for i in ids] if independent else state)
        both = [_mm3(_dot, qg[i], s) for i, s in zip(ids, st_bd)]
        for j, i in enumerate(ids):
            oraw_ref[sub * ch:(sub + 1) * ch, j * LANES:(j + 1) * LANES] = both[j][:ch] + o_loc[i]
        new = [w[ch:] + h_mat[i] for w, i in zip(both, ids)]
        if independent:
            for j in range(n_pair):
                _store_pair_state(sfin_ref, sub, j, new[j])
        else:
            state = new
    if not independent:
        for j in range(n_pair):
            st_ref[j] = state[j]

    o_raw = oraw_ref[...]
    oc = o_raw - _head_sum(o_raw, ones) * (1.0 / HEAD_DIM)
    var = _head_sum(oc * oc, ones) * (1.0 / HEAD_DIM)
    o_n = oc * lax.rsqrt(var + RWKV_GN_EPS) * gng_ref[...] + gnb_ref[...]
    bonus = _head_sum(r * kmod * rk_ref[...], ones)
    o_ref[0] = (o_n + bonus * v) * g

    if not independent:
        @pl.when(c == pl.num_programs(1) - 1)
        def _():
            for j in range(n_pair):
                _store_pair_state(sfin_ref, 0, j, st_ref[j])


def _rwkv(ha, shift_all, s0_all, st_idx, p, layer, t_valid, n_sub, independent):
    b, tp, _ = ha.shape
    rows = n_sub * RWKV_CHUNK
    if independent:
        assert tp == RWKV_CHUNK and b % n_sub == 0
        groups, nc, per = b // n_sub, 1, n_sub
        ha = ha.reshape(groups, rows, A_COLS)
    else:
        groups, nc, per = b, tp // rows, 1
    shift_all = shift_all.reshape(shift_all.shape[0], groups, per, A_COLS)
    st_spec = pl.BlockSpec((per, H_A, HEAD_DIM, HEAD_DIM), lambda i, c: (i, 0, 0, 0))
    lane_head = jnp.arange(W_A) // HEAD_DIM
    ones = (lane_head[:, None] == lane_head[None, :]).astype(BF16)
    names = ('rwkv_mu', 'rwkv_w0', 'rwkv_w_lora', 'rwkv_a0', 'rwkv_a_lora', 'rwkv_g_lora', 'rwkv_k_k',
             'rwkv_k_a', 'rwkv_r_k', 'rwkv_gn_g', 'rwkv_gn_b')
    kern = functools.partial(_rwkv_kernel, t_valid=None if t_valid == tp else t_valid, n_sub=n_sub,
                             independent=independent)
    o, s_new = pl.pallas_call(
        kern,
        grid=(groups, nc),
        in_specs=[pl.BlockSpec((1, rows, A_COLS), lambda i, c: (i, c, 0)),
                  pl.BlockSpec((None, 1, per, A_COLS), lambda i, c: (st_idx, i, 0, 0)),
                  pl.BlockSpec((None, per, H_A, HEAD_DIM, HEAD_DIM), lambda i, c: (st_idx, i, 0, 0, 0))] +
                 [_layer_spec(p[k], layer) for k in names] +
                 [pl.BlockSpec((W_A, W_A), lambda i, c: (0, 0))],
        out_specs=[pl.BlockSpec((1, rows, W_A), lambda i, c: (i, c, 0)), st_spec],
        out_shape=[jax.ShapeDtypeStruct((groups, nc * rows, W_A), F32),
                   jax.ShapeDtypeStruct((b, H_A, HEAD_DIM, HEAD_DIM), F32)],
        scratch_shapes=[pltpu.VMEM((rows + SUBLANES, A_COLS), F32),
                        pltpu.VMEM((W_A // LANES, HEAD_DIM, LANES), F32),
                        pltpu.VMEM((rows, W_A), F32)],
        compiler_params=_params(("parallel", "arbitrary")),
        name="rwkv7",
    )(ha, shift_all, s0_all, *[p[k] for k in names], ones)
    return o.reshape(b, tp, W_A), s_new


ATT_SEG = WIN_MAX
ATT_UNITS = 2


def _attn_prompt_kernel(q_ref, k_ref, v_ref, o_ref, m_ref, l_ref, n_ref):
    seg = pl.program_id(1)
    blk = ATT_BLK
    n_slab = W_B // LANES
    low = lax.broadcasted_iota(jnp.int32, (blk, LANES), 1) < HEAD_DIM
    qi = lax.broadcasted_iota(jnp.int32, (blk, blk), 0)
    ki = lax.broadcasted_iota(jnp.int32, (blk, blk), 1)
    bias_own = jnp.where(ki <= qi, 0.0, NEG)
    ok_prev = ki >= qi
    scale = HEAD_DIM ** -0.5

    def step(it, carry, *, win, dil, first, last):
        log_d = int(math.log2(dil))
        ds = (lambda s: pl.ds(s, blk, stride=dil)) if dil > 1 else (lambda s: pl.ds(s, blk))
        units = [it * ATT_UNITS + i for i in range(ATT_UNITS)]
        start = [(u >> log_d) * (blk * dil) + (u & (dil - 1)) for u in units]
        k_start = [seg * ATT_SEG + s for s in start]
        p_start = [jnp.maximum(s - win, 0) for s in k_start]
        bias = [jnp.concatenate([jnp.where(ok_prev, jnp.where(s >= win, 0.0, NEG), NEG), bias_own], axis=1)
                for s in k_start]
        tiles = [(i, j) for i in range(ATT_UNITS) for j in range(n_slab)]
        chains = [(t, i, j, half) for t, (i, j) in enumerate(tiles) for half in (0, 1)]
        both = lambda ref, i, j: jnp.concatenate([ref[j, ds(p_start[i]), :].astype(BF16),
                                                  ref[j, ds(k_start[i]), :].astype(BF16)], axis=0)
        q = [q_ref[j, ds(start[i]), :] * scale for i, j in tiles]
        k_cat = [both(k_ref, i, j) for i, j in tiles]
        v_cat = [both(v_ref, i, j) for i, j in tiles]
        qh = [jnp.where(low if half == 0 else ~low, q[t], 0.0).astype(BF16) for t, _, _, half in chains]
        s = [_dot_nt(x, k_cat[t]) + bias[i] for x, (t, i, _, _) in zip(qh, chains)]
        m = [jnp.max(x, axis=-1, keepdims=True) for x in s]
        p = [jnp.exp(x - mm) for x, mm in zip(s, m)]
        l = [jnp.sum(x, axis=-1, keepdims=True) for x in p]
        acc = [_dot(x.astype(BF16), v_cat[t]) for x, (t, _, _, _) in zip(p, chains)]
        for t, (i, j) in enumerate(tiles):
            rows = ds(start[i])
            m_t = jnp.where(low, m[2 * t], m[2 * t + 1])
            l_t = jnp.where(low, l[2 * t], l[2 * t + 1])
            n_t = jnp.where(low, acc[2 * t], acc[2 * t + 1])
            if not first:
                m_o = m_ref[j, rows, :]
                m_n = jnp.maximum(m_o, m_t)
                e_o, e_t = jnp.exp(m_o - m_n), jnp.exp(m_t - m_n)
                l_t = l_ref[j, rows, :] * e_o + l_t * e_t
                n_t = n_ref[j, rows, :] * e_o + n_t * e_t
                m_t = m_n
            if last:
                o_ref[j, rows, :] = n_t / l_t
            else:
                m_ref[j, rows, :] = m_t
                l_ref[j, rows, :] = l_t
                n_ref[j, rows, :] = n_t
        return carry

    order = sorted(DILATED, key=lambda wd: -wd[1])
    for idx, (win, dil) in enumerate(order):
        lax.fori_loop(0, ATT_SEG // (blk * ATT_UNITS),
                      functools.partial(step, win=win, dil=dil, first=idx == 0, last=idx == len(order) - 1), 0)


def _attn_prompt(hb_slabs, bsz, t):
    n_slab = W_B // LANES
    assert t % ATT_SEG == 0 and all(win // dil == ATT_BLK and ATT_SEG % win == 0 for win, dil in DILATED)
    nseg = t // ATT_SEG
    state = pltpu.VMEM((n_slab, ATT_SEG, LANES), F32)
    whole = lambda part: pl.BlockSpec((n_slab, t, LANES), lambda i, s: (part, i, 0))
    seg_spec = pl.BlockSpec((n_slab, ATT_SEG, LANES), lambda i, s: (0, i * nseg + s, 0))
    return pl.pallas_call(
        _attn_prompt_kernel,
        grid=(bsz, nseg),
        in_specs=[seg_spec, whole(1), whole(2)],
        out_specs=seg_spec,
        out_shape=jax.ShapeDtypeStruct((n_slab, bsz * t, LANES), F32),
        scratch_shapes=[state, state, state],
        compiler_params=_params(("parallel", "arbitrary")),
        name="attn_prompt",
    )(hb_slabs, hb_slabs, hb_slabs)


def _kv_tail_kernel(k_ref, v_ref, *rest):
    kt_ref, vt_ref = rest[-2:]
    for j in range(W_B // LANES):
        kt_ref[0, 0, j * LANES:(j + 1) * LANES, :] = k_ref[j].T
        vt_ref[0, 0, j * LANES:(j + 1) * LANES, :] = v_ref[j].T


def _kv_tail(hb_slabs, bsz, t, w_keep, layer, depth, prev):
    n_slab = W_B // LANES
    assert t % w_keep == 0
    per = t // w_keep
    tail = lambda part: pl.BlockSpec((n_slab, w_keep, LANES), lambda i: (part, i * per + per - 1, 0))
    out = pl.BlockSpec((1, 1, W_B, w_keep), lambda i: (layer, i, 0, 0))
    in_specs, args, aliases = [tail(1), tail(2)], [hb_slabs, hb_slabs], {}
    if prev is not None:
        in_specs += [pl.BlockSpec(memory_space=pl.ANY)] * 2
        args += list(prev)
        aliases = {2: 0, 3: 1}
    return pl.pallas_call(
        _kv_tail_kernel,
        grid=(bsz,),
        in_specs=in_specs,
        out_specs=[out, out],
        out_shape=[jax.ShapeDtypeStruct((depth, bsz, W_B, w_keep), F32)] * 2,
        input_output_aliases=aliases,
        compiler_params=_params(("parallel",)),
        name="kv_tail",
    )(*args)


def _attn_sample_kernel(hb_ref, ck_ref, cv_ref, *rest, t_new, rows):
    o_ref, nk_ref, nv_ref = rest[-3:]
    w_buf = ck_ref.shape[3]
    hb = hb_ref[0]
    q, k_new, v_new = hb[:, :W_B], hb[:, W_B:2 * W_B], hb[:, 2 * W_B:]
    k_t, v_t = ck_ref[0, 0], cv_ref[0, 0]
    scale = HEAD_DIM ** -0.5
    nrow = H_B * rows

    def iota3(shape, dim):
        return lax.broadcasted_iota(jnp.int32, (H_B, rows) + shape, dim).reshape((nrow,) + shape)

    head_of_lane = iota3((W_B,), 2) >> int(math.log2(HEAD_DIM))
    own_head = iota3((W_B,), 0) == head_of_lane
    q_bd = jnp.where(own_head, jnp.concatenate([q] * H_B, axis=0), 0.0).astype(BF16)
    s_c = _dot(q_bd, k_t.astype(BF16)) * scale
    s_n = _dot_nt(q_bd, k_new.astype(BF16)) * scale
    d_c = w_buf + iota3((w_buf,), 1) - iota3((w_buf,), 2)
    d_n = iota3((rows,), 1) - iota3((rows,), 2)
    new_ok = (d_n >= 0) & (iota3((rows,), 2) < t_new)
    ms, ls, pcs, pns = [], [], [], []
    for win, dil in DILATED:
        sc = jnp.where(((d_c & (dil - 1)) == 0) & (d_c <= win), s_c, NEG)
        sn = jnp.where(new_ok & ((d_n & (dil - 1)) == 0) & (d_n <= win), s_n, NEG)
        m = jnp.maximum(jnp.max(sc, axis=-1, keepdims=True), jnp.max(sn, axis=-1, keepdims=True))
        pc, pn = jnp.exp(sc - m), jnp.exp(sn - m)
        ms.append(m)
        ls.append(jnp.sum(pc, axis=-1, keepdims=True) + jnp.sum(pn, axis=-1, keepdims=True))
        pcs.append(pc.astype(BF16))
        pns.append(pn.astype(BF16))
    acc = _dot_nt(jnp.concatenate(pcs, axis=0), v_t.astype(BF16)) + \
        _dot(jnp.concatenate(pns, axis=0), v_new.astype(BF16))
    m_all = jnp.maximum(jnp.maximum(ms[0], ms[1]), ms[2])
    es = [jnp.exp(m - m_all) for m in ms]
    num = sum(e * acc[g * nrow:(g + 1) * nrow] for g, e in enumerate(es))
    den = sum(e * l for e, l in zip(es, ls))
    o_rows = jnp.where(own_head, num / den, 0.0)
    o_ref[0] = sum(o_rows[h * rows:(h + 1) * rows] for h in range(H_B))

    lane = lax.broadcasted_iota(jnp.int32, (rows, LANES), 1)
    tok = lax.broadcasted_iota(jnp.int32, (rows, LANES), 0)
    place = ((lane == tok + (LANES - t_new)) & (tok < t_new)).astype(BF16)

    def transposed_tail(x):
        hi = x.astype(BF16)
        rem = x - hi.astype(F32)
        mid = rem.astype(BF16)
        lo = (rem - mid.astype(F32)).astype(BF16)
        return _dot_tn(hi, place) + _dot_tn(mid, place) + _dot_tn(lo, place)

    tail_lane = lax.broadcasted_iota(jnp.int32, (W_B, LANES), 1) >= LANES - t_new
    for src, new, dst in ((k_t, k_new, nk_ref), (v_t, v_new, nv_ref)):
        rolled = pltpu.roll(src, w_buf - t_new, 1)
        dst[0, 0, :, 0:w_buf - LANES] = rolled[:, 0:w_buf - LANES]
        dst[0, 0, :, w_buf - LANES:] = jnp.where(tail_lane, transposed_tail(new), rolled[:, w_buf - LANES:])


def _attn_sample(hb, cache_kt, cache_vt, layer, t_new, prev):
    b, rows, _ = hb.shape
    depth, _, _, w_buf = cache_kt.shape
    cspec = pl.BlockSpec((1, 1, W_B, w_buf), lambda i: (layer, i, 0, 0))
    in_specs = [pl.BlockSpec((1, rows, B_COLS), lambda i: (i, 0, 0)), cspec, cspec]
    args = [hb, cache_kt, cache_vt]
    aliases = {}
    if prev is not None:
        in_specs += [pl.BlockSpec(memory_space=pl.ANY)] * 2
        args += list(prev)
        aliases = {3: 1, 4: 2}
    new_shape = jax.ShapeDtypeStruct((depth, b, W_B, w_buf), F32)
    return pl.pallas_call(
        functools.partial(_attn_sample_kernel, t_new=t_new, rows=rows),
        grid=(b,),
        in_specs=in_specs,
        out_specs=[pl.BlockSpec((1, rows, W_B), lambda i: (i, 0, 0)), cspec, cspec],
        out_shape=[jax.ShapeDtypeStruct((b, rows, W_B), F32), new_shape, new_shape],
        input_output_aliases=aliases,
        compiler_params=_params(("parallel",)),
        name="attn_sample",
    )(*args)


def _retention_kernel(q_ref, k_ref, v_ref, g_ref, r0_ref, gng_ref, gnb_ref, ones_ref, o_ref, rfin_ref, st_ref,
                      *, ch, rows, n_sub, independent):
    c = pl.program_id(1)
    n_pair = W_C // LANES
    pair_of = lambda ref, s, j: jnp.concatenate([ref[s, 2 * j], ref[s, 2 * j + 1]], axis=1)

    if not independent:
        @pl.when(c == 0)
        def _():
            for j in range(n_pair):
                st_ref[j] = pair_of(r0_ref, 0, j)

    qi = lax.broadcasted_iota(jnp.int32, (rows, rows), 0)
    kj = lax.broadcasted_iota(jnp.int32, (rows, rows), 1)
    rel = (qi - kj).astype(F32)
    pos = lax.broadcasted_iota(jnp.int32, (rows, LANES), 0).astype(F32)
    low = lax.broadcasted_iota(jnp.int32, (rows, LANES), 1) < HEAD_DIM
    low_st = lax.broadcasted_iota(jnp.int32, (HEAD_DIM, LANES), 1) < HEAD_DIM
    halves = (jnp.where(low_st, 1.0, 0.0).astype(BF16), jnp.where(low_st, 0.0, 1.0).astype(BF16))
    log_gamma = [math.log(1.0 - 2.0 ** (-5.0 - hd)) for hd in range(H_C)]
    dmask = [jnp.where(rel >= 0, jnp.exp(lg * jnp.maximum(rel, 0.0)), 0.0) for lg in log_gamma]
    lg_lane = [jnp.where(low, log_gamma[2 * j], log_gamma[2 * j + 1]) for j in range(n_pair)]
    lg_st = [jnp.where(low_st, log_gamma[2 * j], log_gamma[2 * j + 1]) for j in range(n_pair)]
    ones = ones_ref[...]

    blocks = [(sub, j) for sub in range(n_sub) for j in range(n_pair)]
    tile = lambda ref, sub, j: ref[sub, :, j * LANES:(j + 1) * LANES]
    q = [tile(q_ref, sub, j) for sub, j in blocks]
    k = [tile(k_ref, sub, j) for sub, j in blocks]
    v = [tile(v_ref, sub, j).astype(BF16) for sub, j in blocks]
    k_b = [x.astype(BF16) for x in k]
    att = [[_dot_nt(jnp.where(low if half == 0 else ~low, x, 0.0).astype(BF16), kb) * dmask[2 * j + half]
            for half in (0, 1)] for x, kb, (_, j) in zip(q, k_b, blocks)]
    o_intra = [jnp.where(low, _dot(a0.astype(BF16), vv), _dot(a1.astype(BF16), vv)) for (a0, a1), vv in zip(att, v)]
    q_dec = [(x * jnp.exp(lg_lane[j] * (pos + 1.0))).astype(BF16) for x, (_, j) in zip(q, blocks)]
    kv = [_pair_diag(_dot_tn((x * jnp.exp(lg_lane[j] * (ch - 1.0 - pos))).astype(BF16), vv), low_st)
          for x, vv, (_, j) in zip(k, v, blocks)]

    def store_state(s, j, pair):
        rfin_ref[s, 2 * j] = pair[:, :HEAD_DIM]
        rfin_ref[s, 2 * j + 1] = pair[:, HEAD_DIM:]

    state = None if independent else [st_ref[j] for j in range(n_pair)]
    outs = []
    for i, (sub, j) in enumerate(blocks):
        st = pair_of(r0_ref, sub, j) if independent else state[j]
        st_b = st.astype(BF16)
        st_bd = jnp.concatenate([st_b * halves[0], st_b * halves[1]], axis=0)
        outs.append(o_intra[i] + _dot(q_dec[i], st_bd))
        st = st * jnp.exp(lg_st[j] * ch) + kv[i]
        if independent:
            store_state(sub, j, st)
        else:
            state[j] = st
    if not independent:
        for j in range(n_pair):
            st_ref[j] = state[j]

    for o, (sub, j) in zip(outs, blocks):
        sl = slice(j * LANES, (j + 1) * LANES)
        oc = o - _head_sum(o, ones) * (1.0 / HEAD_DIM)
        var = _head_sum(oc * oc, ones) * (1.0 / HEAD_DIM)
        gate = tile(g_ref, sub, j)
        o_ref[sub, :, sl] = (oc * lax.rsqrt(var + LN_EPS) * gng_ref[:, sl] + gnb_ref[:, sl]) * (gate * _sigmoid(gate))

    if not independent:
        @pl.when(c == pl.num_programs(1) - 1)
        def _():
            for j in range(n_pair):
                store_state(0, j, st_ref[j])


def _retention(hc, r0_all, st_idx, p, layer, rows, ch, n_sub, independent):
    b, tp, _ = hc.shape
    if independent:
        assert tp == rows and b % n_sub == 0
        groups, nc, per = b // n_sub, 1, n_sub
    else:
        groups, nc, per = b, tp // (rows * n_sub), 1
    hc = hc.reshape(b * tp // rows, rows, C_COLS)
    part = lambda j: pl.BlockSpec((n_sub, rows, W_C), lambda i, c: (i * nc + c, 0, j))
    st_spec = pl.BlockSpec((per, H_C, HEAD_DIM, HEAD_DIM), lambda i, c: (i, 0, 0, 0))
    lane_head = jnp.arange(LANES) // HEAD_DIM
    ones = (lane_head[:, None] == lane_head[None, :]).astype(BF16)
    o, r_fin = pl.pallas_call(
        functools.partial(_retention_kernel, ch=ch, rows=rows, n_sub=n_sub, independent=independent),
        grid=(groups, nc),
        in_specs=[part(0), part(1), part(2), part(3),
                  pl.BlockSpec((None, per, H_C, HEAD_DIM, HEAD_DIM), lambda i, c: (st_idx, i, 0, 0, 0)),
                  _layer_spec(p['ret_gn_g'], layer), _layer_spec(p['ret_gn_b'], layer),
                  pl.BlockSpec((LANES, LANES), lambda i, c: (0, 0))],
        out_specs=[part(0), st_spec],
        out_shape=[jax.ShapeDtypeStruct((b * tp // rows, rows, W_C), F32),
                   jax.ShapeDtypeStruct((b, H_C, HEAD_DIM, HEAD_DIM), F32)],
        scratch_shapes=[pltpu.VMEM((W_C // LANES, HEAD_DIM, LANES), F32)],
        compiler_params=_params(("parallel", "arbitrary")),
        name="retention",
    )(hc, hc, hc, hc, r0_all, p['ret_gn_g'], p['ret_gn_b'], ones)
    return o.reshape(b, tp, W_C), r_fin


def _out_proj_kernel(oa_ref, ob_ref, oc_ref, x_ref, w_ref, g_ref, b_ref, y_ref, *, alpha):
    mixed = jnp.concatenate([oa_ref[...].astype(BF16)] + [ob_ref[j].astype(BF16) for j in range(W_B // LANES)] +
                            [oc_ref[...].astype(BF16)], axis=1)
    y_ref[...] = _layer_norm(alpha * x_ref[...] + _dot(mixed, w_ref[...]), g_ref[...], b_ref[...], LN_EPS)


def _out_proj(oa, ob, oc, x2d, w_bf16, p, layer, alpha, bm):
    n, d = x2d.shape
    row = lambda w: pl.BlockSpec((bm, w), lambda i: (i, 0))
    return pl.pallas_call(
        functools.partial(_out_proj_kernel, alpha=alpha),
        grid=(n // bm,),
        in_specs=[row(W_A), pl.BlockSpec((W_B // LANES, bm, LANES), lambda i: (0, i, 0)), row(W_C), row(d),
                  _layer_spec(w_bf16, layer), _layer_spec(p['ln1_g'], layer), _layer_spec(p['ln1_b'], layer)],
        out_specs=row(d),
        out_shape=jax.ShapeDtypeStruct((n, d), F32),
        compiler_params=_params(("parallel",)),
        name="out_proj",
    )(oa, ob, oc, x2d, w_bf16, p['ln1_g'], p['ln1_b'])


FFN_CHUNK = 256
FFN_ROWS = 1024


def _ffn_kernel(x_ref, wg_ref, wu_ref, wd_ref, g_ref, b_ref, y_ref, *, alpha):
    x = x_ref[...]
    xb = x.astype(BF16)
    d_ff = wg_ref.shape[1]
    acc = alpha * x
    for j in range(d_ff // FFN_CHUNK):
        cs = slice(j * FFN_CHUNK, (j + 1) * FFN_CHUNK)
        gate = _dot(xb, wg_ref[:, cs])
        up = _dot(xb, wu_ref[:, cs])
        act = gate * _sigmoid(gate) * up
        acc += _dot(act.astype(BF16), wd_ref[cs, :])
    y_ref[...] = _layer_norm(acc, g_ref[...], b_ref[...], LN_EPS)


def _ffn(x2d, wg, wu, wd, p, layer, alpha, bm):
    n, d = x2d.shape
    row = pl.BlockSpec((bm, d), lambda i: (i, 0))
    full = lambda w: _layer_spec(w, layer, pipeline_mode=pl.Buffered(1))
    return pl.pallas_call(
        functools.partial(_ffn_kernel, alpha=alpha),
        grid=(n // bm,),
        in_specs=[row, full(wg), full(wu), full(wd), _layer_spec(p['ln2_g'], layer), _layer_spec(p['ln2_b'], layer)],
        out_specs=row,
        out_shape=jax.ShapeDtypeStruct((n, d), F32),
        compiler_params=_params(("parallel",)),
        name="ffn",
    )(x2d, wg, wu, wd, p['ln2_g'], p['ln2_b'])


def _pad_rows(z, rows):
    return jnp.pad(z, ((0, 0), (0, rows - z.shape[1]), (0, 0)))


def _layer(x, state, pos0, p, wts, layer, alpha, prev_kv, caches=None):
    bsz, t, d = x.shape
    n = bsz * t
    depth = wts['w_in'].shape[0]
    x2d = x.reshape(n, d)
    bm = min(512, n)
    n_slab = W_B // LANES
    pos = pos0 + jnp.arange(t, dtype=F32)
    reps = max(bm // t, 1)
    tile = lambda tabs: tuple(jnp.tile(z, (reps, 1)) for z in tabs)
    inv_b = ROPE_THETA ** (-jnp.arange(0, ROT_DIM, 2, dtype=F32) / ROT_DIM)
    inv_c = 1.0 / (RET_THETA ** jnp.linspace(0.0, 1.0, HEAD_DIM // 2, dtype=F32))
    ha, hb_slabs, hc = _in_proj(x2d, wts['w_in'], layer, tile(_rot_tables(pos, inv_b, ROT_DIM)),
                                tile(_rot_tables(pos, inv_c, HEAD_DIM)), bm)
    ha = ha.reshape(bsz, t, A_COLS)
    hc = hc.reshape(bsz, t, C_COLS)
    shift_new = ha[:, -1]
    shift_all, wkv_all, ret_all, st_idx = state

    tp = -(-t // RWKV_CHUNK) * RWKV_CHUNK
    if tp % (RWKV_SUBCHUNKS * RWKV_CHUNK) == 0:
        n_sub, independent = RWKV_SUBCHUNKS, False
    elif tp == RWKV_CHUNK and bsz % RWKV_SEQS == 0:
        n_sub, independent = RWKV_SEQS, True
    else:
        n_sub, independent = 1, False
    o_a, wkv_new = _rwkv(_pad_rows(ha, tp), shift_all, wkv_all, st_idx, p, layer, t, n_sub, independent)
    o_a = o_a[:, :t].reshape(n, W_A)

    if caches is None:
        o_b = _attn_prompt(hb_slabs, bsz, t)
        k_keep, v_keep = _kv_tail(hb_slabs, bsz, t, min(WIN_MAX, t), layer, depth, prev_kv)
    else:
        rows = -(-t // SUBLANES) * SUBLANES
        hb = jnp.swapaxes(hb_slabs, 0, 1).reshape(bsz, t, B_COLS)
        o_b, k_keep, v_keep = _attn_sample(_pad_rows(hb, rows), caches[0], caches[1], layer, t, prev_kv)
        o_b = jnp.swapaxes(o_b[:, :t].reshape(n, n_slab, LANES), 0, 1)

    if t % RET_CHUNK == 0:
        o_c, ret_new = _retention(hc, ret_all, st_idx, p, layer, RET_CHUNK, RET_CHUNK,
                                  RET_SUBCHUNKS if t % (RET_SUBCHUNKS * RET_CHUNK) == 0 else 1, False)
    else:
        rows = -(-t // SUBLANES) * SUBLANES
        seqs = RET_SEQS if bsz % RET_SEQS == 0 else 1
        o_c, ret_new = _retention(_pad_rows(hc, rows), ret_all, st_idx, p, layer, rows, t, seqs, seqs > 1)
        o_c = o_c[:, :t]
    o_c = o_c.reshape(n, W_C)

    x1 = _out_proj(o_a, o_b, o_c, x2d, wts['w_out'], p, layer, alpha, bm)
    x2 = _ffn(x1, wts['w_ffn_gate'], wts['w_ffn_up'], wts['w_ffn_down'], p, layer, alpha, min(FFN_ROWS, n))
    return x2.reshape(bsz, t, d), (shift_new, wkv_new, k_keep, v_keep, ret_new)


def _token_major(z):
    z = z.reshape(z.shape[:-2] + (H_B, HEAD_DIM, z.shape[-1]))
    return jnp.moveaxis(z, -1, -3)


def kernel(x_prompt, x_sample, state_rwkv_shift, state_rwkv_wkv, cache_win_k, cache_win_v, state_ret, w_in, rwkv_mu, rwkv_w0, rwkv_w_lora, rwkv_a0, rwkv_a_lora, rwkv_g_lora, rwkv_k_k, rwkv_k_a, rwkv_r_k, rwkv_gn_g, rwkv_gn_b, ret_gn_g, ret_gn_b, w_out, ln1_g, ln1_b, w_ffn_gate, w_ffn_up, w_ffn_down, ln2_g, ln2_b):
    depth = w_in.shape[0]
    alpha = (2 * depth) ** 0.25
    vectors = dict(rwkv_mu=rwkv_mu, rwkv_w0=rwkv_w0, rwkv_a0=rwkv_a0, rwkv_k_k=rwkv_k_k, rwkv_k_a=rwkv_k_a,
                   rwkv_r_k=rwkv_r_k, rwkv_gn_g=rwkv_gn_g, rwkv_gn_b=rwkv_gn_b, ret_gn_g=ret_gn_g,
                   ret_gn_b=ret_gn_b, ln1_g=ln1_g, ln1_b=ln1_b, ln2_g=ln2_g, ln2_b=ln2_b)
    p = {k: v.reshape(depth, 1, -1) for k, v in vectors.items()}
    p.update(rwkv_w_lora=rwkv_w_lora.astype(BF16), rwkv_a_lora=rwkv_a_lora.astype(BF16),
             rwkv_g_lora=rwkv_g_lora.astype(BF16))
    wts = {'w_in': w_in.astype(BF16), 'w_out': w_out.astype(BF16), 'w_ffn_gate': w_ffn_gate.astype(BF16),
           'w_ffn_up': w_ffn_up.astype(BF16), 'w_ffn_down': w_ffn_down.astype(BF16)}
    bp = x_prompt.shape[0]
    empty = (jnp.zeros((1, bp, A_COLS), F32), jnp.zeros((1, bp, H_A, HEAD_DIM, HEAD_DIM), F32),
             jnp.zeros((1, bp, H_C, HEAD_DIM, HEAD_DIM), F32), 0)
    to_minor = lambda c: jnp.moveaxis(c, 2, -1).reshape(c.shape[:2] + (W_B, c.shape[2]))
    caches = (to_minor(cache_win_k), to_minor(cache_win_v))
    xp, xs = x_prompt, x_sample
    p_states, s_states = [], []
    p_kv = s_kv = None
    for l in range(depth):
        xp, sp = _layer(xp, empty, 0.0, p, wts, l, alpha, p_kv)
        xs, ss = _layer(xs, (state_rwkv_shift, state_rwkv_wkv, state_ret, l), float(PAST_LEN), p, wts, l, alpha,
                        s_kv, caches=caches)
        p_kv, s_kv = (sp[2], sp[3]), (ss[2], ss[3])
        p_states.append(sp)
        s_states.append(ss)
    stack = lambda states, j: jnp.stack([s[j] for s in states])
    outs = []
    for states, kv in ((p_states, p_kv), (s_states, s_kv)):
        outs += [stack(states, 0), stack(states, 1), _token_major(kv[0]), _token_major(kv[1]), stack(states, 4)]
    return (xp, xs) + tuple(outs)
```
